```python
import math
import jax
import jax.numpy as jnp
from jax import lax
import numpy as np

D_MODEL = 2048
BATCH = 32
SEQ = 256
DEPTH = 4
DEC_BATCH = 8
DEC_SEQ = 2048
PAST_LEN = 256

GRID_W = 64
N_BRANCH = 4
MIX_W = D_MODEL // N_BRANCH
BLOCK = 128
ROPE_BASE = 10000.0
EPS = 1e-6
NEG = -1e30
LB_FLOOR = 1e-30
N_MOD = 9
D_FF = 5632
A_HEADS = 4
A_KV_HEADS = 2
A_GROUP = A_HEADS // A_KV_HEADS
A_HD = MIX_W // A_HEADS
A_WINDOW = 128
B_HEADS = 4
B_HD = MIX_W // (2 * B_HEADS)
HY_W = MIX_W
HY_ORDER = 2
HY_EMB = 33
HY_FFN = 64
HY_SIN_LAYERS = 2
HY_TARGET = 1e-2
HY_FAST = 0.3
HY_SLOW = 1.5
HG_HEADS = 4
HG_DK = MIX_W // HG_HEADS
HG_DV = HG_DK
HG_CHUNK = 32
IN_SIZES = (A_HEADS * A_HD, A_KV_HEADS * A_HD, A_KV_HEADS * A_HD,
            2 * B_HEADS * B_HD, 2 * B_HEADS * B_HD, 2 * B_HEADS * B_HD,
            3 * HY_W, 5 * MIX_W, N_BRANCH * D_MODEL)
IN_TOTAL = sum(IN_SIZES)

kernel_name = 'hybrid_diffusion_prefix_step'

F32 = jnp.float32


def rmsnorm(x, g):
    xf = x.astype(F32)
    y = xf * lax.rsqrt(jnp.mean(xf * xf, axis=-1, keepdims=True) + EPS)
    return (y * g.astype(F32)).astype(x.dtype)


def split_cols(a, sizes):
    out, start = [], 0
    for s in sizes:
        out.append(a[..., start:start + s])
        start += s
    return out


def swiglu(h, w_gu, w_down):
    a, b = jnp.split(h @ w_gu, 2, axis=-1)
    return (jax.nn.silu(a) * b) @ w_down


def axial_rope_tables(L, rot_dim):
    n_rows = L // GRID_W
    rows = jnp.broadcast_to(jnp.arange(n_rows, dtype=F32)[:, None], (n_rows, GRID_W)).reshape(-1)
    cols = jnp.broadcast_to(jnp.arange(GRID_W, dtype=F32)[None, :], (n_rows, GRID_W)).reshape(-1)
    axis_dim = rot_dim // 2
    inv = ROPE_BASE ** (-jnp.arange(0, axis_dim, 2, dtype=F32) / axis_dim)
    ang = jnp.concatenate([rows[:, None] * inv, cols[:, None] * inv], axis=-1)
    return jnp.cos(ang), jnp.sin(ang)


def apply_rope(x, cos, sin):
    shp = (cos.shape[0],) + (1,) * (x.ndim - 3) + (cos.shape[1],)
    c, s = cos.reshape(shp), sin.reshape(shp)
    xf = x.astype(F32)
    x1, x2 = xf[..., 0::2], xf[..., 1::2]
    return jnp.stack([x1 * c - x2 * s, x1 * s + x2 * c], axis=-1).reshape(x.shape).astype(x.dtype)


def context_sink_attn(q, k, v, sink):
    B, L, KVH, G, d = q.shape
    nb = L // BLOCK
    scale = d ** -0.5
    qb = q.reshape(B, nb, BLOCK, KVH, G, d).swapaxes(0, 1)
    sink_f = sink.astype(F32)[None, :, :, None, None]

    def one(qi):
        s = jnp.einsum('bqkgd,bskd->bkgqs', qi, k).astype(F32) * scale
        s_sink = jnp.broadcast_to(sink_f, s.shape[:-1] + (1,))
        p = jax.nn.softmax(jnp.concatenate([s_sink, s], axis=-1), axis=-1)[..., 1:]
        return jnp.einsum('bkgqs,bskd->bqkgd', p.astype(v.dtype), v)

    o = lax.map(one, qb)
    return o.swapaxes(0, 1).reshape(B, L, KVH * G * d)


def latent_window_attn(q, k, v, kc, vc, sink):
    B, L, KVH, G, d = q.shape
    nb = L // BLOCK
    scale = d ** -0.5

    def band(a):
        ap = jnp.pad(a, ((0, 0), (BLOCK, BLOCK), (0, 0), (0, 0)))
        ab = ap.reshape(B, nb + 2, BLOCK, KVH, d)
        return jnp.concatenate([ab[:, :-2], ab[:, 1:-1], ab[:, 2:]], axis=2)

    kw, vw = band(k), band(v)
    qb = q.reshape(B, nb, BLOCK, KVH, G, d)
    s_loc = jnp.einsum('bnqkgd,bnskd->bnkgqs', qb, kw).astype(F32) * scale
    s_ctx = jnp.einsum('bnqkgd,bskd->bnkgqs', qb, kc).astype(F32) * scale
    blk = jnp.arange(nb)[:, None]
    qpos = blk * BLOCK + jnp.arange(BLOCK)[None, :]
    kpos = (blk - 1) * BLOCK + jnp.arange(3 * BLOCK)[None, :]
    kp = kpos[:, None, :]
    valid = (kp >= 0) & (kp < L) & (jnp.abs(qpos[:, :, None] - kp) <= A_WINDOW)
    s_loc = jnp.where(valid[None, :, None, None], s_loc, NEG)
    s_sink = jnp.broadcast_to(sink.astype(F32)[None, None, :, :, None, None], s_loc.shape[:-1] + (1,))
    p = jax.nn.softmax(jnp.concatenate([s_sink, s_ctx, s_loc], axis=-1), axis=-1).astype(v.dtype)
    P = kc.shape[1]
    o = (jnp.einsum('bnkgqs,bskd->bnqkgd', p[..., 1:1 + P], vc)
         + jnp.einsum('bnkgqs,bnskd->bnqkgd', p[..., 1 + P:], vw))
    return o.reshape(B, L, KVH * G * d)


def diff_attn(q, k, v, lam, g, lam_init):
    B, L, H, _, d = q.shape
    nb = L // BLOCK
    scale = d ** -0.5
    qb = q.reshape(B, nb, BLOCK, H, 2, d).swapaxes(0, 1)

    def one(qi):
        s = jnp.einsum('bqhcd,bshcd->bhcqs', qi, k).astype(F32) * scale
        p = jax.nn.softmax(s, axis=-1)
        a = p[:, :, 0] - lam * p[:, :, 1]
        return jnp.einsum('bhqs,bshe->bqhe', a.astype(v.dtype), v)

    o = lax.map(one, qb).swapaxes(0, 1).reshape(B, L, H, 2 * d)
    o = rmsnorm(o, g) * (1.0 - lam_init)
    return o.reshape(B, L, H * 2 * d)


def short_conv3(u, w, b):
    up = jnp.pad(u, ((0, 0), (1, 1), (0, 0)))
    return up[:, :-2] * w[0] + up[:, 1:-1] * w[1] + up[:, 2:] * w[2] + b


def hyena_kernels(L, w1, b1, w2, b2, w3, freq):
    t = jnp.linspace(0.0, 1.0, L, dtype=F32)[:, None]
    bands = (HY_EMB - 1) // 2
    w = 2.0 * math.pi * jnp.arange(L, dtype=F32)[:, None] / L
    fr = jnp.linspace(1e-4, bands - 1, bands, dtype=F32)[None, :]
    feats = jnp.concatenate([t, jnp.cos(fr * w), -jnp.sin(fr * w)], axis=-1)
    h = jnp.sin(freq[0] * (feats @ w1 + b1))
    h = jnp.sin(freq[1] * (h @ w2 + b2))
    h = (h @ w3).astype(F32).reshape(L, HY_ORDER, 2, HY_W)
    deltas = jnp.abs(jnp.linspace(math.log(HY_TARGET) / HY_SLOW, math.log(HY_TARGET) / HY_FAST, HY_W, dtype=F32))
    h = h * jnp.exp(-t * deltas)[:, None, None, :]
    fwd, bwd = h[:, :, 0], h[:, :, 1]
    kern = jnp.concatenate([fwd, jnp.zeros((1, HY_ORDER, HY_W), F32), bwd[:0:-1]], axis=0)
    return kern / (jnp.sum(jnp.abs(kern), axis=0, keepdims=True) + EPS)


def long_conv(z, kern, bias):
    L = z.shape[1]
    zf = jnp.fft.rfft(z.astype(F32), n=2 * L, axis=1)
    kf = jnp.fft.rfft(kern, n=2 * L, axis=0)
    y = jnp.fft.irfft(zf * kf[None], n=2 * L, axis=1)[:, :L]
    return (y + z.astype(F32) * bias.astype(F32)).astype(z.dtype)


def hyena(u, conv_w, conv_b, w1, b1, w2, b2, w3, freq, bias):
    L = u.shape[1]
    v, x1, x2 = jnp.split(short_conv3(u, conv_w, conv_b), 3, axis=-1)
    kern = hyena_kernels(L, w1, b1, w2, b2, w3, freq)
    y = x1 * long_conv(v, kern[:, 0], bias[0])
    return x2 * long_conv(y, kern[:, 1], bias[1])


def hgrn_scan(q, k, log_f, v, s0):
    B, L, H, dk = q.shape
    dv = v.shape[-1]
    C = HG_CHUNK
    nc = L // C

    def chunks(a):
        return a.reshape(B, nc, C, H, a.shape[-1]).transpose(1, 0, 3, 2, 4)

    causal = jnp.tril(jnp.ones((C, C), dtype=bool))[:, :, None]

    def step(S, xs):
        qc, kc, gc, vc = xs
        b = jnp.cumsum(gc, axis=2)
        o_inter = jnp.einsum('bhtk,bhkv->bhtv', qc * jnp.exp(b), S)
        rel = b[:, :, :, None, :] - b[:, :, None, :, :]
        decay = jnp.where(causal, jnp.exp(jnp.where(causal, rel, 0.0)), 0.0)
        a = jnp.einsum('bhtk,bhsk,bhtsk->bhts', qc, kc, decay)
        o_intra = jnp.einsum('bhts,bhsv->bhtv', a, vc)
        b_last = b[:, :, -1:, :]
        S = (jnp.exp(b_last[:, :, 0, :])[..., None] * S
             + jnp.einsum('bhsk,bhsv->bhkv', kc * jnp.exp(b_last - b), vc))
        return S, o_inter + o_intra

    S, o = lax.scan(step, s0.astype(F32), (chunks(q), chunks(k), chunks(log_f), chunks(v)))
    return o.transpose(1, 0, 3, 2, 4).reshape(B, L, H, dv), S


def hgrn_branch(u, lb, norm_g, s0):
    B, L, _ = u.shape
    xq, xff, xfb, xi, xg = jnp.split(u, 5, axis=-1)
    shp = (B, L, HG_HEADS, HG_DK)
    q = jax.nn.silu(xq.astype(F32)).reshape(shp)
    v = xi.astype(F32).reshape(B, L, HG_HEADS, HG_DV)

    def gate(xf, lb_d):
        z = xf.astype(F32).reshape(shp)
        log_lb = jnp.log(jnp.maximum(lb_d, LB_FLOOR))
        log_f = jnp.logaddexp(jax.nn.log_sigmoid(z), log_lb + jax.nn.log_sigmoid(-z))
        return log_f, (1.0 - lb_d) * jax.nn.sigmoid(-z)

    g_f, k_f = gate(xff, lb[0])
    g_b, k_b = gate(xfb, lb[1])
    o_f, s_f = hgrn_scan(q, k_f, g_f, v, s0[:, 0])
    o_b, s_b = hgrn_scan(q[:, ::-1], k_b[:, ::-1], g_b[:, ::-1], v[:, ::-1], s0[:, 1])
    o = rmsnorm(o_f + o_b[:, ::-1], norm_g) * jax.nn.silu(xg.astype(F32)).reshape(B, L, HG_HEADS, HG_DV)
    return o.reshape(B, L, MIX_W).astype(u.dtype), jnp.stack([s_f, s_b], axis=1)


def mixer(h, ctx, rope_a, rope_b, lam_init, lb, mw):
    (w_in, w_branch, w_out, a_sink, b_lambda, b_subln, hy_conv_w, hy_conv_b,
     hy_w1, hy_b1, hy_w2, hy_b2, hy_w3, hy_freq, hy_bias, hg_norm) = mw
    B, L, _ = h.shape
    aq, ak, av, bq, bk, bv, hy, hg, gt = split_cols(h @ w_in, IN_SIZES)
    qa = aq.reshape(B, L, A_KV_HEADS, A_GROUP, A_HD)
    ka = ak.reshape(B, L, A_KV_HEADS, A_HD)
    va = av.reshape(B, L, A_KV_HEADS, A_HD)
    qb = bq.reshape(B, L, B_HEADS, 2, B_HD)
    kb = bk.reshape(B, L, B_HEADS, 2, B_HD)
    vb = bv.reshape(B, L, B_HEADS, 2 * B_HD)
    sink = a_sink.reshape(A_KV_HEADS, A_GROUP)
    lf = b_lambda.astype(F32)
    lam = jnp.exp(jnp.sum(lf[0] * lf[1])) - jnp.exp(jnp.sum(lf[2] * lf[3])) + lam_init
    if ctx is None:
        o_a = context_sink_attn(qa, ka, va, sink)
        o_b = diff_attn(qb, kb, vb, lam, b_subln, lam_init)
        s0 = jnp.zeros((B, 2, HG_HEADS, HG_DK, HG_DV), F32)
    else:
        ka_c, va_c, kb_c, vb_c, s0 = ctx
        qa = apply_rope(qa, *rope_a)
        ka = apply_rope(ka, *rope_a)
        qb = apply_rope(qb, *rope_b)
        kb = apply_rope(kb, *rope_b)
        o_a = latent_window_attn(qa, ka, va, ka_c, va_c, sink)
        o_b = diff_attn(qb, jnp.concatenate([kb_c, kb], axis=1), jnp.concatenate([vb_c, vb], axis=1),
                        lam, b_subln, lam_init)
    o_c = hyena(hy, hy_conv_w, hy_conv_b, hy_w1, hy_b1, hy_w2, hy_b2, hy_w3, hy_freq, hy_bias)
    o_d, s_fin = hgrn_branch(hg, lb, hg_norm, s0)
    gates = jax.nn.sigmoid(gt).reshape(B, L, N_BRANCH, D_MODEL)
    merged = gates[:, :, 0] * (o_a @ w_branch[0])
    for n, o in ((1, o_b), (2, o_c), (3, o_d)):
        merged = merged + gates[:, :, n] * (o @ w_branch[n])
    out = merged @ w_out
    ctx_out = (ka, va, kb, vb, s_fin) if ctx is None else None
    return out, ctx_out


def layer(x, cond, ctx, rope_a, rope_b, l, lb, lw, mw):
    w_mod, b_mod, norm_g, w1_gu, w1_dn, w2_gu, w2_dn = lw
    mod = (jax.nn.silu(cond) @ w_mod + b_mod).reshape(cond.shape[0], 1, N_MOD, D_MODEL)
    sh1, sc1, ga1, sh2, sc2, ga2, sh3, sc3, ga3 = [mod[:, :, i] for i in range(N_MOD)]
    lam_init = 0.8 - 0.6 * math.exp(-0.3 * l)
    h = rmsnorm(x, norm_g[0]) * (1.0 + sc1) + sh1
    x = x + 0.5 * ga1 * swiglu(h, w1_gu, w1_dn)
    h = rmsnorm(x, norm_g[1]) * (1.0 + sc2) + sh2
    mix, ctx_out = mixer(h, ctx, rope_a, rope_b, lam_init, lb, mw)
    x = x + ga2 * mix
    h = rmsnorm(x, norm_g[2]) * (1.0 + sc3) + sh3
    x = x + 0.5 * ga3 * swiglu(h, w2_gu, w2_dn)
    return x, ctx_out


def setup_inputs(seed: int = 0) -> dict:
    key = jax.random.key(seed)
    ks = jax.random.split(key, 40)
    D = D_MODEL

    def nrm(i, shape, scale):
        return jax.random.normal(ks[i], shape, F32) * scale

    return {
        'x_prompt': nrm(0, (BATCH, SEQ, D), 1.0),
        'x_sample': nrm(1, (DEC_BATCH, DEC_SEQ, D), 1.0),
        'c': nrm(2, (DEC_BATCH, D), 1.0),
        'cache_a_k': nrm(3, (DEC_BATCH, DEPTH, PAST_LEN, A_KV_HEADS, A_HD), 1.0),
        'cache_a_v': nrm(4, (DEC_BATCH, DEPTH, PAST_LEN, A_KV_HEADS, A_HD), 1.0),
        'cache_b_k': nrm(5, (DEC_BATCH, DEPTH, PAST_LEN, B_HEADS, 2, B_HD), 1.0),
        'cache_b_v': nrm(6, (DEC_BATCH, DEPTH, PAST_LEN, B_HEADS, 2 * B_HD), 1.0),
        'state_hgrn': nrm(7, (DEC_BATCH, DEPTH, 2, HG_HEADS, HG_DK, HG_DV), 0.5),
        'c_ctx': nrm(8, (D,), 1.0),
        'w_mod': nrm(9, (DEPTH, D, N_MOD * D), 0.5 * D ** -0.5),
        'b_mod': nrm(10, (DEPTH, N_MOD * D), 0.01),
        'norm_g': 1.0 + nrm(11, (DEPTH, 3, D), 0.05),
        'w_ffn1_gu': nrm(12, (DEPTH, D, 2 * D_FF), D ** -0.5),
        'w_ffn1_down': nrm(13, (DEPTH, D_FF, D), D_FF ** -0.5),
        'w_ffn2_gu': nrm(14, (DEPTH, D, 2 * D_FF), D ** -0.5),
        'w_ffn2_down': nrm(15, (DEPTH, D_FF, D), D_FF ** -0.5),
        'w_in': nrm(16, (DEPTH, D, IN_TOTAL), D ** -0.5),
        'w_branch': nrm(17, (DEPTH, N_BRANCH, MIX_W, D), MIX_W ** -0.5),
        'w_out': nrm(18, (DEPTH, D, D), D ** -0.5),
        'a_sink': nrm(19, (DEPTH, A_HEADS), 0.5),
        'b_lambda': nrm(20, (DEPTH, 4, B_HD), 0.1),
        'b_subln': 1.0 + nrm(21, (DEPTH, 2 * B_HD), 0.05),
        'hy_conv_w': nrm(22, (DEPTH, 3, 3 * HY_W), 3 ** -0.5),
        'hy_conv_b': nrm(23, (DEPTH, 3 * HY_W), 0.01),
        'hy_w1': nrm(24, (DEPTH, HY_EMB, HY_FFN), HY_EMB ** -0.5),
        'hy_b1': nrm(25, (DEPTH, HY_FFN), 0.1),
        'hy_w2': nrm(26, (DEPTH, HY_FFN, HY_FFN), HY_FFN ** -0.5),
        'hy_b2': nrm(27, (DEPTH, HY_FFN), 0.1),
        'hy_w3': nrm(28, (DEPTH, HY_FFN, HY_ORDER * 2 * HY_W), HY_FFN ** -0.5),
        'hy_freq': 1.0 + nrm(29, (DEPTH, HY_SIN_LAYERS, HY_FFN), 0.1),
        'hy_bias': nrm(30, (DEPTH, HY_ORDER, HY_W), 0.5),
        'hg_lb': nrm(31, (DEPTH, 2, MIX_W), 1.0),
        'hg_norm': 1.0 + nrm(32, (DEPTH, HG_DV), 0.05),
        'final_g': 1.0 + nrm(33, (D,), 0.05),
    }


def reference(x_prompt, x_sample, c, cache_a_k, cache_a_v, cache_b_k, cache_b_v, state_hgrn,
              c_ctx, w_mod, b_mod, norm_g, w_ffn1_gu, w_ffn1_down, w_ffn2_gu, w_ffn2_down,
              w_in, w_branch, w_out, a_sink, b_lambda, b_subln, hy_conv_w, hy_conv_b,
              hy_w1, hy_b1, hy_w2, hy_b2, hy_w3, hy_freq, hy_bias, hg_lb, hg_norm, final_g):
    sm = jax.nn.softmax(hg_lb.astype(F32), axis=0)
    lb_all = jnp.maximum(jnp.cumsum(sm, axis=0) - sm[:1], 0.0).reshape(DEPTH, 2, HG_HEADS, HG_DK)
    L_lat = x_sample.shape[1]
    rope_a = axial_rope_tables(L_lat, A_HD)
    rope_b = axial_rope_tables(L_lat, B_HD)
    cond_ctx = c_ctx[None, :]
    xp, xs = x_prompt, x_sample
    ak_l, av_l, bk_l, bv_l, st_l = [], [], [], [], []
    for l in range(DEPTH):
        lw = (w_mod[l], b_mod[l], norm_g[l], w_ffn1_gu[l], w_ffn1_down[l], w_ffn2_gu[l], w_ffn2_down[l])
        mw = (w_in[l], w_branch[l], w_out[l], a_sink[l], b_lambda[l], b_subln[l], hy_conv_w[l], hy_conv_b[l],
              hy_w1[l], hy_b1[l], hy_w2[l], hy_b2[l], hy_w3[l], hy_freq[l], hy_bias[l], hg_norm[l])
        xp, ctx_new = layer(xp, cond_ctx, None, None, None, l, lb_all[l], lw, mw)
        ka, va, kb, vb, st = ctx_new
        ak_l.append(ka)
        av_l.append(va)
        bk_l.append(kb)
        bv_l.append(vb)
        st_l.append(st)
        cache_l = (cache_a_k[:, l], cache_a_v[:, l], cache_b_k[:, l], cache_b_v[:, l], state_hgrn[:, l])
        xs, _ = layer(xs, c, cache_l, rope_a, rope_b, l, lb_all[l], lw, mw)
    y_prompt = rmsnorm(xp, final_g)
    y_sample = rmsnorm(xs, final_g)
    new_a_k = jnp.stack(ak_l, axis=1)
    new_a_v = jnp.stack(av_l, axis=1)
    new_b_k = jnp.stack(bk_l, axis=1)
    new_b_v = jnp.stack(bv_l, axis=1)
    new_state_hgrn = jnp.stack(st_l, axis=1)
    return (y_prompt, y_sample, new_a_k, new_a_v, new_b_k, new_b_v, new_state_hgrn)
```

```python
import functools
import math

import jax
import jax.numpy as jnp
import numpy as np
from jax import lax
from jax.experimental import pallas as pl
from jax.experimental.pallas import tpu as pltpu

F32 = jnp.float32
BF16 = jnp.bfloat16

EPS = 1e-6
NEG = -1e30
LB_FLOOR = 1e-30
ROPE_BASE = 10000.0
GRID_W = 64
N_MOD = 9
N_BRANCH = 4
HD = 128
A_WINDOW = 128
HY_EMB = 33
HY_TARGET, HY_FAST, HY_SLOW = 1e-2, 0.3, 1.5
CHUNK = 128
SUB = 8
MOD_ROWS = 16
VMEM_MB = 2 ** 20


def _cp(sem, vmem_mb):
    return pltpu.CompilerParams(dimension_semantics=sem, vmem_limit_bytes=vmem_mb * VMEM_MB)


def _dot(a, b):
    return jnp.dot(a, b, preferred_element_type=F32)


def _dot_nt(a, b):
    return lax.dot_general(a, b, (((1,), (1,)), ((), ())), preferred_element_type=F32)


def _dot_tn(a, b):
    return lax.dot_general(a, b, (((0,), (0,)), ((), ())), preferred_element_type=F32)


def _split(a):
    hi = a.astype(BF16)
    lo = (a - hi.astype(F32)).astype(BF16)
    return hi, lo


def _dot3(a, b):
    ah, al = _split(a)
    bh, bl = _split(b)
    return _dot(ah, bh) + _dot(ah, bl) + _dot(al, bh)


def _sigmoid(x):
    return jax.nn.sigmoid(x)


def _norm_mod(x, g, sc, sh):
    y = x * lax.rsqrt(jnp.mean(x * x, axis=-1, keepdims=True) + EPS)
    return (y * g) * (1.0 + sc) + sh


def _rms_lanes(x, g):
    return x * lax.rsqrt(jnp.mean(x * x, axis=-1, keepdims=True) + EPS) * g


def _mod_body(c_ref, w_ref, b_ref, o_ref):
    c = c_ref[...]
    a = (c * _sigmoid(c)).astype(BF16)
    o_ref[...] = _dot(a, w_ref[...].astype(BF16)) + b_ref[...]


def modulation(cond, w_mod, b_mod, tn=1024):
    depth, d, n = w_mod.shape
    return pl.pallas_call(
        _mod_body,
        grid=(depth, n // tn),
        in_specs=[pl.BlockSpec((MOD_ROWS, d), lambda l, j: (0, 0)),
                  pl.BlockSpec((None, d, tn), lambda l, j: (l, 0, j)),
                  pl.BlockSpec((None, 1, tn), lambda l, j: (l, 0, j))],
        out_specs=pl.BlockSpec((None, MOD_ROWS, tn), lambda l, j: (l, 0, j)),
        out_shape=jax.ShapeDtypeStruct((depth, MOD_ROWS, n), F32),
        compiler_params=_cp(("arbitrary", "arbitrary"), 40),
        name="modulation",
    )(cond, w_mod, b_mod.reshape(depth, 1, n))


def _mod_row(i, tm, t_ctx, l_lat):
    start = i * tm
    return jnp.where(start < t_ctx, 0, 1 + (start - t_ctx) // l_lat)


def _ffn_body(x_ref, mod_ref, g_ref, wg_ref, wu_ref, wd_ref, o_ref, h_scr, *, sub, nj):
    j = pl.program_id(1)

    @pl.when(j == 0)
    def _():
        m = mod_ref[...]
        h = _norm_mod(x_ref[...], g_ref[...], m[3 * sub + 1:3 * sub + 2], m[3 * sub:3 * sub + 1])
        h_scr[...] = h.astype(BF16)
        o_ref[...] = jnp.zeros_like(o_ref)

    h = h_scr[...]
    a = _dot(h, wg_ref[...])
    b = _dot(h, wu_ref[...])
    act = (a * _sigmoid(a) * b).astype(BF16)
    o_ref[...] += _dot(act, wd_ref[...])

    @pl.when(j == nj - 1)
    def _():
        ga = mod_ref[...][3 * sub + 2:3 * sub + 3]
        o_ref[...] = x_ref[...] + 0.5 * ga * o_ref[...]


def ffn(x, mod_l, g_row, w_gu, w_dn, sub, t_ctx, l_lat, tm=512, tf=512):
    t, d = x.shape
    dff = w_dn.shape[0]
    nj = dff // tf
    row = functools.partial(_mod_row, tm=tm, t_ctx=t_ctx, l_lat=l_lat)
    return pl.pallas_call(
        functools.partial(_ffn_body, sub=sub, nj=nj),
        grid=(t // tm, nj),
        in_specs=[pl.BlockSpec((tm, d), lambda i, j: (i, 0)),
                  pl.BlockSpec((None, N_MOD, d), lambda i, j: (row(i), 0, 0)),
                  pl.BlockSpec((1, d), lambda i, j: (0, 0)),
                  pl.BlockSpec((d, tf), lambda i, j: (0, j)),
                  pl.BlockSpec((d, tf), lambda i, j: (0, j + nj)),
                  pl.BlockSpec((tf, d), lambda i, j: (j, 0))],
        out_specs=pl.BlockSpec((tm, d), lambda i, j: (i, 0)),
        out_shape=jax.ShapeDtypeStruct((t, d), F32),
        scratch_shapes=[pltpu.VMEM((tm, d), BF16)],
        compiler_params=_cp(("arbitrary", "arbitrary"), 56),
        name="ffn",
    )(x, mod_l, g_row, w_gu, w_gu, w_dn)


def _inproj_body(x_ref, mod_ref, g_ref, w_ref, oa_ref, ohy_ref, ohg_ref, ogt_ref, h_scr, *, bounds):
    j = pl.program_id(1)
    ja, jy, jg = bounds

    @pl.when(j == 0)
    def _():
        m = mod_ref[...]
        h_scr[...] = _norm_mod(x_ref[...], g_ref[...], m[4:5], m[3:4]).astype(BF16)

    r = _dot(h_scr[...], w_ref[...])

    @pl.when(j < ja)
    def _():
        oa_ref[...] = r

    @pl.when((j >= ja) & (j < jy))
    def _():
        ohy_ref[...] = r

    @pl.when((j >= jy) & (j < jg))
    def _():
        ohg_ref[...] = r

    @pl.when(j >= jg)
    def _():
        ogt_ref[...] = _sigmoid(r).astype(BF16)


def inproj(x, mod_l, g_row, w_in, sizes, t_ctx, l_lat, tm=1024, tn=512):
    t, d = x.shape
    n_attn, n_hy, n_hg, n_gt = sizes
    ja = n_attn // tn
    jy = ja + n_hy // tn
    jg = jy + n_hg // tn
    nj = jg + n_gt // tn
    row = functools.partial(_mod_row, tm=tm, t_ctx=t_ctx, l_lat=l_lat)
    return pl.pallas_call(
        functools.partial(_inproj_body, bounds=(ja, jy, jg)),
        grid=(t // tm, nj),
        in_specs=[pl.BlockSpec((tm, d), lambda i, j: (i, 0), pipeline_mode=pl.Buffered(1)),
                  pl.BlockSpec((None, N_MOD, d), lambda i, j: (row(i), 0, 0)),
                  pl.BlockSpec((1, d), lambda i, j: (0, 0)),
                  pl.BlockSpec((d, tn), lambda i, j: (0, j))],
        out_specs=[pl.BlockSpec((tm, tn), lambda i, j: (i, jnp.minimum(j, ja - 1))),
                   pl.BlockSpec((tm, tn), lambda i, j: (i, jnp.clip(j - ja, 0, jy - ja - 1))),
                   pl.BlockSpec((tm, tn), lambda i, j: (i, jnp.clip(j - jy, 0, jg - jy - 1))),
                   pl.BlockSpec((tm, tn), lambda i, j: (i, jnp.maximum(j - jg, 0)))],
        out_shape=[jax.ShapeDtypeStruct((t, n_attn), F32),
                   jax.ShapeDtypeStruct((t, n_hy), F32),
                   jax.ShapeDtypeStruct((t, n_hg), F32),
                   jax.ShapeDtypeStruct((t, n_gt), BF16)],
        scratch_shapes=[pltpu.VMEM((tm, d), BF16)],
        compiler_params=_cp(("arbitrary", "arbitrary"), 58),
        name="inproj",
    )(x, mod_l, g_row, w_in)


def rope_tables(l, rot_dim, reps):
    n_rows = l // GRID_W
    rows = jnp.broadcast_to(jnp.arange(n_rows, dtype=F32)[:, None], (n_rows, GRID_W)).reshape(-1)
    cols = jnp.broadcast_to(jnp.arange(GRID_W, dtype=F32)[None, :], (n_rows, GRID_W)).reshape(-1)
    axis_dim = rot_dim // 2
    inv = ROPE_BASE ** (-jnp.arange(0, axis_dim, 2, dtype=F32) / axis_dim)
    ang = jnp.concatenate([rows[:, None] * inv, cols[:, None] * inv], axis=-1)
    cos = jnp.repeat(jnp.cos(ang), 2, axis=-1)
    sin = jnp.repeat(jnp.sin(ang), 2, axis=-1)
    sign = jnp.tile(jnp.array([-1.0, 1.0], F32), rot_dim // 2)
    return jnp.tile(cos, (1, reps)), jnp.tile(sin * sign, (1, reps))


def _rope(x, c, s):
    lane = lax.broadcasted_iota(jnp.int32, x.shape, 1)
    nxt = pltpu.roll(x, x.shape[1] - 1, 1)
    prv = pltpu.roll(x, 1, 1)
    return x * c + jnp.where((lane & 1) == 0, nxt, prv) * s


def _attn_a_ctx_body(sink_ref, q_ref, k_ref, v_ref, o_ref, *, group):
    kh = pl.program_id(1)
    scale = HD ** -0.5
    k = k_ref[...].astype(BF16)
    v = v_ref[...].astype(BF16)
    for g in range(group):
        q = q_ref[:, g * HD:(g + 1) * HD].astype(BF16)
        s = _dot_nt(q, k) * scale
        sink = sink_ref[kh * group + g]
        m = jnp.maximum(jnp.max(s, axis=-1, keepdims=True), sink)
        p = jnp.exp(s - m)
        den = jnp.sum(p, axis=-1, keepdims=True) + jnp.exp(sink - m)
        o_ref[:, g * HD:(g + 1) * HD] = (_dot(p.astype(BF16), v) / den).astype(BF16)


def attn_a_ctx(u_attn, sink, batch, seq, kvh, group):
    qw = group * HD
    kcol = kvh * group
    vcol = kcol + kvh
    return pl.pallas_call(
        functools.partial(_attn_a_ctx_body, group=group),
        grid=(batch, kvh),
        in_specs=[pl.BlockSpec(memory_space=pltpu.SMEM),
                  pl.BlockSpec((seq, qw), lambda b, h: (b, h)),
                  pl.BlockSpec((seq, HD), lambda b, h: (b, kcol + h)),
                  pl.BlockSpec((seq, HD), lambda b, h: (b, vcol + h))],
        out_specs=pl.BlockSpec((seq, qw), lambda b, h: (b, h)),
        out_shape=jax.ShapeDtypeStruct((batch * seq, kvh * qw), BF16),
        compiler_params=_cp(("arbitrary", "arbitrary"), 32),
        name="attn_a_ctx",
    )(sink, u_attn, u_attn, u_attn)


def _attn_a_lat_body(sink_ref, q_ref, k0_ref, k1_ref, k2_ref, v0_ref, v1_ref, v2_ref, kc_ref, vc_ref,
                     cq_ref, sq_ref, c0_ref, s0_ref, c2_ref, s2_ref, o_ref, *, group, seq):
    kh = pl.program_id(1)
    n = pl.program_id(2)
    blk = q_ref.shape[0]
    scale = HD ** -0.5
    cq, sq = cq_ref[...], sq_ref[...]
    kw = jnp.concatenate([_rope(k0_ref[...], c0_ref[...], s0_ref[...]),
                          _rope(k1_ref[...], cq, sq),
                          _rope(k2_ref[...], c2_ref[...], s2_ref[...])], axis=0).astype(BF16)
    vw = jnp.concatenate([v0_ref[...], v1_ref[...], v2_ref[...]], axis=0).astype(BF16)
    kc = kc_ref[...].astype(BF16)
    vc = vc_ref[...].astype(BF16)
    qi = lax.broadcasted_iota(jnp.int32, (blk, 3 * blk), 0)
    kj = lax.broadcasted_iota(jnp.int32, (blk, 3 * blk), 1)
    kpos = (n - 1) * blk + kj
    qpos = n * blk + qi
    valid = (kpos >= 0) & (kpos < seq) & (jnp.abs(qpos - kpos) <= A_WINDOW)
    for g in range(group):
        q = _rope(q_ref[:, g * HD:(g + 1) * HD], cq, sq).astype(BF16)
        s_loc = jnp.where(valid, _dot_nt(q, kw) * scale, NEG)
        s_ctx = _dot_nt(q, kc) * scale
        sink = sink_ref[kh * group + g]
        m = jnp.maximum(jnp.maximum(jnp.max(s_loc, axis=-1, keepdims=True),
                                    jnp.max(s_ctx, axis=-1, keepdims=True)), sink)
        p_loc = jnp.exp(s_loc - m)
        p_ctx = jnp.exp(s_ctx - m)
        den = (jnp.sum(p_loc, axis=-1, keepdims=True) + jnp.sum(p_ctx, axis=-1, keepdims=True)
               + jnp.exp(sink - m))
        o = _dot(p_ctx.astype(BF16), vc) + _dot(p_loc.astype(BF16), vw)
        o_ref[:, g * HD:(g + 1) * HD] = (o / den).astype(BF16)


def attn_a_lat(u_attn, cache_k, cache_v, sink, rope_c, rope_s, layer, batch, seq, row0, kvh, group):
    blk = HD
    nb = seq // blk
    rb0 = row0 // blk
    qw = group * HD
    kcol = kvh * group
    vcol = kcol + kvh
    past = cache_k.shape[2]

    def rows(b, n):
        return rb0 + b * nb + n

    prev = lambda n: jnp.maximum(n - 1, 0)
    nxt = lambda n: jnp.minimum(n + 1, nb - 1)
    tab = lambda f: pl.BlockSpec((blk, HD), lambda b, h, n: (f(n), 0))
    return pl.pallas_call(
        functools.partial(_attn_a_lat_body, group=group, seq=seq),
        grid=(batch, kvh, nb),
        in_specs=[pl.BlockSpec(memory_space=pltpu.SMEM),
                  pl.BlockSpec((blk, qw), lambda b, h, n: (rows(b, n), h)),
                  pl.BlockSpec((blk, HD), lambda b, h, n: (rows(b, prev(n)), kcol + h)),
                  pl.BlockSpec((blk, HD), lambda b, h, n: (rows(b, n), kcol + h)),
                  pl.BlockSpec((blk, HD), lambda b, h, n: (rows(b, nxt(n)), kcol + h)),
                  pl.BlockSpec((blk, HD), lambda b, h, n: (rows(b, prev(n)), vcol + h)),
                  pl.BlockSpec((blk, HD), lambda b, h, n: (rows(b, n), vcol + h)),
                  pl.BlockSpec((blk, HD), lambda b, h, n: (rows(b, nxt(n)), vcol + h)),
                  pl.BlockSpec((None, None, past, HD), lambda b, h, n: (b, layer, 0, h)),
                  pl.BlockSpec((None, None, past, HD), lambda b, h, n: (b, layer, 0, h)),
                  tab(lambda n: n), tab(lambda n: n), tab(prev), tab(prev), tab(nxt), tab(nxt)],
        out_specs=pl.BlockSpec((blk, qw), lambda b, h, n: (b * nb + n, h)),
        out_shape=jax.ShapeDtypeStruct((batch * seq, kvh * qw), BF16),
        compiler_params=_cp(("arbitrary", "arbitrary", "arbitrary"), 32),
        name="attn_a_lat",
    )(sink, u_attn, u_attn, u_attn, u_attn, u_attn, u_attn, u_attn, cache_k, cache_v,
      rope_c, rope_s, rope_c, rope_s, rope_c, rope_s)


def _lambda(lam_ref, lam_init):
    lw = lam_ref[...]
    return (jnp.exp(jnp.sum(lw[0:1] * lw[1:2], axis=-1, keepdims=True))
            - jnp.exp(jnp.sum(lw[2:3] * lw[3:4], axis=-1, keepdims=True)) + lam_init)


def _softmax_parts(parts):
    m = functools.reduce(jnp.maximum, [jnp.max(s, axis=-1, keepdims=True) for s in parts])
    ps = [jnp.exp(s - m) for s in parts]
    inv = 1.0 / functools.reduce(lambda a, b: a + b, [jnp.sum(p, axis=-1, keepdims=True) for p in ps])
    return [p * inv for p in ps]


def _attn_b_ctx_body(lam_ref, g_ref, q_ref, k_ref, v_ref, o_ref, *, lam_init):
    hd = HD // 2
    scale = hd ** -0.5
    lam = _lambda(lam_ref, lam_init)
    q = q_ref[...].astype(BF16)
    k = k_ref[...].astype(BF16)
    p0, = _softmax_parts([_dot_nt(q[:, :hd], k[:, :hd]) * scale])
    p1, = _softmax_parts([_dot_nt(q[:, hd:], k[:, hd:]) * scale])
    o = _dot((p0 - lam * p1).astype(BF16), v_ref[...].astype(BF16))
    o_ref[...] = (_rms_lanes(o, g_ref[...]) * (1.0 - lam_init)).astype(BF16)


def attn_b_ctx(u_attn, b_lambda, b_subln, lam_init, batch, seq, heads, col0):
    qcol = col0 // HD
    kcol = qcol + heads
    vcol = kcol + heads
    return pl.pallas_call(
        functools.partial(_attn_b_ctx_body, lam_init=lam_init),
        grid=(batch, heads),
        in_specs=[pl.BlockSpec(b_lambda.shape, lambda b, h: (0, 0)),
                  pl.BlockSpec((1, HD), lambda b, h: (0, 0)),
                  pl.BlockSpec((seq, HD), lambda b, h: (b, qcol + h)),
                  pl.BlockSpec((seq, HD), lambda b, h: (b, kcol + h)),
                  pl.BlockSpec((seq, HD), lambda b, h: (b, vcol + h))],
        out_specs=pl.BlockSpec((seq, HD), lambda b, h: (b, h)),
        out_shape=jax.ShapeDtypeStruct((batch * seq, heads * HD), BF16),
        compiler_params=_cp(("arbitrary", "arbitrary"), 32),
        name="attn_b_ctx",
    )(b_lambda, b_subln.reshape(1, HD), u_attn, u_attn, u_attn)


def _attn_b_lat_body(lam_ref, g_ref, q_ref, k_ref, v_ref, kc_ref, vc_ref, cq_ref, sq_ref, ck_ref, sk_ref,
                     o_ref, k_scr, v_scr, *, lam_init):
    hd = HD // 2
    scale = hd ** -0.5
    nq = pl.program_id(2)

    @pl.when(nq == 0)
    def _():
        k_scr[...] = _rope(k_ref[...], ck_ref[...], sk_ref[...]).astype(BF16)
        v_scr[...] = v_ref[...].astype(BF16)

    lam = _lambda(lam_ref, lam_init)
    q = _rope(q_ref[...], cq_ref[...], sq_ref[...]).astype(BF16)
    k = k_scr[...]
    kc = kc_ref[...].astype(BF16)
    p0c, p0l = _softmax_parts([_dot_nt(q[:, :hd], kc[:, :hd]) * scale, _dot_nt(q[:, :hd], k[:, :hd]) * scale])
    p1c, p1l = _softmax_parts([_dot_nt(q[:, hd:], kc[:, hd:]) * scale, _dot_nt(q[:, hd:], k[:, hd:]) * scale])
    o = (_dot((p0c - lam * p1c).astype(BF16), vc_ref[...].astype(BF16))
         + _dot((p0l - lam * p1l).astype(BF16), v_scr[...]))
    o_ref[...] = (_rms_lanes(o, g_ref[...]) * (1.0 - lam_init)).astype(BF16)


def attn_b_lat(u_attn, cache_k, cache_v, b_lambda, b_subln, rope_c, rope_s, lam_init, layer,
               batch, seq, row0, heads, col0, tq=256):
    qcol = col0 // HD
    kcol = qcol + heads
    vcol = kcol + heads
    nq = seq // tq
    past = cache_k.shape[2]
    rq0 = row0 // tq
    rs0 = row0 // seq
    return pl.pallas_call(
        functools.partial(_attn_b_lat_body, lam_init=lam_init),
        grid=(batch, heads, nq),
        in_specs=[pl.BlockSpec(b_lambda.shape, lambda b, h, n: (0, 0)),
                  pl.BlockSpec((1, HD), lambda b, h, n: (0, 0)),
                  pl.BlockSpec((tq, HD), lambda b, h, n: (rq0 + b * nq + n, qcol + h)),
                  pl.BlockSpec((seq, HD), lambda b, h, n: (rs0 + b, kcol + h)),
                  pl.BlockSpec((seq, HD), lambda b, h, n: (rs0 + b, vcol + h)),
                  pl.BlockSpec((None, None, past, HD), lambda b, h, n: (b, layer, 0, h)),
                  pl.BlockSpec((None, None, past, HD), lambda b, h, n: (b, layer, 0, h)),
                  pl.BlockSpec((tq, HD), lambda b, h, n: (n, 0)),
                  pl.BlockSpec((tq, HD), lambda b, h, n: (n, 0)),
                  pl.BlockSpec((seq, HD), lambda b, h, n: (0, 0)),
                  pl.BlockSpec((seq, HD), lambda b, h, n: (0, 0))],
        out_specs=pl.BlockSpec((tq, HD), lambda b, h, n: (b * nq + n, h)),
        out_shape=jax.ShapeDtypeStruct((batch * seq, heads * HD), BF16),
        scratch_shapes=[pltpu.VMEM((seq, HD), BF16), pltpu.VMEM((seq, HD), BF16)],
        compiler_params=_cp(("arbitrary", "arbitrary", "arbitrary"), 40),
        name="attn_b_lat",
    )(b_lambda, b_subln.reshape(1, HD), u_attn, u_attn, u_attn, cache_k, cache_v,
      rope_c, rope_s, rope_c, rope_s)


def dft_tables(l):
    f = jnp.arange(l, dtype=jnp.int32)[:, None]
    s = jnp.arange(l, dtype=jnp.int32)[None, :]
    m = ((2 * f + 1) * s) % (4 * l)
    ang = m.astype(F32) * (math.pi / (2 * l))
    fm = jnp.concatenate([jnp.cos(ang), -jnp.sin(ang)], axis=0)
    hi = fm.astype(BF16)
    lo = (fm - hi.astype(F32)).astype(BF16)
    return hi, lo, hi.T


def hyena_feats(l):
    t = jnp.linspace(0.0, 1.0, l, dtype=F32)[:, None]
    bands = (HY_EMB - 1) // 2
    w = 2.0 * math.pi * jnp.arange(l, dtype=F32)[:, None] / l
    fr = jnp.linspace(1e-4, bands - 1, bands, dtype=F32)[None, :]
    feats = jnp.concatenate([t, jnp.cos(fr * w), -jnp.sin(fr * w)], axis=-1)
    return jnp.pad(feats, ((0, 0), (0, HD - HY_EMB)))


def hyena_decay(l, width):
    t = jnp.linspace(0.0, 1.0, l, dtype=F32)[:, None]
    deltas = jnp.abs(jnp.linspace(math.log(HY_TARGET) / HY_SLOW, math.log(HY_TARGET) / HY_FAST, width, dtype=F32))
    return jnp.exp(-t * deltas)


def _hy_time_body(feat_ref, dec_ref, w1_ref, b1_ref, w2_ref, b2_ref, w3_ref, fr_ref, kh_ref, kl_ref, nrm_scr,
                  *, width, l):
    ps = pl.program_id(2)
    c = pl.program_id(3)
    fr = fr_ref[...]
    h = jnp.sin(fr[0:1] * (_dot3(feat_ref[...], w1_ref[...]) + b1_ref[...]))
    h = jnp.sin(fr[1:2] * (_dot3(h, w2_ref[...]) + b2_ref[...]))
    dec = dec_ref[...]
    row = lax.broadcasted_iota(jnp.int32, dec.shape, 0)
    fwd = _dot3(h, w3_ref[:, :width]) * dec
    bwd = jnp.where((row == 0) & (c == 0), 0.0, _dot3(h, w3_ref[:, width:]) * dec)

    @pl.when((ps == 0) & (c == 0))
    def _():
        nrm_scr[...] = jnp.zeros_like(nrm_scr)

    @pl.when(ps == 0)
    def _():
        nrm_scr[...] += (jnp.sum(jnp.abs(fwd), axis=0, keepdims=True)
                         + jnp.sum(jnp.abs(bwd), axis=0, keepdims=True))

    @pl.when(ps == 1)
    def _():
        inv = 1.0 / ((nrm_scr[...] + EPS) * l)
        for part, val in ((0, fwd * inv), (1, bwd * inv)):
            hi, lo = _split(val)
            kh_ref[:, part * width:(part + 1) * width] = hi
            kl_ref[:, part * width:(part + 1) * width] = lo


def _hy_spec_body(fh_ref, fl_ref, kh_ref, kl_ref, o_ref, *, width, n_re):
    r = pl.program_id(2)
    fh = fh_ref[...]
    kh = kh_ref[...]
    res = _dot(fh, kh) + _dot(fh, kl_ref[...]) + _dot(fl_ref[...], kh)
    sign = jnp.where(r < n_re, 1.0, -1.0)
    o_ref[...] = res[:, :width] + sign * res[:, width:]


def hyena_filters(l, hy_w1p, hy_b1, hy_w2, hy_b2, hy_w3, hy_freq, f_hi, f_lo, width, rc=512, tr=512):
    depth = hy_w3.shape[0]
    ffn_w = hy_w2.shape[1]
    rc = min(rc, l)
    tr = min(tr, l)
    n_re = l // tr
    feats = hyena_feats(l)
    dec = hyena_decay(l, width)
    wspec = lambda shape: pl.BlockSpec((None,) + shape, lambda d, o, p, c: (d, 0, 0))
    kspec = pl.BlockSpec((None, rc, 2 * width), lambda d, o, p, c: (d, c * p, o))
    k_hi, k_lo = pl.pallas_call(
        functools.partial(_hy_time_body, width=width, l=l),
        grid=(depth, 2, 2, l // rc),
        in_specs=[pl.BlockSpec((rc, HD), lambda d, o, p, c: (c, 0)),
                  pl.BlockSpec((rc, width), lambda d, o, p, c: (c, 0)),
                  wspec((HD, ffn_w)), wspec((1, ffn_w)), wspec((ffn_w, ffn_w)), wspec((1, ffn_w)),
                  pl.BlockSpec((None, ffn_w, 2 * width), lambda d, o, p, c: (d, 0, o)),
                  wspec((2, ffn_w))],
        out_specs=[kspec, kspec],
        out_shape=[jax.ShapeDtypeStruct((depth, l, 4 * width), BF16)] * 2,
        scratch_shapes=[pltpu.VMEM((1, width), F32)],
        compiler_params=_cp(("arbitrary",) * 4, 32),
        name="hyena_time_filters",
    )(feats, dec, hy_w1p, hy_b1.reshape(depth, 1, ffn_w), hy_w2, hy_b2.reshape(depth, 1, ffn_w), hy_w3, hy_freq)
    fspec = pl.BlockSpec((tr, l), lambda d, o, r: (r, 0))
    kfull = pl.BlockSpec((None, l, 2 * width), lambda d, o, r: (d, 0, o))
    return pl.pallas_call(
        functools.partial(_hy_spec_body, width=width, n_re=n_re),
        grid=(depth, 2, 2 * n_re),
        in_specs=[fspec, fspec, kfull, kfull],
        out_specs=pl.BlockSpec((None, None, tr, width), lambda d, o, r: (d, o, r, 0)),
        out_shape=jax.ShapeDtypeStruct((depth, 2, 2 * l, width), F32),
        compiler_params=_cp(("arbitrary",) * 3, 48),
        name="hyena_spectra",
    )(f_hi, f_lo, k_hi, k_lo)


def _short_conv(u, w, b):
    l = u.shape[0]
    row = lax.broadcasted_iota(jnp.int32, u.shape, 0)
    up = jnp.where(row == 0, 0.0, pltpu.roll(u, 1, 0))
    dn = jnp.where(row == l - 1, 0.0, pltpu.roll(u, l - 1, 0))
    return up * w[0:1] + u * w[1:2] + dn * w[2:3] + b


def _hyena_body(hv_ref, h1_ref, h2_ref, wv_ref, w1_ref, w2_ref, bv_ref, b1_ref, b2_ref, bias_ref,
                fre_ref, fim_ref, gre_ref, gim_ref, kre_ref, kim_ref, o_ref,
                z_scr, x1_scr, x2_scr, zb_scr, acc_scr, *, nf):
    ph = pl.program_id(2)
    j = pl.program_id(3)

    @pl.when((ph == 0) & (j == 0))
    def _():
        v = _short_conv(hv_ref[...], wv_ref[...], bv_ref[...])
        z_scr[...] = v
        zb_scr[...] = v.astype(BF16)
        x1_scr[...] = _short_conv(h1_ref[...], w1_ref[...], b1_ref[...])
        x2_scr[...] = _short_conv(h2_ref[...], w2_ref[...], b2_ref[...])

    @pl.when(j == 0)
    def _():
        acc_scr[...] = jnp.zeros_like(acc_scr)

    zb = zb_scr[...]
    zre = _dot(fre_ref[...], zb)
    zim = _dot(fim_ref[...], zb)
    kre, kim = kre_ref[...], kim_ref[...]
    yre = (zre * kre - zim * kim).astype(BF16)
    yim = (zre * kim + zim * kre).astype(BF16)
    acc_scr[...] += _dot(gre_ref[...], yre) + _dot(gim_ref[...], yim)

    @pl.when((j == nf - 1) & (ph == 0))
    def _():
        y = x1_scr[...] * (acc_scr[...] + z_scr[...] * bias_ref[0:1])
        z_scr[...] = y
        zb_scr[...] = y.astype(BF16)

    @pl.when((j == nf - 1) & (ph == 1))
    def _():
        o_ref[...] = (x2_scr[...] * (acc_scr[...] + z_scr[...] * bias_ref[1:2])).astype(BF16)


def hyena(u_hy, conv_w, conv_b, bias, kf, f_hi, g_hi, layer, batch, seq, row0, width, cw=256, tfq=256):
    nch = width // cw
    tfq = min(tfq, seq)
    nf = seq // tfq
    rb = row0 // seq
    hspec = lambda part: pl.BlockSpec((seq, cw), lambda b, c, p, j: (rb + b, part * nch + c))
    wspec = lambda part: pl.BlockSpec((3, cw), lambda b, c, p, j: (0, part * nch + c))
    bspec = lambda part: pl.BlockSpec((1, cw), lambda b, c, p, j: (0, part * nch + c))
    return pl.pallas_call(
        functools.partial(_hyena_body, nf=nf),
        grid=(batch, nch, 2, nf),
        in_specs=[hspec(0), hspec(1), hspec(2), wspec(0), wspec(1), wspec(2), bspec(0), bspec(1), bspec(2),
                  pl.BlockSpec((2, cw), lambda b, c, p, j: (0, c)),
                  pl.BlockSpec((tfq, seq), lambda b, c, p, j: (j, 0)),
                  pl.BlockSpec((tfq, seq), lambda b, c, p, j: (nf + j, 0)),
                  pl.BlockSpec((seq, tfq), lambda b, c, p, j: (0, j)),
                  pl.BlockSpec((seq, tfq), lambda b, c, p, j: (0, nf + j)),
                  pl.BlockSpec((None, None, tfq, cw), lambda b, c, p, j: (layer, p, j, c)),
                  pl.BlockSpec((None, None, tfq, cw), lambda b, c, p, j: (layer, p, nf + j, c))],
        out_specs=pl.BlockSpec((seq, cw), lambda b, c, p, j: (b, c)),
        out_shape=jax.ShapeDtypeStruct((batch * seq, width), BF16),
        scratch_shapes=[pltpu.VMEM((seq, cw), F32), pltpu.VMEM((seq, cw), F32), pltpu.VMEM((seq, cw), F32),
                        pltpu.VMEM((seq, cw), BF16), pltpu.VMEM((seq, cw), F32)],
        compiler_params=_cp(("arbitrary",) * 4, 48),
        name="hyena",
    )(u_hy, u_hy, u_hy, conv_w, conv_w, conv_w, conv_b, conv_b, conv_b, bias,
      f_hi, f_hi, g_hi, g_hi, kf, kf)


def _cumsum_rows(x, reverse):
    n = x.shape[0]
    row = lax.broadcasted_iota(jnp.int32, x.shape, 0)
    sh = 1
    while sh < n:
        if reverse:
            x = x + jnp.where(row < n - sh, pltpu.roll(x, n - sh, 0), 0.0)
        else:
            x = x + jnp.where(row >= sh, pltpu.roll(x, sh, 0), 0.0)
        sh *= 2
    return x


def _hgrn_chunk(q, k, g, v, st, fwd):
    c = CHUNK
    b = _cumsum_rows(g, reverse=not fwd)
    tot = b[c - 1:c] if fwd else b[0:1]
    vb = v.astype(BF16)
    o = _dot_nt((q * jnp.exp(b)).astype(BF16), st.astype(BF16))
    kd = (k * jnp.exp(tot - b)).astype(BF16)
    st_new = st * jnp.exp(tot) + _dot_tn(vb, kd)

    ti = lax.broadcasted_iota(jnp.int32, (c, c), 0)
    si = lax.broadcasted_iota(jnp.int32, (c, c), 1)
    a = jnp.zeros((c, c), F32)
    blk = 2 * SUB
    while blk <= c:
        nb, half = c // blk, blk // 2
        b3, q3, k3 = (t.reshape(nb, blk, HD) for t in (b, q, k))
        pos = lax.broadcasted_iota(jnp.int32, (nb, blk, HD), 1)
        if fwd:
            ref = b3[:, half - 1:half, :]
            qm = pos >= half
        else:
            ref = b3[:, half:half + 1, :]
            qm = pos < half
        qt = jnp.where(qm, q3 * jnp.exp(jnp.minimum(b3 - ref, 0.0)), 0.0).reshape(c, HD).astype(BF16)
        kt = jnp.where(qm, 0.0, k3 * jnp.exp(jnp.minimum(ref - b3, 0.0))).reshape(c, HD).astype(BF16)
        shift = int(math.log2(blk))
        a = a + jnp.where((ti >> shift) == (si >> shift), _dot_nt(qt, kt), 0.0)
        blk *= 2
    o = o + _dot(a.astype(BF16), vb)

    nd = c // SUB
    b3, q3, k3, v3 = (t.reshape(nd, SUB, HD) for t in (b, q, k, v))
    pos = lax.broadcasted_iota(jnp.int32, (nd, SUB, HD), 1)
    rows = []
    for t in range(SUB):
        qt = q3[:, t:t + 1, :]
        bt = b3[:, t:t + 1, :]
        msk = (pos <= t) if fwd else (pos >= t)
        p = jnp.where(msk, qt * k3 * jnp.exp(jnp.minimum(bt - b3, 0.0)), 0.0)
        w = jnp.sum(p, axis=-1, keepdims=True)
        rows.append(jnp.sum(w * v3, axis=1, keepdims=True))
    o = o + jnp.concatenate(rows, axis=1).reshape(c, HD)
    return o, st_new


def _log1p(x):
    return jnp.log(1.0 + x)


def _hgrn_gate(z, lb):
    e = jnp.exp(-jnp.abs(z))
    l1p = _log1p(e)
    lsp = jnp.minimum(z, 0.0) - l1p
    lsn = jnp.minimum(-z, 0.0) - l1p
    bq = jnp.log(jnp.maximum(lb, LB_FLOOR)) + lsn
    log_f = jnp.maximum(lsp, bq) + _log1p(jnp.exp(-jnp.abs(lsp - bq)))
    sig_neg = jnp.where(z >= 0, e, 1.0) / (1.0 + e)
    return log_f, (1.0 - lb) * sig_neg


def _hgrn_lower_bound(lb_ref, layer, depth, direction):
    x = lb_ref[...]
    rows = [x[2 * i + direction:2 * i + direction + 1] for i in range(depth)]
    m = functools.reduce(jnp.maximum, rows)
    es = [jnp.exp(r - m) for r in rows]
    tot = functools.reduce(lambda a, b: a + b, es)
    cum = jnp.zeros_like(tot)
    for i in range(1, layer + 1):
        cum = cum + es[i]
    return jnp.maximum(cum / tot, 0.0)


def _hgrn_body(*refs, layer, depth, has_state, emit_state):
    it = iter(refs)
    lb_ref, ng_ref, q_ref, ff_ref, fb_ref, i_ref, g_ref = (next(it) for _ in range(7))
    s0_ref = next(it) if has_state else None
    o_ref = next(it)
    so_ref = next(it) if emit_state else None
    o_scr = next(it)
    nc = q_ref.shape[0] // CHUNK

    for direction, f_ref in ((0, ff_ref), (1, fb_ref)):
        fwd = direction == 0
        lb = _hgrn_lower_bound(lb_ref, layer, depth, direction)
        st0 = s0_ref[direction].T if has_state else jnp.zeros((HD, HD), F32)

        def step(i, st, fwd=fwd, lb=lb, f_ref=f_ref):
            ci = i if fwd else nc - 1 - i
            sl = pl.ds(pl.multiple_of(ci * CHUNK, CHUNK), CHUNK)
            xq = q_ref[sl, :]
            log_f, kk = _hgrn_gate(f_ref[sl, :], lb)
            o, st = _hgrn_chunk(xq * _sigmoid(xq), kk, log_f, i_ref[sl, :], st, fwd)
            if fwd:
                o_scr[sl, :] = o
            else:
                o_scr[sl, :] += o
            return st

        st = lax.fori_loop(0, nc, step, st0)
        if emit_state:
            so_ref[direction] = st.T

    xg = g_ref[...]
    o_ref[...] = (_rms_lanes(o_scr[...], ng_ref[...]) * (xg * _sigmoid(xg))).astype(BF16)


def hgrn(u_hg, hg_lb2, hg_norm, state, layer, depth, batch, seq, row0, heads, emit_state):
    rb = row0 // seq
    part = lambda p: pl.BlockSpec((seq, HD), lambda b, h: (rb + b, p * heads + h))
    in_specs = [pl.BlockSpec((2 * depth, HD), lambda b, h: (0, h)),
                pl.BlockSpec((1, HD), lambda b, h: (0, 0)),
                part(0), part(1), part(2), part(3), part(4)]
    args = [hg_lb2, hg_norm.reshape(1, HD), u_hg, u_hg, u_hg, u_hg, u_hg]
    if state is not None:
        in_specs.append(pl.BlockSpec((None, None, 2, None, HD, HD), lambda b, h: (b, layer, 0, h, 0, 0)))
        args.append(state)
    out_specs = [pl.BlockSpec((seq, HD), lambda b, h: (b, h))]
    out_shape = [jax.ShapeDtypeStruct((batch * seq, heads * HD), BF16)]
    if emit_state:
        out_specs.append(pl.BlockSpec((None, 2, None, HD, HD), lambda b, h: (b, 0, h, 0, 0)))
        out_shape.append(jax.ShapeDtypeStruct((batch, 2, heads, HD, HD), F32))
    return pl.pallas_call(
        functools.partial(_hgrn_body, layer=layer, depth=depth, has_state=state is not None,
                          emit_state=emit_state),
        grid=(batch, heads),
        in_specs=in_specs,
        out_specs=out_specs,
        out_shape=out_shape,
        scratch_shapes=[pltpu.VMEM((seq, HD), F32)],
        compiler_params=_cp(("arbitrary", "arbitrary"), 40),
        name="hgrn",
    )(*args)


def _merge_body(oa_ref, ob_ref, oc_ref, od_ref, gt_ref, x_ref, mod_ref, wb_ref, wo_ref, out_ref):
    d = x_ref.shape[1]
    acc = None
    for n, o_ref in enumerate((oa_ref, ob_ref, oc_ref, od_ref)):
        y = gt_ref[:, n * d:(n + 1) * d].astype(F32) * _dot(o_ref[...], wb_ref[n])
        acc = y if acc is None else acc + y
    out = _dot(acc.astype(BF16), wo_ref[...])
    out_ref[...] = x_ref[...] + mod_ref[...][5:6] * out


def merge(o_a, o_b, o_c, o_d, gates, x, mod_l, w_branch, w_out, t_ctx, l_lat, tm=256):
    t, d = x.shape
    mw = o_a.shape[1]
    row = functools.partial(_mod_row, tm=tm, t_ctx=t_ctx, l_lat=l_lat)
    ospec = pl.BlockSpec((tm, mw), lambda i: (i, 0))
    return pl.pallas_call(
        _merge_body,
        grid=(t // tm,),
        in_specs=[ospec, ospec, ospec, ospec,
                  pl.BlockSpec((tm, N_BRANCH * d), lambda i: (i, 0)),
                  pl.BlockSpec((tm, d), lambda i: (i, 0)),
                  pl.BlockSpec((None, N_MOD, d), lambda i: (row(i), 0, 0)),
                  pl.BlockSpec((N_BRANCH, mw, d), lambda i: (0, 0, 0), pipeline_mode=pl.Buffered(1)),
                  pl.BlockSpec((d, d), lambda i: (0, 0), pipeline_mode=pl.Buffered(1))],
        out_specs=pl.BlockSpec((tm, d), lambda i: (i, 0)),
        out_shape=jax.ShapeDtypeStruct((t, d), F32),
        compiler_params=_cp(("arbitrary",), 56),
        name="merge",
    )(o_a, o_b, o_c, o_d, gates, x, mod_l, w_branch, w_out)


def _final_norm_body(x_ref, g_ref, o_ref):
    o_ref[...] = _rms_lanes(x_ref[...], g_ref[...])


def final_norm(x, g, tm=512):
    t, d = x.shape
    return pl.pallas_call(
        _final_norm_body,
        grid=(t // tm,),
        in_specs=[pl.BlockSpec((tm, d), lambda i: (i, 0)), pl.BlockSpec((1, d), lambda i: (0, 0))],
        out_specs=pl.BlockSpec((tm, d), lambda i: (i, 0)),
        out_shape=jax.ShapeDtypeStruct((t, d), F32),
        compiler_params=_cp(("arbitrary",), 32),
        name="final_norm",
    )(x, g.reshape(1, d))


def kernel(x_prompt, x_sample, c, cache_a_k, cache_a_v, cache_b_k, cache_b_v, state_hgrn, c_ctx, w_mod, b_mod, norm_g, w_ffn1_gu, w_ffn1_down, w_ffn2_gu, w_ffn2_down, w_in, w_branch, w_out, a_sink, b_lambda, b_subln, hy_conv_w, hy_conv_b, hy_w1, hy_b1, hy_w2, hy_b2, hy_w3, hy_freq, hy_bias, hg_lb, hg_norm, final_g):
    batch, seq, d = x_prompt.shape
    dec_batch, dec_seq, _ = x_sample.shape
    depth = w_mod.shape[0]
    mix_w = w_branch.shape[2]
    a_heads = a_sink.shape[1]
    a_kvh = cache_a_k.shape[3]
    a_group = a_heads // a_kvh
    b_heads = cache_b_k.shape[3]
    hg_heads = state_hgrn.shape[3]
    t_ctx, t_lat = batch * seq, dec_batch * dec_seq

    n_attn = (a_heads + 2 * a_kvh + 3 * b_heads) * HD
    sizes = (n_attn, 3 * mix_w, 5 * mix_w, N_BRANCH * d)
    b_col0 = (a_heads + 2 * a_kvh) * HD

    x = jnp.concatenate([x_prompt.reshape(t_ctx, d), x_sample.reshape(t_lat, d)], axis=0)
    cond = jnp.concatenate([c_ctx[None, :], c], axis=0)
    cond = jnp.pad(cond, ((0, MOD_ROWS - cond.shape[0]), (0, 0)))
    mod = modulation(cond, w_mod, b_mod).reshape(depth, MOD_ROWS, N_MOD, d)

    rope_a = rope_tables(dec_seq, HD, 1)
    rope_b = rope_tables(dec_seq, HD // 2, 2)
    fc_hi, fc_lo, gc_hi = dft_tables(seq)
    fl_hi, fl_lo, gl_hi = dft_tables(dec_seq)
    w1p = jnp.pad(hy_w1, ((0, 0), (0, HD - hy_w1.shape[1]), (0, 0)))
    kf_ctx = hyena_filters(seq, w1p, hy_b1, hy_w2, hy_b2, hy_w3, hy_freq, fc_hi, fc_lo, mix_w)
    kf_lat = hyena_filters(dec_seq, w1p, hy_b1, hy_w2, hy_b2, hy_w3, hy_freq, fl_hi, fl_lo, mix_w)

    cak = cache_a_k.reshape(dec_batch, depth, -1, a_kvh * HD)
    cav = cache_a_v.reshape(dec_batch, depth, -1, a_kvh * HD)
    cbk = cache_b_k.reshape(dec_batch, depth, -1, b_heads * HD)
    cbv = cache_b_v.reshape(dec_batch, depth, -1, b_heads * HD)
    hg_lb2 = hg_lb.reshape(depth * 2, mix_w)

    ak_l, av_l, bk_l, bv_l, st_l = [], [], [], [], []
    for l in range(depth):
        lam_init = 0.8 - 0.6 * math.exp(-0.3 * l)
        mod_l = mod[l]
        x = ffn(x, mod_l, norm_g[l, 0:1], w_ffn1_gu[l].astype(BF16), w_ffn1_down[l].astype(BF16), 0,
                t_ctx, dec_seq)
        u_attn, u_hy, u_hg, gates = inproj(x, mod_l, norm_g[l, 1:2], w_in[l].astype(BF16), sizes, t_ctx, dec_seq)

        uc = u_attn[:t_ctx]
        k0 = a_heads * HD
        ak_l.append(uc[:, k0:k0 + a_kvh * HD].reshape(batch, seq, a_kvh, HD))
        av_l.append(uc[:, k0 + a_kvh * HD:b_col0].reshape(batch, seq, a_kvh, HD))
        k1 = b_col0 + b_heads * HD
        bk_l.append(uc[:, k1:k1 + b_heads * HD].reshape(batch, seq, b_heads, 2, HD // 2))
        bv_l.append(uc[:, k1 + b_heads * HD:].reshape(batch, seq, b_heads, HD))

        oa_c = attn_a_ctx(u_attn, a_sink[l], batch, seq, a_kvh, a_group)
        oa_s = attn_a_lat(u_attn, cak, cav, a_sink[l], rope_a[0], rope_a[1], l, dec_batch, dec_seq, t_ctx,
                          a_kvh, a_group)
        ob_c = attn_b_ctx(u_attn, b_lambda[l], b_subln[l], lam_init, batch, seq, b_heads, b_col0)
        ob_s = attn_b_lat(u_attn, cbk, cbv, b_lambda[l], b_subln[l], rope_b[0], rope_b[1], lam_init, l,
                          dec_batch, dec_seq, t_ctx, b_heads, b_col0)
        cb = hy_conv_b[l].reshape(1, -1)
        oc_c = hyena(u_hy, hy_conv_w[l], cb, hy_bias[l], kf_ctx, fc_hi, gc_hi, l, batch, seq, 0, mix_w)
        oc_s = hyena(u_hy, hy_conv_w[l], cb, hy_bias[l], kf_lat, fl_hi, gl_hi, l, dec_batch, dec_seq, t_ctx, mix_w)
        od_c, st = hgrn(u_hg, hg_lb2, hg_norm[l], None, l, depth, batch, seq, 0, hg_heads, True)
        od_s, = hgrn(u_hg, hg_lb2, hg_norm[l], state_hgrn, l, depth, dec_batch, dec_seq, t_ctx, hg_heads, False)
        st_l.append(st)

        cat = lambda a, b: jnp.concatenate([a, b], axis=0)
        x = merge(cat(oa_c, oa_s), cat(ob_c, ob_s), cat(oc_c, oc_s), cat(od_c, od_s), gates, x, mod_l,
                  w_branch[l].astype(BF16), w_out[l].astype(BF16), t_ctx, dec_seq)
        x = ffn(x, mod_l, norm_g[l, 2:3], w_ffn2_gu[l].astype(BF16), w_ffn2_down[l].astype(BF16), 2,
                t_ctx, dec_seq)

    y = final_norm(x, final_g)
    return (y[:t_ctx].reshape(batch, seq, d), y[t_ctx:].reshape(dec_batch, dec_seq, d),
            jnp.stack(ak_l, axis=1), jnp.stack(av_l, axis=1), jnp.stack(bk_l, axis=1), jnp.stack(bv_l, axis=1),
            jnp.stack(st_l, axis=1))
```

```python
import functools
import math

import jax
import jax.numpy as jnp
import numpy as np
from jax import lax
from jax.experimental import pallas as pl
from jax.experimental.pallas import tpu as pltpu

F32 = jnp.float32
BF16 = jnp.bfloat16

EPS = 1e-6
NEG = -1e30
LB_FLOOR = 1e-30
ROPE_BASE = 10000.0
GRID_W = 64
N_MOD = 9
N_BRANCH = 4
HD = 128
A_WINDOW = 128
HY_EMB = 33
HY_TARGET, HY_FAST, HY_SLOW = 1e-2, 0.3, 1.5
CHUNK = 128
SUB = 8
LOG2E = 1.4426950408889634
MOD_ROWS = 16
VMEM_MB = 2 ** 20


def _cp(sem, vmem_mb):
    return pltpu.CompilerParams(dimension_semantics=sem, vmem_limit_bytes=vmem_mb * VMEM_MB)


def _dot(a, b):
    return jnp.dot(a, b, preferred_element_type=F32)


def _dot_nt(a, b):
    return lax.dot_general(a, b, (((1,), (1,)), ((), ())), preferred_element_type=F32)


def _dot_tn(a, b):
    return lax.dot_general(a, b, (((0,), (0,)), ((), ())), preferred_element_type=F32)


def _split(a):
    hi = a.astype(BF16)
    lo = (a - hi.astype(F32)).astype(BF16)
    return hi, lo


def _dot3(a, b):
    ah, al = _split(a)
    bh, bl = _split(b)
    return _dot(ah, bh) + _dot(ah, bl) + _dot(al, bh)


def _sigmoid(x):
    return jax.nn.sigmoid(x)


def _norm_mod(x, g, sc, sh):
    y = x * lax.rsqrt(jnp.mean(x * x, axis=-1, keepdims=True) + EPS)
    return (y * g) * (1.0 + sc) + sh


def _rms_lanes(x, g):
    return x * lax.rsqrt(jnp.mean(x * x, axis=-1, keepdims=True) + EPS) * g


def _mod_body(c_ref, w_ref, b_ref, o_ref):
    c = c_ref[...]
    a = (c * _sigmoid(c)).astype(BF16)
    o_ref[...] = _dot(a, w_ref[...].astype(BF16)) + b_ref[...]


def modulation(cond, w_mod, b_mod, tn=1024):
    depth, d, n = w_mod.shape
    return pl.pallas_call(
        _mod_body,
        grid=(depth, n // tn),
        in_specs=[pl.BlockSpec((MOD_ROWS, d), lambda l, j: (0, 0)),
                  pl.BlockSpec((None, d, tn), lambda l, j: (l, 0, j)),
                  pl.BlockSpec((None, 1, tn), lambda l, j: (l, 0, j))],
        out_specs=pl.BlockSpec((None, MOD_ROWS, tn), lambda l, j: (l, 0, j)),
        out_shape=jax.ShapeDtypeStruct((depth, MOD_ROWS, n), F32),
        compiler_params=_cp(("arbitrary", "arbitrary"), 40),
        name="modulation",
    )(cond, w_mod, b_mod.reshape(depth, 1, n))


def _mod_row(i, tm, t_ctx, l_lat):
    start = i * tm
    return jnp.where(start < t_ctx, 0, 1 + (start - t_ctx) // l_lat)


def _ffn_body(x_ref, mod_ref, g_ref, wg_ref, wu_ref, wd_ref, o_ref, h_scr, *, sub, nj):
    j = pl.program_id(1)

    @pl.when(j == 0)
    def _():
        m = mod_ref[...]
        h = _norm_mod(x_ref[...], g_ref[...], m[3 * sub + 1:3 * sub + 2], m[3 * sub:3 * sub + 1])
        h_scr[...] = h.astype(BF16)
        o_ref[...] = jnp.zeros_like(o_ref)

    h = h_scr[...]
    a = _dot(h, wg_ref[...])
    b = _dot(h, wu_ref[...])
    act = (a * _sigmoid(a) * b).astype(BF16)
    o_ref[...] += _dot(act, wd_ref[...])

    @pl.when(j == nj - 1)
    def _():
        ga = mod_ref[...][3 * sub + 2:3 * sub + 3]
        o_ref[...] = x_ref[...] + 0.5 * ga * o_ref[...]


def ffn(x, mod_l, g_row, w_gu, w_dn, sub, t_ctx, l_lat, tm=512, tf=512):
    t, d = x.shape
    dff = w_dn.shape[0]
    nj = dff // tf
    row = functools.partial(_mod_row, tm=tm, t_ctx=t_ctx, l_lat=l_lat)
    return pl.pallas_call(
        functools.partial(_ffn_body, sub=sub, nj=nj),
        grid=(t // tm, nj),
        in_specs=[pl.BlockSpec((tm, d), lambda i, j: (i, 0)),
                  pl.BlockSpec((None, N_MOD, d), lambda i, j: (row(i), 0, 0)),
                  pl.BlockSpec((1, d), lambda i, j: (0, 0)),
                  pl.BlockSpec((d, tf), lambda i, j: (0, j)),
                  pl.BlockSpec((d, tf), lambda i, j: (0, j + nj)),
                  pl.BlockSpec((tf, d), lambda i, j: (j, 0))],
        out_specs=pl.BlockSpec((tm, d), lambda i, j: (i, 0)),
        out_shape=jax.ShapeDtypeStruct((t, d), F32),
        scratch_shapes=[pltpu.VMEM((tm, d), BF16)],
        compiler_params=_cp(("arbitrary", "arbitrary"), 56),
        name="ffn",
    )(x, mod_l, g_row, w_gu, w_gu, w_dn)


def _inproj_body(x_ref, mod_ref, g_ref, w_ref, oa_ref, ohy_ref, ohg_ref, ogt_ref, h_scr, *, bounds):
    j = pl.program_id(1)
    ja, jy, jg = bounds

    @pl.when(j == 0)
    def _():
        m = mod_ref[...]
        h_scr[...] = _norm_mod(x_ref[...], g_ref[...], m[4:5], m[3:4]).astype(BF16)

    @pl.when(j < ja)
    def _():
        oa_ref[...] = _dot(h_scr[...], w_ref[...])

    @pl.when((j >= ja) & (j < jy))
    def _():
        ohy_ref[...] = _dot(h_scr[...], w_ref[...])

    @pl.when((j >= jy) & (j < jg))
    def _():
        ohg_ref[...] = _dot(h_scr[...], w_ref[...])

    @pl.when(j >= jg)
    def _():
        ogt_ref[...] = _sigmoid(_dot(h_scr[...], w_ref[...])).astype(BF16)


def inproj(x, mod_l, g_row, w_in, sizes, t_ctx, l_lat, tm=1024, tn=512):
    t, d = x.shape
    n_attn, n_hy, n_hg, n_gt = sizes
    ja = n_attn // tn
    jy = ja + n_hy // tn
    jg = jy + n_hg // tn
    nj = jg + n_gt // tn
    row = functools.partial(_mod_row, tm=tm, t_ctx=t_ctx, l_lat=l_lat)
    return pl.pallas_call(
        functools.partial(_inproj_body, bounds=(ja, jy, jg)),
        grid=(t // tm, nj),
        in_specs=[pl.BlockSpec((tm, d), lambda i, j: (i, 0), pipeline_mode=pl.Buffered(1)),
                  pl.BlockSpec((None, N_MOD, d), lambda i, j: (row(i), 0, 0)),
                  pl.BlockSpec((1, d), lambda i, j: (0, 0)),
                  pl.BlockSpec((d, tn), lambda i, j: (0, j))],
        out_specs=[pl.BlockSpec((tm, tn), lambda i, j: (i, jnp.minimum(j, ja - 1))),
                   pl.BlockSpec((tm, tn), lambda i, j: (i, jnp.clip(j - ja, 0, jy - ja - 1))),
                   pl.BlockSpec((tm, tn), lambda i, j: (i, jnp.clip(j - jy, 0, jg - jy - 1))),
                   pl.BlockSpec((tm, tn), lambda i, j: (i, jnp.maximum(j - jg, 0)))],
        out_shape=[jax.ShapeDtypeStruct((t, n_attn), F32),
                   jax.ShapeDtypeStruct((t, n_hy), F32),
                   jax.ShapeDtypeStruct((t, n_hg), F32),
                   jax.ShapeDtypeStruct((t, n_gt), BF16)],
        scratch_shapes=[pltpu.VMEM((tm, d), BF16)],
        compiler_params=_cp(("arbitrary", "arbitrary"), 58),
        name="inproj",
    )(x, mod_l, g_row, w_in)


def rope_tables(l, rot_dim, reps):
    n_rows = l // GRID_W
    rows = jnp.broadcast_to(jnp.arange(n_rows, dtype=F32)[:, None], (n_rows, GRID_W)).reshape(-1)
    cols = jnp.broadcast_to(jnp.arange(GRID_W, dtype=F32)[None, :], (n_rows, GRID_W)).reshape(-1)
    axis_dim = rot_dim // 2
    inv = ROPE_BASE ** (-jnp.arange(0, axis_dim, 2, dtype=F32) / axis_dim)
    ang = jnp.concatenate([rows[:, None] * inv, cols[:, None] * inv], axis=-1)
    cos = jnp.repeat(jnp.cos(ang), 2, axis=-1)
    sin = jnp.repeat(jnp.sin(ang), 2, axis=-1)
    sign = jnp.tile(jnp.array([-1.0, 1.0], F32), rot_dim // 2)
    return jnp.tile(cos, (1, reps)), jnp.tile(sin * sign, (1, reps))


def _rope(x, c, s):
    lane = lax.broadcasted_iota(jnp.int32, x.shape, 1)
    nxt = pltpu.roll(x, x.shape[1] - 1, 1)
    prv = pltpu.roll(x, 1, 1)
    return x * c + jnp.where((lane & 1) == 0, nxt, prv) * s


def _attn_a_ctx_body(sink_ref, q_ref, k_ref, v_ref, o_ref, *, group):
    kh = pl.program_id(1)
    scale = HD ** -0.5
    k = k_ref[...].astype(BF16)
    v = v_ref[...].astype(BF16)
    for g in range(group):
        q = q_ref[:, g * HD:(g + 1) * HD].astype(BF16)
        s = _dot_nt(q, k) * scale
        sink = sink_ref[kh * group + g]
        m = jnp.maximum(jnp.max(s, axis=-1, keepdims=True), sink)
        p = jnp.exp(s - m)
        den = jnp.sum(p, axis=-1, keepdims=True) + jnp.exp(sink - m)
        o_ref[:, g * HD:(g + 1) * HD] = (_dot(p.astype(BF16), v) / den).astype(BF16)


def attn_a_ctx(u_attn, sink, batch, seq, kvh, group):
    qw = group * HD
    kcol = kvh * group
    vcol = kcol + kvh
    return pl.pallas_call(
        functools.partial(_attn_a_ctx_body, group=group),
        grid=(batch, kvh),
        in_specs=[pl.BlockSpec(memory_space=pltpu.SMEM),
                  pl.BlockSpec((seq, qw), lambda b, h: (b, h)),
                  pl.BlockSpec((seq, HD), lambda b, h: (b, kcol + h)),
                  pl.BlockSpec((seq, HD), lambda b, h: (b, vcol + h))],
        out_specs=pl.BlockSpec((seq, qw), lambda b, h: (b, h)),
        out_shape=jax.ShapeDtypeStruct((batch * seq, kvh * qw), BF16),
        compiler_params=_cp(("arbitrary", "arbitrary"), 32),
        name="attn_a_ctx",
    )(sink, u_attn, u_attn, u_attn)


def _attn_a_lat_body(sink_ref, q_ref, k0_ref, k1_ref, k2_ref, v0_ref, v1_ref, v2_ref, kc_ref, vc_ref,
                     cq_ref, sq_ref, c0_ref, s0_ref, c2_ref, s2_ref, o_ref, *, group, seq):
    kh = pl.program_id(1)
    n = pl.program_id(2)
    blk = q_ref.shape[0]
    scale = HD ** -0.5
    cq, sq = cq_ref[...], sq_ref[...]
    kw = jnp.concatenate([_rope(k0_ref[...], c0_ref[...], s0_ref[...]),
                          _rope(k1_ref[...], cq, sq),
                          _rope(k2_ref[...], c2_ref[...], s2_ref[...])], axis=0).astype(BF16)
    vw = jnp.concatenate([v0_ref[...], v1_ref[...], v2_ref[...]], axis=0).astype(BF16)
    kc = kc_ref[...].astype(BF16)
    vc = vc_ref[...].astype(BF16)
    qi = lax.broadcasted_iota(jnp.int32, (blk, 3 * blk), 0)
    kj = lax.broadcasted_iota(jnp.int32, (blk, 3 * blk), 1)
    kpos = (n - 1) * blk + kj
    qpos = n * blk + qi
    valid = (kpos >= 0) & (kpos < seq) & (jnp.abs(qpos - kpos) <= A_WINDOW)
    for g in range(group):
        q = _rope(q_ref[:, g * HD:(g + 1) * HD], cq, sq).astype(BF16)
        s_loc = jnp.where(valid, _dot_nt(q, kw) * scale, NEG)
        s_ctx = _dot_nt(q, kc) * scale
        sink = sink_ref[kh * group + g]
        m = jnp.maximum(jnp.maximum(jnp.max(s_loc, axis=-1, keepdims=True),
                                    jnp.max(s_ctx, axis=-1, keepdims=True)), sink)
        p_loc = jnp.exp(s_loc - m)
        p_ctx = jnp.exp(s_ctx - m)
        den = (jnp.sum(p_loc, axis=-1, keepdims=True) + jnp.sum(p_ctx, axis=-1, keepdims=True)
               + jnp.exp(sink - m))
        o = _dot(p_ctx.astype(BF16), vc) + _dot(p_loc.astype(BF16), vw)
        o_ref[:, g * HD:(g + 1) * HD] = (o / den).astype(BF16)


def attn_a_lat(u_attn, cache_k, cache_v, sink, rope_c, rope_s, layer, batch, seq, row0, kvh, group):
    blk = HD
    nb = seq // blk
    rb0 = row0 // blk
    qw = group * HD
    kcol = kvh * group
    vcol = kcol + kvh
    past = cache_k.shape[2]

    def rows(b, n):
        return rb0 + b * nb + n

    prev = lambda n: jnp.maximum(n - 1, 0)
    nxt = lambda n: jnp.minimum(n + 1, nb - 1)
    tab = lambda f: pl.BlockSpec((blk, HD), lambda b, h, n: (f(n), 0))
    return pl.pallas_call(
        functools.partial(_attn_a_lat_body, group=group, seq=seq),
        grid=(batch, kvh, nb),
        in_specs=[pl.BlockSpec(memory_space=pltpu.SMEM),
                  pl.BlockSpec((blk, qw), lambda b, h, n: (rows(b, n), h)),
                  pl.BlockSpec((blk, HD), lambda b, h, n: (rows(b, prev(n)), kcol + h)),
                  pl.BlockSpec((blk, HD), lambda b, h, n: (rows(b, n), kcol + h)),
                  pl.BlockSpec((blk, HD), lambda b, h, n: (rows(b, nxt(n)), kcol + h)),
                  pl.BlockSpec((blk, HD), lambda b, h, n: (rows(b, prev(n)), vcol + h)),
                  pl.BlockSpec((blk, HD), lambda b, h, n: (rows(b, n), vcol + h)),
                  pl.BlockSpec((blk, HD), lambda b, h, n: (rows(b, nxt(n)), vcol + h)),
                  pl.BlockSpec((None, None, past, HD), lambda b, h, n: (b, layer, 0, h)),
                  pl.BlockSpec((None, None, past, HD), lambda b, h, n: (b, layer, 0, h)),
                  tab(lambda n: n), tab(lambda n: n), tab(prev), tab(prev), tab(nxt), tab(nxt)],
        out_specs=pl.BlockSpec((blk, qw), lambda b, h, n: (b * nb + n, h)),
        out_shape=jax.ShapeDtypeStruct((batch * seq, kvh * qw), BF16),
        compiler_params=_cp(("arbitrary", "arbitrary", "arbitrary"), 32),
        name="attn_a_lat",
    )(sink, u_attn, u_attn, u_attn, u_attn, u_attn, u_attn, u_attn, cache_k, cache_v,
      rope_c, rope_s, rope_c, rope_s, rope_c, rope_s)


def _lambda(lam_ref, lam_init):
    lw = lam_ref[...]
    return (jnp.exp(jnp.sum(lw[0:1] * lw[1:2], axis=-1, keepdims=True))
            - jnp.exp(jnp.sum(lw[2:3] * lw[3:4], axis=-1, keepdims=True)) + lam_init)


def _softmax_parts(parts):
    m = functools.reduce(jnp.maximum, [jnp.max(s, axis=-1, keepdims=True) for s in parts])
    ps = [jnp.exp(s - m) for s in parts]
    inv = 1.0 / functools.reduce(lambda a, b: a + b, [jnp.sum(p, axis=-1, keepdims=True) for p in ps])
    return [p * inv for p in ps]


def _attn_b_ctx_body(lam_ref, g_ref, q_ref, k_ref, v_ref, o_ref, *, lam_init):
    hd = HD // 2
    scale = hd ** -0.5
    lam = _lambda(lam_ref, lam_init)
    q = q_ref[...].astype(BF16)
    k = k_ref[...].astype(BF16)
    p0, = _softmax_parts([_dot_nt(q[:, :hd], k[:, :hd]) * scale])
    p1, = _softmax_parts([_dot_nt(q[:, hd:], k[:, hd:]) * scale])
    o = _dot((p0 - lam * p1).astype(BF16), v_ref[...].astype(BF16))
    o_ref[...] = (_rms_lanes(o, g_ref[...]) * (1.0 - lam_init)).astype(BF16)


def attn_b_ctx(u_attn, b_lambda, b_subln, lam_init, batch, seq, heads, col0):
    qcol = col0 // HD
    kcol = qcol + heads
    vcol = kcol + heads
    return pl.pallas_call(
        functools.partial(_attn_b_ctx_body, lam_init=lam_init),
        grid=(batch, heads),
        in_specs=[pl.BlockSpec(b_lambda.shape, lambda b, h: (0, 0)),
                  pl.BlockSpec((1, HD), lambda b, h: (0, 0)),
                  pl.BlockSpec((seq, HD), lambda b, h: (b, qcol + h)),
                  pl.BlockSpec((seq, HD), lambda b, h: (b, kcol + h)),
                  pl.BlockSpec((seq, HD), lambda b, h: (b, vcol + h))],
        out_specs=pl.BlockSpec((seq, HD), lambda b, h: (b, h)),
        out_shape=jax.ShapeDtypeStruct((batch * seq, heads * HD), BF16),
        compiler_params=_cp(("arbitrary", "arbitrary"), 32),
        name="attn_b_ctx",
    )(b_lambda, b_subln.reshape(1, HD), u_attn, u_attn, u_attn)


def _attn_b_lat_body(lam_ref, g_ref, q_ref, k_ref, v_ref, kc_ref, vc_ref, cq_ref, sq_ref, ck_ref, sk_ref,
                     o_ref, k_scr, kc_scr, v_scr, *, lam_init):
    hd = HD // 2
    scale = hd ** -0.5
    nq = pl.program_id(2)

    @pl.when(nq == 0)
    def _():
        k_scr[...] = _rope(k_ref[...], ck_ref[...], sk_ref[...]).T.astype(BF16)
        kc_scr[...] = kc_ref[...].T.astype(BF16)
        v_scr[...] = v_ref[...].astype(BF16)

    lam = _lambda(lam_ref, lam_init)
    q = _rope(q_ref[...], cq_ref[...], sq_ref[...]) * (scale * LOG2E)
    lane = lax.broadcasted_iota(jnp.int32, q.shape, 1)
    k = k_scr[...]
    v = v_scr[...]
    kc = kc_scr[...]
    vc = vc_ref[...].astype(BF16)
    outs = []
    for c in range(2):
        qc = jnp.where((lane >= hd) == (c == 1), q, 0.0).astype(BF16)
        s_ctx = _dot(qc, kc)
        s_lat = _dot(qc, k)
        m = jnp.maximum(jnp.max(s_ctx, axis=-1, keepdims=True), jnp.max(s_lat, axis=-1, keepdims=True))
        p_ctx = jnp.exp2(s_ctx - m)
        p_lat = jnp.exp2(s_lat - m)
        den = jnp.sum(p_ctx, axis=-1, keepdims=True) + jnp.sum(p_lat, axis=-1, keepdims=True)
        outs.append((_dot(p_ctx.astype(BF16), vc) + _dot(p_lat.astype(BF16), v)) / den)
    o = outs[0] - lam * outs[1]
    o_ref[...] = (_rms_lanes(o, g_ref[...]) * (1.0 - lam_init)).astype(BF16)


def attn_b_lat(u_attn, cache_k, cache_v, b_lambda, b_subln, rope_c, rope_s, lam_init, layer,
               batch, seq, row0, heads, col0, tq=256):
    qcol = col0 // HD
    kcol = qcol + heads
    vcol = kcol + heads
    nq = seq // tq
    past = cache_k.shape[2]
    rq0 = row0 // tq
    rs0 = row0 // seq
    return pl.pallas_call(
        functools.partial(_attn_b_lat_body, lam_init=lam_init),
        grid=(batch, heads, nq),
        in_specs=[pl.BlockSpec(b_lambda.shape, lambda b, h, n: (0, 0)),
                  pl.BlockSpec((1, HD), lambda b, h, n: (0, 0)),
                  pl.BlockSpec((tq, HD), lambda b, h, n: (rq0 + b * nq + n, qcol + h)),
                  pl.BlockSpec((seq, HD), lambda b, h, n: (rs0 + b, kcol + h)),
                  pl.BlockSpec((seq, HD), lambda b, h, n: (rs0 + b, vcol + h)),
                  pl.BlockSpec((None, None, past, HD), lambda b, h, n: (b, layer, 0, h)),
                  pl.BlockSpec((None, None, past, HD), lambda b, h, n: (b, layer, 0, h)),
                  pl.BlockSpec((tq, HD), lambda b, h, n: (n, 0)),
                  pl.BlockSpec((tq, HD), lambda b, h, n: (n, 0)),
                  pl.BlockSpec((seq, HD), lambda b, h, n: (0, 0)),
                  pl.BlockSpec((seq, HD), lambda b, h, n: (0, 0))],
        out_specs=pl.BlockSpec((tq, HD), lambda b, h, n: (b * nq + n, h)),
        out_shape=jax.ShapeDtypeStruct((batch * seq, heads * HD), BF16),
        scratch_shapes=[pltpu.VMEM((HD, seq), BF16), pltpu.VMEM((HD, past), BF16), pltpu.VMEM((seq, HD), BF16)],
        compiler_params=_cp(("arbitrary", "arbitrary", "arbitrary"), 40),
        name="attn_b_lat",
    )(b_lambda, b_subln.reshape(1, HD), u_attn, u_attn, u_attn, cache_k, cache_v,
      rope_c, rope_s, rope_c, rope_s)


def dft_tables(l):
    f = jnp.arange(l, dtype=jnp.int32)[:, None]
    s = jnp.arange(l, dtype=jnp.int32)[None, :]
    m = ((2 * f + 1) * s) % (4 * l)
    ang = m.astype(F32) * (math.pi / (2 * l))
    fm = jnp.concatenate([jnp.cos(ang), -jnp.sin(ang)], axis=0)
    hi = fm.astype(BF16)
    lo = (fm - hi.astype(F32)).astype(BF16)
    return hi, lo, hi.T


def hyena_feats(l):
    t = jnp.linspace(0.0, 1.0, l, dtype=F32)[:, None]
    bands = (HY_EMB - 1) // 2
    w = 2.0 * math.pi * jnp.arange(l, dtype=F32)[:, None] / l
    fr = jnp.linspace(1e-4, bands - 1, bands, dtype=F32)[None, :]
    feats = jnp.concatenate([t, jnp.cos(fr * w), -jnp.sin(fr * w)], axis=-1)
    return jnp.pad(feats, ((0, 0), (0, HD - HY_EMB)))


def hyena_decay(l, width):
    t = jnp.linspace(0.0, 1.0, l, dtype=F32)[:, None]
    deltas = jnp.abs(jnp.linspace(math.log(HY_TARGET) / HY_SLOW, math.log(HY_TARGET) / HY_FAST, width, dtype=F32))
    return jnp.exp(-t * deltas)


def _hy_time_body(feat_ref, dec_ref, w1_ref, b1_ref, w2_ref, b2_ref, w3_ref, fr_ref, kh_ref, kl_ref, nrm_scr,
                  *, width, l):
    ps = pl.program_id(2)
    c = pl.program_id(3)
    fr = fr_ref[...]
    h = jnp.sin(fr[0:1] * (_dot3(feat_ref[...], w1_ref[...]) + b1_ref[...]))
    h = jnp.sin(fr[1:2] * (_dot3(h, w2_ref[...]) + b2_ref[...]))
    dec = dec_ref[...]
    row = lax.broadcasted_iota(jnp.int32, dec.shape, 0)
    fwd = _dot3(h, w3_ref[:, :width]) * dec
    bwd = jnp.where((row == 0) & (c == 0), 0.0, _dot3(h, w3_ref[:, width:]) * dec)

    @pl.when((ps == 0) & (c == 0))
    def _():
        nrm_scr[...] = jnp.zeros_like(nrm_scr)

    @pl.when(ps == 0)
    def _():
        nrm_scr[...] += (jnp.sum(jnp.abs(fwd), axis=0, keepdims=True)
                         + jnp.sum(jnp.abs(bwd), axis=0, keepdims=True))

    @pl.when(ps == 1)
    def _():
        inv = 1.0 / ((nrm_scr[...] + EPS) * l)
        for part, val in ((0, fwd * inv), (1, bwd * inv)):
            hi, lo = _split(val)
            kh_ref[:, part * width:(part + 1) * width] = hi
            kl_ref[:, part * width:(part + 1) * width] = lo


def _hy_spec_body(fh_ref, fl_ref, kh_ref, kl_ref, o_ref, *, width, n_re):
    r = pl.program_id(2)
    fh = fh_ref[...]
    kh = kh_ref[...]
    res = _dot(fh, kh) + _dot(fh, kl_ref[...]) + _dot(fl_ref[...], kh)
    sign = jnp.where(r < n_re, 1.0, -1.0)
    o_ref[...] = res[:, :width] + sign * res[:, width:]


def hyena_filters(l, hy_w1p, hy_b1, hy_w2, hy_b2, hy_w3, hy_freq, f_hi, f_lo, width, rc=512, tr=512):
    depth = hy_w3.shape[0]
    ffn_w = hy_w2.shape[1]
    rc = min(rc, l)
    tr = min(tr, l)
    n_re = l // tr
    feats = hyena_feats(l)
    dec = hyena_decay(l, width)
    wspec = lambda shape: pl.BlockSpec((None,) + shape, lambda d, o, p, c: (d, 0, 0))
    kspec = pl.BlockSpec((None, rc, 2 * width), lambda d, o, p, c: (d, c * p, o))
    k_hi, k_lo = pl.pallas_call(
        functools.partial(_hy_time_body, width=width, l=l),
        grid=(depth, 2, 2, l // rc),
        in_specs=[pl.BlockSpec((rc, HD), lambda d, o, p, c: (c, 0)),
                  pl.BlockSpec((rc, width), lambda d, o, p, c: (c, 0)),
                  wspec((HD, ffn_w)), wspec((1, ffn_w)), wspec((ffn_w, ffn_w)), wspec((1, ffn_w)),
                  pl.BlockSpec((None, ffn_w, 2 * width), lambda d, o, p, c: (d, 0, o)),
                  wspec((2, ffn_w))],
        out_specs=[kspec, kspec],
        out_shape=[jax.ShapeDtypeStruct((depth, l, 4 * width), BF16)] * 2,
        scratch_shapes=[pltpu.VMEM((1, width), F32)],
        compiler_params=_cp(("arbitrary",) * 4, 32),
        name="hyena_time_filters",
    )(feats, dec, hy_w1p, hy_b1.reshape(depth, 1, ffn_w), hy_w2, hy_b2.reshape(depth, 1, ffn_w), hy_w3, hy_freq)
    fspec = pl.BlockSpec((tr, l), lambda d, o, r: (r, 0))
    kfull = pl.BlockSpec((None, l, 2 * width), lambda d, o, r: (d, 0, o))
    return pl.pallas_call(
        functools.partial(_hy_spec_body, width=width, n_re=n_re),
        grid=(depth, 2, 2 * n_re),
        in_specs=[fspec, fspec, kfull, kfull],
        out_specs=pl.BlockSpec((None, None, tr, width), lambda d, o, r: (d, o, r, 0)),
        out_shape=jax.ShapeDtypeStruct((depth, 2, 2 * l, width), F32),
        compiler_params=_cp(("arbitrary",) * 3, 48),
        name="hyena_spectra",
    )(f_hi, f_lo, k_hi, k_lo)


def _short_conv(u, w, b):
    l = u.shape[0]
    row = lax.broadcasted_iota(jnp.int32, u.shape, 0)
    up = jnp.where(row == 0, 0.0, pltpu.roll(u, 1, 0))
    dn = jnp.where(row == l - 1, 0.0, pltpu.roll(u, l - 1, 0))
    return up * w[0:1] + u * w[1:2] + dn * w[2:3] + b


def _hyena_body(hv_ref, h1_ref, h2_ref, wv_ref, w1_ref, w2_ref, bv_ref, b1_ref, b2_ref, bias_ref,
                fre_ref, fim_ref, gre_ref, gim_ref, kre_ref, kim_ref, o_ref,
                z_scr, x1_scr, x2_scr, zb_scr, acc_scr, *, nf):
    ph = pl.program_id(2)
    j = pl.program_id(3)

    @pl.when((ph == 0) & (j == 0))
    def _():
        v = _short_conv(hv_ref[...], wv_ref[...], bv_ref[...])
        z_scr[...] = v
        zb_scr[...] = v.astype(BF16)
        x1_scr[...] = _short_conv(h1_ref[...], w1_ref[...], b1_ref[...])
        x2_scr[...] = _short_conv(h2_ref[...], w2_ref[...], b2_ref[...])

    @pl.when(j == 0)
    def _():
        acc_scr[...] = jnp.zeros_like(acc_scr)

    zb = zb_scr[...]
    zre = _dot(fre_ref[...], zb)
    zim = _dot(fim_ref[...], zb)
    kre, kim = kre_ref[...], kim_ref[...]
    yre = (zre * kre - zim * kim).astype(BF16)
    yim = (zre * kim + zim * kre).astype(BF16)
    acc_scr[...] += _dot(gre_ref[...], yre) + _dot(gim_ref[...], yim)

    @pl.when((j == nf - 1) & (ph == 0))
    def _():
        y = x1_scr[...] * (acc_scr[...] + z_scr[...] * bias_ref[0:1])
        z_scr[...] = y
        zb_scr[...] = y.astype(BF16)

    @pl.when((j == nf - 1) & (ph == 1))
    def _():
        o_ref[...] = (x2_scr[...] * (acc_scr[...] + z_scr[...] * bias_ref[1:2])).astype(BF16)


def hyena(u_hy, conv_w, conv_b, bias, kf, f_hi, g_hi, layer, batch, seq, row0, width, cw=256, tfq=256):
    nch = width // cw
    tfq = min(tfq, seq)
    nf = seq // tfq
    rb = row0 // seq
    hspec = lambda part: pl.BlockSpec((seq, cw), lambda b, c, p, j: (rb + b, part * nch + c))
    wspec = lambda part: pl.BlockSpec((3, cw), lambda b, c, p, j: (0, part * nch + c))
    bspec = lambda part: pl.BlockSpec((1, cw), lambda b, c, p, j: (0, part * nch + c))
    return pl.pallas_call(
        functools.partial(_hyena_body, nf=nf),
        grid=(batch, nch, 2, nf),
        in_specs=[hspec(0), hspec(1), hspec(2), wspec(0), wspec(1), wspec(2), bspec(0), bspec(1), bspec(2),
                  pl.BlockSpec((2, cw), lambda b, c, p, j: (0, c)),
                  pl.BlockSpec((tfq, seq), lambda b, c, p, j: (j, 0)),
                  pl.BlockSpec((tfq, seq), lambda b, c, p, j: (nf + j, 0)),
                  pl.BlockSpec((seq, tfq), lambda b, c, p, j: (0, j)),
                  pl.BlockSpec((seq, tfq), lambda b, c, p, j: (0, nf + j)),
                  pl.BlockSpec((None, None, tfq, cw), lambda b, c, p, j: (layer, p, j, c)),
                  pl.BlockSpec((None, None, tfq, cw), lambda b, c, p, j: (layer, p, nf + j, c))],
        out_specs=pl.BlockSpec((seq, cw), lambda b, c, p, j: (b, c)),
        out_shape=jax.ShapeDtypeStruct((batch * seq, width), BF16),
        scratch_shapes=[pltpu.VMEM((seq, cw), F32), pltpu.VMEM((seq, cw), F32), pltpu.VMEM((seq, cw), F32),
                        pltpu.VMEM((seq, cw), BF16), pltpu.VMEM((seq, cw), F32)],
        compiler_params=_cp(("arbitrary",) * 4, 48),
        name="hyena",
    )(u_hy, u_hy, u_hy, conv_w, conv_w, conv_w, conv_b, conv_b, conv_b, bias,
      f_hi, f_hi, g_hi, g_hi, kf, kf)


def _hgrn_pair_level(t, s, fwd):
    x = t ^ s
    lvl = jnp.where(x == 0, 0, 32 - lax.clz(x))
    used = (s <= t) if fwd else (s >= t)
    return jnp.where(used, lvl, -1)


def _hgrn_midpoint(b3, blk, fwd):
    rows = b3.shape[1]
    r0 = blk // 2 - 1 if fwd else blk // 2
    ref = b3[:, r0:r0 + 1, :]
    if rows > blk:
        pos = lax.broadcasted_iota(jnp.int32, b3.shape, 1)
        for j in range(1, rows // blk):
            ref = jnp.where(pos >= j * blk, b3[:, j * blk + r0:j * blk + r0 + 1, :], ref)
    return ref


def _hgrn_chunk(q, k, g, v, st, tri, lvl, fwd):
    c = CHUNK
    gh, gl = _split(g)
    b = _dot(tri, gh) + _dot(tri, gl)
    tot = b[c - 1:c] if fwd else b[0:1]
    vb = v.astype(BF16)
    o = _dot_nt((q * jnp.exp2(b)).astype(BF16), st.astype(BF16))
    kd = (k * jnp.exp2(tot - b)).astype(BF16)
    st_new = st * jnp.exp2(tot) + _dot_tn(vb, kd)

    a = jnp.where(lvl == 0, jnp.sum(q * k, axis=-1, keepdims=True), 0.0)
    a = jnp.where(lvl == 1, _dot_nt((q * jnp.exp2(g)).astype(BF16), k.astype(BF16)), a)
    blk, m = 4, 2
    while blk <= c:
        rows = max(blk, SUB)
        b3, q3, k3 = (x.reshape(c // rows, rows, HD) for x in (b, q, k))
        ref = _hgrn_midpoint(b3, blk, fwd)
        qt = (q3 * jnp.exp2(b3 - ref)).reshape(c, HD).astype(BF16)
        kt = (k3 * jnp.exp2(ref - b3)).reshape(c, HD).astype(BF16)
        a = jnp.where(lvl == m, _dot_nt(qt, kt), a)
        blk, m = blk * 2, m + 1
    return o + _dot(a.astype(BF16), vb), st_new


def _hgrn_gate(z, lb):
    e = jnp.exp(-jnp.abs(z))
    r = 1.0 / (1.0 + e)
    pos = z >= 0
    sig_pos = jnp.where(pos, r, e * r)
    sig_neg = jnp.where(pos, e * r, r)
    f = sig_pos + jnp.maximum(lb, LB_FLOOR) * sig_neg
    return jnp.log(f) * LOG2E, (1.0 - lb) * sig_neg


def _hgrn_lower_bound(lb_ref, layer, depth, direction):
    x = lb_ref[...]
    rows = [x[2 * i + direction:2 * i + direction + 1] for i in range(depth)]
    m = functools.reduce(jnp.maximum, rows)
    es = [jnp.exp(r - m) for r in rows]
    tot = functools.reduce(lambda a, b: a + b, es)
    cum = jnp.zeros_like(tot)
    for i in range(1, layer + 1):
        cum = cum + es[i]
    return jnp.maximum(cum / tot, 0.0)


def _hgrn_body(*refs, layer, depth, has_state, emit_state):
    it = iter(refs)
    lb_ref, ng_ref, q_ref, ff_ref, fb_ref, i_ref, g_ref = (next(it) for _ in range(7))
    s0_ref = next(it) if has_state else None
    o_ref = next(it)
    so_ref = next(it) if emit_state else None
    of_scr, ob_scr, st_scr, tri_scr, lvl_scr = (next(it) for _ in range(5))
    nc = q_ref.shape[0] // CHUNK
    hp = q_ref.shape[1] // HD
    ti = lax.broadcasted_iota(jnp.int32, (CHUNK, CHUNK), 0)
    si = lax.broadcasted_iota(jnp.int32, (CHUNK, CHUNK), 1)
    chains = [(h, d) for h in range(hp) for d in range(2)]
    lbs = {}
    for d in range(2):
        tri_scr[d] = jnp.where((si <= ti) if d == 0 else (si >= ti), 1.0, 0.0).astype(BF16)
        lvl_scr[d] = _hgrn_pair_level(ti, si, d == 0)
        lb_all = _hgrn_lower_bound(lb_ref, layer, depth, d)
        for h in range(hp):
            lbs[h, d] = lb_all[:, h * HD:(h + 1) * HD]
            st_scr[2 * h + d] = s0_ref[d, h].T if has_state else jnp.zeros((HD, HD), F32)

    def step(i, carry):
        for h, d in chains:
            fwd = d == 0
            ci = i if fwd else nc - 1 - i
            sl = pl.ds(pl.multiple_of(ci * CHUNK, CHUNK), CHUNK)
            cs = slice(h * HD, (h + 1) * HD)
            xq = q_ref[sl, cs]
            log_f, kk = _hgrn_gate((ff_ref if fwd else fb_ref)[sl, cs], lbs[h, d])
            o, st = _hgrn_chunk(xq * _sigmoid(xq), kk, log_f, i_ref[sl, cs], st_scr[2 * h + d],
                                tri_scr[d], lvl_scr[d], fwd)
            st_scr[2 * h + d] = st
            (of_scr if fwd else ob_scr)[sl, cs] = o
        return carry

    lax.fori_loop(0, nc, step, 0)
    for h, d in chains:
        if emit_state:
            so_ref[d, h] = st_scr[2 * h + d].T
    for h in range(hp):
        cs = slice(h * HD, (h + 1) * HD)
        xg = g_ref[:, cs]
        o_ref[:, cs] = (_rms_lanes(of_scr[:, cs] + ob_scr[:, cs], ng_ref[...]) * (xg * _sigmoid(xg))).astype(BF16)


def hgrn(u_hg, hg_lb2, hg_norm, state, layer, depth, batch, seq, row0, heads, emit_state, hp=2):
    rb = row0 // seq
    ng = heads // hp
    w = hp * HD
    part = lambda p: pl.BlockSpec((seq, w), lambda b, h: (rb + b, p * ng + h))
    in_specs = [pl.BlockSpec((2 * depth, w), lambda b, h: (0, h)),
                pl.BlockSpec((1, HD), lambda b, h: (0, 0)),
                part(0), part(1), part(2), part(3), part(4)]
    args = [hg_lb2, hg_norm.reshape(1, HD), u_hg, u_hg, u_hg, u_hg, u_hg]
    if state is not None:
        in_specs.append(pl.BlockSpec((None, None, 2, hp, HD, HD), lambda b, h: (b, layer, 0, h, 0, 0)))
        args.append(state)
    out_specs = [pl.BlockSpec((seq, w), lambda b, h: (b, h))]
    out_shape = [jax.ShapeDtypeStruct((batch * seq, heads * HD), BF16)]
    if emit_state:
        out_specs.append(pl.BlockSpec((None, 2, hp, HD, HD), lambda b, h: (b, 0, h, 0, 0)))
        out_shape.append(jax.ShapeDtypeStruct((batch, 2, heads, HD, HD), F32))
    return pl.pallas_call(
        functools.partial(_hgrn_body, layer=layer, depth=depth, has_state=state is not None,
                          emit_state=emit_state),
        grid=(batch, ng),
        in_specs=in_specs,
        out_specs=out_specs,
        out_shape=out_shape,
        scratch_shapes=[pltpu.VMEM((seq, w), F32), pltpu.VMEM((seq, w), F32),
                        pltpu.VMEM((2 * hp, HD, HD), F32),
                        pltpu.VMEM((2, CHUNK, CHUNK), BF16), pltpu.VMEM((2, CHUNK, CHUNK), jnp.int32)],
        compiler_params=_cp(("arbitrary", "arbitrary"), 48),
        name="hgrn",
    )(*args)


def _merge_body(*refs, n_ctx_tiles):
    o_refs, (gt_ref, x_ref, mod_ref, wb_ref, wo_ref, out_ref) = refs[:2 * N_BRANCH], refs[2 * N_BRANCH:]
    d = x_ref.shape[1]
    is_ctx = pl.program_id(0) < n_ctx_tiles
    acc = None
    for n in range(N_BRANCH):
        o = jnp.where(is_ctx, o_refs[2 * n][...], o_refs[2 * n + 1][...])
        y = gt_ref[:, n * d:(n + 1) * d].astype(F32) * _dot(o, wb_ref[n])
        acc = y if acc is None else acc + y
    out = _dot(acc.astype(BF16), wo_ref[...])
    out_ref[...] = x_ref[...] + mod_ref[...][5:6] * out


def merge(branch_outs, gates, x, mod_l, w_branch, w_out, t_ctx, l_lat, tm=256):
    t, d = x.shape
    mw = w_branch.shape[1]
    nct = t_ctx // tm
    row = functools.partial(_mod_row, tm=tm, t_ctx=t_ctx, l_lat=l_lat)
    cspec = pl.BlockSpec((tm, mw), lambda i: (jnp.minimum(i, nct - 1), 0))
    lspec = pl.BlockSpec((tm, mw), lambda i: (jnp.maximum(i - nct, 0), 0))
    return pl.pallas_call(
        functools.partial(_merge_body, n_ctx_tiles=nct),
        grid=(t // tm,),
        in_specs=[cspec, lspec] * N_BRANCH + [
                  pl.BlockSpec((tm, N_BRANCH * d), lambda i: (i, 0)),
                  pl.BlockSpec((tm, d), lambda i: (i, 0)),
                  pl.BlockSpec((None, N_MOD, d), lambda i: (row(i), 0, 0)),
                  pl.BlockSpec((N_BRANCH, mw, d), lambda i: (0, 0, 0), pipeline_mode=pl.Buffered(1)),
                  pl.BlockSpec((d, d), lambda i: (0, 0), pipeline_mode=pl.Buffered(1))],
        out_specs=pl.BlockSpec((tm, d), lambda i: (i, 0)),
        out_shape=jax.ShapeDtypeStruct((t, d), F32),
        compiler_params=_cp(("arbitrary",), 56),
        name="merge",
    )(*[o for pair in branch_outs for o in pair], gates, x, mod_l, w_branch, w_out)


def _final_norm_body(x_ref, g_ref, o_ref):
    o_ref[...] = _rms_lanes(x_ref[...], g_ref[...])


def final_norm(x, g, tm=512):
    t, d = x.shape
    return pl.pallas_call(
        _final_norm_body,
        grid=(t // tm,),
        in_specs=[pl.BlockSpec((tm, d), lambda i: (i, 0)), pl.BlockSpec((1, d), lambda i: (0, 0))],
        out_specs=pl.BlockSpec((tm, d), lambda i: (i, 0)),
        out_shape=jax.ShapeDtypeStruct((t, d), F32),
        compiler_params=_cp(("arbitrary",), 32),
        name="final_norm",
    )(x, g.reshape(1, d))


def kernel(x_prompt, x_sample, c, cache_a_k, cache_a_v, cache_b_k, cache_b_v, state_hgrn, c_ctx, w_mod, b_mod, norm_g, w_ffn1_gu, w_ffn1_down, w_ffn2_gu, w_ffn2_down, w_in, w_branch, w_out, a_sink, b_lambda, b_subln, hy_conv_w, hy_conv_b, hy_w1, hy_b1, hy_w2, hy_b2, hy_w3, hy_freq, hy_bias, hg_lb, hg_norm, final_g):
    batch, seq, d = x_prompt.shape
    dec_batch, dec_seq, _ = x_sample.shape
    depth = w_mod.shape[0]
    mix_w = w_branch.shape[2]
    a_heads = a_sink.shape[1]
    a_kvh = cache_a_k.shape[3]
    a_group = a_heads // a_kvh
    b_heads = cache_b_k.shape[3]
    hg_heads = state_hgrn.shape[3]
    t_ctx, t_lat = batch * seq, dec_batch * dec_seq

    n_attn = (a_heads + 2 * a_kvh + 3 * b_heads) * HD
    sizes = (n_attn, 3 * mix_w, 5 * mix_w, N_BRANCH * d)
    b_col0 = (a_heads + 2 * a_kvh) * HD

    x = jnp.concatenate([x_prompt.reshape(t_ctx, d), x_sample.reshape(t_lat, d)], axis=0)
    cond = jnp.concatenate([c_ctx[None, :], c], axis=0)
    cond = jnp.pad(cond, ((0, MOD_ROWS - cond.shape[0]), (0, 0)))
    mod = modulation(cond, w_mod, b_mod).reshape(depth, MOD_ROWS, N_MOD, d)

    rope_a = rope_tables(dec_seq, HD, 1)
    rope_b = rope_tables(dec_seq, HD // 2, 2)
    fc_hi, fc_lo, gc_hi = dft_tables(seq)
    fl_hi, fl_lo, gl_hi = dft_tables(dec_seq)
    w1p = jnp.pad(hy_w1, ((0, 0), (0, HD - hy_w1.shape[1]), (0, 0)))
    kf_ctx = hyena_filters(seq, w1p, hy_b1, hy_w2, hy_b2, hy_w3, hy_freq, fc_hi, fc_lo, mix_w)
    kf_lat = hyena_filters(dec_seq, w1p, hy_b1, hy_w2, hy_b2, hy_w3, hy_freq, fl_hi, fl_lo, mix_w)

    cak = cache_a_k.reshape(dec_batch, depth, -1, a_kvh * HD)
    cav = cache_a_v.reshape(dec_batch, depth, -1, a_kvh * HD)
    cbk = cache_b_k.reshape(dec_batch, depth, -1, b_heads * HD)
    cbv = cache_b_v.reshape(dec_batch, depth, -1, b_heads * HD)
    hg_lb2 = hg_lb.reshape(depth * 2, mix_w)

    ak_l, av_l, bk_l, bv_l, st_l = [], [], [], [], []
    for l in range(depth):
        lam_init = 0.8 - 0.6 * math.exp(-0.3 * l)
        mod_l = mod[l]
        x = ffn(x, mod_l, norm_g[l, 0:1], w_ffn1_gu[l].astype(BF16), w_ffn1_down[l].astype(BF16), 0,
                t_ctx, dec_seq)
        u_attn, u_hy, u_hg, gates = inproj(x, mod_l, norm_g[l, 1:2], w_in[l].astype(BF16), sizes, t_ctx, dec_seq)

        uc = u_attn[:t_ctx]
        k0 = a_heads * HD
        ak_l.append(uc[:, k0:k0 + a_kvh * HD].reshape(batch, seq, a_kvh, HD))
        av_l.append(uc[:, k0 + a_kvh * HD:b_col0].reshape(batch, seq, a_kvh, HD))
        k1 = b_col0 + b_heads * HD
        bk_l.append(uc[:, k1:k1 + b_heads * HD].reshape(batch, seq, b_heads, 2, HD // 2))
        bv_l.append(uc[:, k1 + b_heads * HD:].reshape(batch, seq, b_heads, HD))

        oa_c = attn_a_ctx(u_attn, a_sink[l], batch, seq, a_kvh, a_group)
        oa_s = attn_a_lat(u_attn, cak, cav, a_sink[l], rope_a[0], rope_a[1], l, dec_batch, dec_seq, t_ctx,
                          a_kvh, a_group)
        ob_c = attn_b_ctx(u_attn, b_lambda[l], b_subln[l], lam_init, batch, seq, b_heads, b_col0)
        ob_s = attn_b_lat(u_attn, cbk, cbv, b_lambda[l], b_subln[l], rope_b[0], rope_b[1], lam_init, l,
                          dec_batch, dec_seq, t_ctx, b_heads, b_col0)
        cb = hy_conv_b[l].reshape(1, -1)
        oc_c = hyena(u_hy, hy_conv_w[l], cb, hy_bias[l], kf_ctx, fc_hi, gc_hi, l, batch, seq, 0, mix_w)
        oc_s = hyena(u_hy, hy_conv_w[l], cb, hy_bias[l], kf_lat, fl_hi, gl_hi, l, dec_batch, dec_seq, t_ctx, mix_w)
        od_c, st = hgrn(u_hg, hg_lb2, hg_norm[l], None, l, depth, batch, seq, 0, hg_heads, True)
        od_s, = hgrn(u_hg, hg_lb2, hg_norm[l], state_hgrn, l, depth, dec_batch, dec_seq, t_ctx, hg_heads, False)
        st_l.append(st)

        x = merge(((oa_c, oa_s), (ob_c, ob_s), (oc_c, oc_s), (od_c, od_s)), gates, x, mod_l,
                  w_branch[l].astype(BF16), w_out[l].astype(BF16), t_ctx, dec_seq)
        x = ffn(x, mod_l, norm_g[l, 2:3], w_ffn2_gu[l].astype(BF16), w_ffn2_down[l].astype(BF16), 2,
                t_ctx, dec_seq)

    y = final_norm(x, final_g)
    return (y[:t_ctx].reshape(batch, seq, d), y[t_ctx:].reshape(dec_batch, dec_seq, d),
            jnp.stack(ak_l, axis=1), jnp.stack(av_l, axis=1), jnp.stack(bk_l, axis=1), jnp.stack(bv_l, axis=1),
            jnp.stack(st_l, axis=1))
```

```python
import functools
import math

import jax
import jax.numpy as jnp
import numpy as np
from jax import lax
from jax.experimental import pallas as pl
from jax.experimental.pallas import tpu as pltpu

F32 = jnp.float32
BF16 = jnp.bfloat16

EPS = 1e-6
NEG = -1e30
LB_FLOOR = 1e-30
ROPE_BASE = 10000.0
GRID_W = 64
N_MOD = 9
N_BRANCH = 4
HD = 128
A_WINDOW = 128
HY_EMB = 33
HY_TARGET, HY_FAST, HY_SLOW = 1e-2, 0.3, 1.5
CHUNK = 128
SUB = 8
LOG2E = 1.4426950408889634
MOD_ROWS = 16
VMEM_MB = 2 ** 20


def _cp(sem, vmem_mb):
    return pltpu.CompilerParams(dimension_semantics=sem, vmem_limit_bytes=vmem_mb * VMEM_MB)


def _dot(a, b):
    return jnp.dot(a, b, preferred_element_type=F32)


def _dot_nt(a, b):
    return lax.dot_general(a, b, (((1,), (1,)), ((), ())), preferred_element_type=F32)


def _dot_tn(a, b):
    return lax.dot_general(a, b, (((0,), (0,)), ((), ())), preferred_element_type=F32)


def _split(a):
    hi = a.astype(BF16)
    lo = (a - hi.astype(F32)).astype(BF16)
    return hi, lo


def _dot3(a, b):
    ah, al = _split(a)
    bh, bl = _split(b)
    return _dot(ah, bh) + _dot(ah, bl) + _dot(al, bh)


def _sigmoid(x):
    return jax.nn.sigmoid(x)


def _norm_mod(x, g, sc, sh):
    y = x * lax.rsqrt(jnp.mean(x * x, axis=-1, keepdims=True) + EPS)
    return (y * g) * (1.0 + sc) + sh


def _rms_lanes(x, g):
    return x * lax.rsqrt(jnp.mean(x * x, axis=-1, keepdims=True) + EPS) * g


def _mod_body(c_ref, w_ref, b_ref, o_ref):
    c = c_ref[...]
    a = (c * _sigmoid(c)).astype(BF16)
    o_ref[...] = _dot(a, w_ref[...].astype(BF16)) + b_ref[...]


def modulation(cond, w_mod, b_mod, tn=1024):
    depth, d, n = w_mod.shape
    return pl.pallas_call(
        _mod_body,
        grid=(depth, n // tn),
        in_specs=[pl.BlockSpec((MOD_ROWS, d), lambda l, j: (0, 0)),
                  pl.BlockSpec((None, d, tn), lambda l, j: (l, 0, j)),
                  pl.BlockSpec((None, 1, tn), lambda l, j: (l, 0, j))],
        out_specs=pl.BlockSpec((None, MOD_ROWS, tn), lambda l, j: (l, 0, j)),
        out_shape=jax.ShapeDtypeStruct((depth, MOD_ROWS, n), F32),
        compiler_params=_cp(("arbitrary", "arbitrary"), 40),
        name="modulation",
    )(cond, w_mod, b_mod.reshape(depth, 1, n))


def _mod_row(i, tm, t_ctx, l_lat):
    start = i * tm
    return jnp.where(start < t_ctx, 0, 1 + (start - t_ctx) // l_lat)


def _ffn_body(*refs, sub, nj, next_sub):
    if next_sub is None:
        x_ref, mod_ref, g_ref, wg_ref, wu_ref, wd_ref, o_ref, h_scr = refs
    else:
        x_ref, mod_ref, g_ref, gn_ref, wg_ref, wu_ref, wd_ref, o_ref, hn_ref, h_scr = refs
    j = pl.program_id(1)

    @pl.when(j == 0)
    def _():
        m = mod_ref[...]
        h = _norm_mod(x_ref[...], g_ref[...], m[3 * sub + 1:3 * sub + 2], m[3 * sub:3 * sub + 1])
        h_scr[...] = h.astype(BF16)
        o_ref[...] = jnp.zeros_like(o_ref)

    h = h_scr[...]
    a = _dot(h, wg_ref[...])
    b = _dot(h, wu_ref[...])
    act = (a * _sigmoid(a) * b).astype(BF16)
    o_ref[...] += _dot(act, wd_ref[...])

    @pl.when(j == nj - 1)
    def _():
        m = mod_ref[...]
        y = x_ref[...] + 0.5 * m[3 * sub + 2:3 * sub + 3] * o_ref[...]
        o_ref[...] = y
        if next_sub is not None:
            hn_ref[...] = _norm_mod(y, gn_ref[...], m[3 * next_sub + 1:3 * next_sub + 2],
                                    m[3 * next_sub:3 * next_sub + 1]).astype(BF16)


def ffn(x, mod_l, g_row, w_gu, w_dn, sub, t_ctx, l_lat, next_g_row=None, tm=512, tf=512):
    t, d = x.shape
    dff = w_dn.shape[0]
    nj = dff // tf
    emit = next_g_row is not None
    row = functools.partial(_mod_row, tm=tm, t_ctx=t_ctx, l_lat=l_lat)
    gspec = pl.BlockSpec((1, d), lambda i, j: (0, 0))
    xspec = pl.BlockSpec((tm, d), lambda i, j: (i, 0))
    outs = pl.pallas_call(
        functools.partial(_ffn_body, sub=sub, nj=nj, next_sub=sub + 1 if emit else None),
        grid=(t // tm, nj),
        in_specs=[xspec,
                  pl.BlockSpec((None, N_MOD, d), lambda i, j: (row(i), 0, 0)),
                  gspec] + ([gspec] if emit else []) + [
                  pl.BlockSpec((d, tf), lambda i, j: (0, j)),
                  pl.BlockSpec((d, tf), lambda i, j: (0, j + nj)),
                  pl.BlockSpec((tf, d), lambda i, j: (j, 0))],
        out_specs=[xspec] + ([xspec] if emit else []),
        out_shape=[jax.ShapeDtypeStruct((t, d), F32)] + ([jax.ShapeDtypeStruct((t, d), BF16)] if emit else []),
        scratch_shapes=[pltpu.VMEM((tm, d), BF16)],
        compiler_params=_cp(("arbitrary", "arbitrary"), 56),
        name="ffn",
    )(x, mod_l, g_row, *([next_g_row] if emit else []), w_gu, w_gu, w_dn)
    return outs if emit else outs[0]


def _proj_body(h_ref, w_ref, o_ref, *, gate):
    r = _dot(h_ref[...], w_ref[...])
    o_ref[...] = (_sigmoid(r) if gate else r).astype(o_ref.dtype)


def proj(h, w, out_dtype, gate, tm=2048, tn=512):
    t, d = h.shape
    n = w.shape[1]
    while t % tm:
        tm //= 2
    return pl.pallas_call(
        functools.partial(_proj_body, gate=gate),
        grid=(t // tm, n // tn),
        in_specs=[pl.BlockSpec((tm, d), lambda i, j: (i, 0)),
                  pl.BlockSpec((d, tn), lambda i, j: (0, j))],
        out_specs=pl.BlockSpec((tm, tn), lambda i, j: (i, j)),
        out_shape=jax.ShapeDtypeStruct((t, n), out_dtype),
        compiler_params=_cp(("arbitrary", "arbitrary"), 56),
        name="proj",
    )(h, w)


def rope_tables(l, rot_dim, reps):
    n_rows = l // GRID_W
    rows = jnp.broadcast_to(jnp.arange(n_rows, dtype=F32)[:, None], (n_rows, GRID_W)).reshape(-1)
    cols = jnp.broadcast_to(jnp.arange(GRID_W, dtype=F32)[None, :], (n_rows, GRID_W)).reshape(-1)
    axis_dim = rot_dim // 2
    inv = ROPE_BASE ** (-jnp.arange(0, axis_dim, 2, dtype=F32) / axis_dim)
    ang = jnp.concatenate([rows[:, None] * inv, cols[:, None] * inv], axis=-1)
    cos = jnp.repeat(jnp.cos(ang), 2, axis=-1)
    sin = jnp.repeat(jnp.sin(ang), 2, axis=-1)
    sign = jnp.tile(jnp.array([-1.0, 1.0], F32), rot_dim // 2)
    return jnp.tile(cos, (1, reps)), jnp.tile(sin * sign, (1, reps))


def _rope(x, c, s):
    lane = lax.broadcasted_iota(jnp.int32, x.shape, 1)
    nxt = pltpu.roll(x, x.shape[1] - 1, 1)
    prv = pltpu.roll(x, 1, 1)
    return x * c + jnp.where((lane & 1) == 0, nxt, prv) * s


def _attn_a_ctx_body(sink_ref, q_ref, k_ref, v_ref, o_ref, *, group):
    kh = pl.program_id(1)
    scale = HD ** -0.5
    k = k_ref[...].astype(BF16)
    v = v_ref[...].astype(BF16)
    for g in range(group):
        q = q_ref[:, g * HD:(g + 1) * HD].astype(BF16)
        s = _dot_nt(q, k) * scale
        sink = sink_ref[kh * group + g]
        m = jnp.maximum(jnp.max(s, axis=-1, keepdims=True), sink)
        p = jnp.exp(s - m)
        den = jnp.sum(p, axis=-1, keepdims=True) + jnp.exp(sink - m)
        o_ref[:, g * HD:(g + 1) * HD] = (_dot(p.astype(BF16), v) / den).astype(BF16)


def attn_a_ctx(u_attn, sink, batch, seq, kvh, group):
    qw = group * HD
    kcol = kvh * group
    vcol = kcol + kvh
    return pl.pallas_call(
        functools.partial(_attn_a_ctx_body, group=group),
        grid=(batch, kvh),
        in_specs=[pl.BlockSpec(memory_space=pltpu.SMEM),
                  pl.BlockSpec((seq, qw), lambda b, h: (b, h)),
                  pl.BlockSpec((seq, HD), lambda b, h: (b, kcol + h)),
                  pl.BlockSpec((seq, HD), lambda b, h: (b, vcol + h))],
        out_specs=pl.BlockSpec((seq, qw), lambda b, h: (b, h)),
        out_shape=jax.ShapeDtypeStruct((batch * seq, kvh * qw), BF16),
        compiler_params=_cp(("arbitrary", "arbitrary"), 32),
        name="attn_a_ctx",
    )(sink, u_attn, u_attn, u_attn)


def _attn_a_lat_body(sink_ref, q_ref, k0_ref, k1_ref, k2_ref, v0_ref, v1_ref, v2_ref, kc_ref, vc_ref,
                     cq_ref, sq_ref, c0_ref, s0_ref, c2_ref, s2_ref, o_ref, *, group, seq):
    kh = pl.program_id(1)
    n = pl.program_id(2)
    blk = q_ref.shape[0]
    scale = HD ** -0.5
    cq, sq = cq_ref[...], sq_ref[...]
    kw = jnp.concatenate([_rope(k0_ref[...], c0_ref[...], s0_ref[...]),
                          _rope(k1_ref[...], cq, sq),
                          _rope(k2_ref[...], c2_ref[...], s2_ref[...])], axis=0).astype(BF16)
    vw = jnp.concatenate([v0_ref[...], v1_ref[...], v2_ref[...]], axis=0).astype(BF16)
    kc = kc_ref[...].astype(BF16)
    vc = vc_ref[...].astype(BF16)
    qi = lax.broadcasted_iota(jnp.int32, (blk, 3 * blk), 0)
    kj = lax.broadcasted_iota(jnp.int32, (blk, 3 * blk), 1)
    kpos = (n - 1) * blk + kj
    qpos = n * blk + qi
    valid = (kpos >= 0) & (kpos < seq) & (jnp.abs(qpos - kpos) <= A_WINDOW)
    for g in range(group):
        q = _rope(q_ref[:, g * HD:(g + 1) * HD], cq, sq).astype(BF16)
        s_loc = jnp.where(valid, _dot_nt(q, kw) * scale, NEG)
        s_ctx = _dot_nt(q, kc) * scale
        sink = sink_ref[kh * group + g]
        m = jnp.maximum(jnp.maximum(jnp.max(s_loc, axis=-1, keepdims=True),
                                    jnp.max(s_ctx, axis=-1, keepdims=True)), sink)
        p_loc = jnp.exp(s_loc - m)
        p_ctx = jnp.exp(s_ctx - m)
        den = (jnp.sum(p_loc, axis=-1, keepdims=True) + jnp.sum(p_ctx, axis=-1, keepdims=True)
               + jnp.exp(sink - m))
        o = _dot(p_ctx.astype(BF16), vc) + _dot(p_loc.astype(BF16), vw)
        o_ref[:, g * HD:(g + 1) * HD] = (o / den).astype(BF16)


def attn_a_lat(u_attn, cache_k, cache_v, sink, rope_c, rope_s, layer, batch, seq, row0, kvh, group):
    blk = HD
    nb = seq // blk
    rb0 = row0 // blk
    qw = group * HD
    kcol = kvh * group
    vcol = kcol + kvh
    past = cache_k.shape[2]

    def rows(b, n):
        return rb0 + b * nb + n

    prev = lambda n: jnp.maximum(n - 1, 0)
    nxt = lambda n: jnp.minimum(n + 1, nb - 1)
    tab = lambda f: pl.BlockSpec((blk, HD), lambda b, h, n: (f(n), 0))
    return pl.pallas_call(
        functools.partial(_attn_a_lat_body, group=group, seq=seq),
        grid=(batch, kvh, nb),
        in_specs=[pl.BlockSpec(memory_space=pltpu.SMEM),
                  pl.BlockSpec((blk, qw), lambda b, h, n: (rows(b, n), h)),
                  pl.BlockSpec((blk, HD), lambda b, h, n: (rows(b, prev(n)), kcol + h)),
                  pl.BlockSpec((blk, HD), lambda b, h, n: (rows(b, n), kcol + h)),
                  pl.BlockSpec((blk, HD), lambda b, h, n: (rows(b, nxt(n)), kcol + h)),
                  pl.BlockSpec((blk, HD), lambda b, h, n: (rows(b, prev(n)), vcol + h)),
                  pl.BlockSpec((blk, HD), lambda b, h, n: (rows(b, n), vcol + h)),
                  pl.BlockSpec((blk, HD), lambda b, h, n: (rows(b, nxt(n)), vcol + h)),
                  pl.BlockSpec((None, None, past, HD), lambda b, h, n: (b, layer, 0, h)),
                  pl.BlockSpec((None, None, past, HD), lambda b, h, n: (b, layer, 0, h)),
                  tab(lambda n: n), tab(lambda n: n), tab(prev), tab(prev), tab(nxt), tab(nxt)],
        out_specs=pl.BlockSpec((blk, qw), lambda b, h, n: (b * nb + n, h)),
        out_shape=jax.ShapeDtypeStruct((batch * seq, kvh * qw), BF16),
        compiler_params=_cp(("arbitrary", "arbitrary", "arbitrary"), 32),
        name="attn_a_lat",
    )(sink, u_attn, u_attn, u_attn, u_attn, u_attn, u_attn, u_attn, cache_k, cache_v,
      rope_c, rope_s, rope_c, rope_s, rope_c, rope_s)


def _lambda(lam_ref, lam_init):
    lw = lam_ref[...]
    return (jnp.exp(jnp.sum(lw[0:1] * lw[1:2], axis=-1, keepdims=True))
            - jnp.exp(jnp.sum(lw[2:3] * lw[3:4], axis=-1, keepdims=True)) + lam_init)


def _softmax_parts(parts):
    m = functools.reduce(jnp.maximum, [jnp.max(s, axis=-1, keepdims=True) for s in parts])
    ps = [jnp.exp(s - m) for s in parts]
    inv = 1.0 / functools.reduce(lambda a, b: a + b, [jnp.sum(p, axis=-1, keepdims=True) for p in ps])
    return [p * inv for p in ps]


def _attn_b_ctx_body(lam_ref, g_ref, q_ref, k_ref, v_ref, o_ref, *, lam_init):
    hd = HD // 2
    scale = hd ** -0.5
    lam = _lambda(lam_ref, lam_init)
    q = q_ref[...].astype(BF16)
    k = k_ref[...].astype(BF16)
    p0, = _softmax_parts([_dot_nt(q[:, :hd], k[:, :hd]) * scale])
    p1, = _softmax_parts([_dot_nt(q[:, hd:], k[:, hd:]) * scale])
    o = _dot((p0 - lam * p1).astype(BF16), v_ref[...].astype(BF16))
    o_ref[...] = (_rms_lanes(o, g_ref[...]) * (1.0 - lam_init)).astype(BF16)


def attn_b_ctx(u_attn, b_lambda, b_subln, lam_init, batch, seq, heads, col0):
    qcol = col0 // HD
    kcol = qcol + heads
    vcol = kcol + heads
    return pl.pallas_call(
        functools.partial(_attn_b_ctx_body, lam_init=lam_init),
        grid=(batch, heads),
        in_specs=[pl.BlockSpec(b_lambda.shape, lambda b, h: (0, 0)),
                  pl.BlockSpec((1, HD), lambda b, h: (0, 0)),
                  pl.BlockSpec((seq, HD), lambda b, h: (b, qcol + h)),
                  pl.BlockSpec((seq, HD), lambda b, h: (b, kcol + h)),
                  pl.BlockSpec((seq, HD), lambda b, h: (b, vcol + h))],
        out_specs=pl.BlockSpec((seq, HD), lambda b, h: (b, h)),
        out_shape=jax.ShapeDtypeStruct((batch * seq, heads * HD), BF16),
        compiler_params=_cp(("arbitrary", "arbitrary"), 32),
        name="attn_b_ctx",
    )(b_lambda, b_subln.reshape(1, HD), u_attn, u_attn, u_attn)


def _attn_b_lat_body(lam_ref, g_ref, q_ref, k_ref, v_ref, kc_ref, vc_ref, cq_ref, sq_ref, ck_ref, sk_ref,
                     o_ref, k_scr, kc_scr, v_scr, *, lam_init):
    hd = HD // 2
    scale = hd ** -0.5
    nq = pl.program_id(2)

    @pl.when(nq == 0)
    def _():
        k_scr[...] = _rope(k_ref[...], ck_ref[...], sk_ref[...]).T.astype(BF16)
        kc_scr[...] = kc_ref[...].T.astype(BF16)
        v_scr[...] = v_ref[...].astype(BF16)

    lam = _lambda(lam_ref, lam_init)
    k = k_scr[...]
    v = v_scr[...]
    kc = kc_scr[...]
    vc = vc_ref[...].astype(BF16)
    q = _rope(q_ref[...], cq_ref[...], sq_ref[...]) * (scale * LOG2E)
    lane = lax.broadcasted_iota(jnp.int32, q.shape, 1)
    outs = []
    for c in range(2):
        qc = jnp.where((lane >= hd) == (c == 1), q, 0.0).astype(BF16)
        s_ctx = _dot(qc, kc)
        s_lat = _dot(qc, k)
        m = jnp.maximum(jnp.max(s_ctx, axis=-1, keepdims=True), jnp.max(s_lat, axis=-1, keepdims=True))
        p_ctx = jnp.exp2(s_ctx - m)
        p_lat = jnp.exp2(s_lat - m)
        den = jnp.sum(p_ctx, axis=-1, keepdims=True) + jnp.sum(p_lat, axis=-1, keepdims=True)
        outs.append((_dot(p_ctx.astype(BF16), vc) + _dot(p_lat.astype(BF16), v)) / den)
    o = outs[0] - lam * outs[1]
    o_ref[...] = (_rms_lanes(o, g_ref[...]) * (1.0 - lam_init)).astype(BF16)


def attn_b_lat(u_attn, cache_k, cache_v, b_lambda, b_subln, rope_c, rope_s, lam_init, layer,
               batch, seq, row0, heads, col0, tq=256):
    qcol = col0 // HD
    kcol = qcol + heads
    vcol = kcol + heads
    nq = seq // tq
    past = cache_k.shape[2]
    rq0 = row0 // tq
    rs0 = row0 // seq
    return pl.pallas_call(
        functools.partial(_attn_b_lat_body, lam_init=lam_init),
        grid=(batch, heads, nq),
        in_specs=[pl.BlockSpec(b_lambda.shape, lambda b, h, n: (0, 0)),
                  pl.BlockSpec((1, HD), lambda b, h, n: (0, 0)),
                  pl.BlockSpec((tq, HD), lambda b, h, n: (rq0 + b * nq + n, qcol + h)),
                  pl.BlockSpec((seq, HD), lambda b, h, n: (rs0 + b, kcol + h)),
                  pl.BlockSpec((seq, HD), lambda b, h, n: (rs0 + b, vcol + h)),
                  pl.BlockSpec((None, None, past, HD), lambda b, h, n: (b, layer, 0, h)),
                  pl.BlockSpec((None, None, past, HD), lambda b, h, n: (b, layer, 0, h)),
                  pl.BlockSpec((tq, HD), lambda b, h, n: (n, 0)),
                  pl.BlockSpec((tq, HD), lambda b, h, n: (n, 0)),
                  pl.BlockSpec((seq, HD), lambda b, h, n: (0, 0)),
                  pl.BlockSpec((seq, HD), lambda b, h, n: (0, 0))],
        out_specs=pl.BlockSpec((tq, HD), lambda b, h, n: (b * nq + n, h)),
        out_shape=jax.ShapeDtypeStruct((batch * seq, heads * HD), BF16),
        scratch_shapes=[pltpu.VMEM((HD, seq), BF16), pltpu.VMEM((HD, past), BF16), pltpu.VMEM((seq, HD), BF16)],
        compiler_params=_cp(("arbitrary", "arbitrary", "arbitrary"), 40),
        name="attn_b_lat",
    )(b_lambda, b_subln.reshape(1, HD), u_attn, u_attn, u_attn, cache_k, cache_v,
      rope_c, rope_s, rope_c, rope_s)


def dft_tables(l):
    f = jnp.arange(l, dtype=jnp.int32)[:, None]
    s = jnp.arange(l, dtype=jnp.int32)[None, :]
    m = ((2 * f + 1) * s) % (4 * l)
    ang = m.astype(F32) * (math.pi / (2 * l))
    fm = jnp.concatenate([jnp.cos(ang), -jnp.sin(ang)], axis=0)
    hi = fm.astype(BF16)
    lo = (fm - hi.astype(F32)).astype(BF16)
    return hi, lo, hi.T


def hyena_feats(l):
    t = jnp.linspace(0.0, 1.0, l, dtype=F32)[:, None]
    bands = (HY_EMB - 1) // 2
    w = 2.0 * math.pi * jnp.arange(l, dtype=F32)[:, None] / l
    fr = jnp.linspace(1e-4, bands - 1, bands, dtype=F32)[None, :]
    feats = jnp.concatenate([t, jnp.cos(fr * w), -jnp.sin(fr * w)], axis=-1)
    return jnp.pad(feats, ((0, 0), (0, HD - HY_EMB)))


def hyena_decay(l, width):
    t = jnp.linspace(0.0, 1.0, l, dtype=F32)[:, None]
    deltas = jnp.abs(jnp.linspace(math.log(HY_TARGET) / HY_SLOW, math.log(HY_TARGET) / HY_FAST, width, dtype=F32))
    return jnp.exp(-t * deltas)


def _hy_time_body(feat_ref, dec_ref, w1_ref, b1_ref, w2_ref, b2_ref, w3_ref, fr_ref, kh_ref, kl_ref, nrm_scr,
                  *, width, l):
    ps = pl.program_id(2)
    c = pl.program_id(3)
    fr = fr_ref[...]
    h = jnp.sin(fr[0:1] * (_dot3(feat_ref[...], w1_ref[...]) + b1_ref[...]))
    h = jnp.sin(fr[1:2] * (_dot3(h, w2_ref[...]) + b2_ref[...]))
    dec = dec_ref[...]
    row = lax.broadcasted_iota(jnp.int32, dec.shape, 0)
    fwd = _dot3(h, w3_ref[:, :width]) * dec
    bwd = jnp.where((row == 0) & (c == 0), 0.0, _dot3(h, w3_ref[:, width:]) * dec)

    @pl.when((ps == 0) & (c == 0))
    def _():
        nrm_scr[...] = jnp.zeros_like(nrm_scr)

    @pl.when(ps == 0)
    def _():
        nrm_scr[...] += (jnp.sum(jnp.abs(fwd), axis=0, keepdims=True)
                         + jnp.sum(jnp.abs(bwd), axis=0, keepdims=True))

    @pl.when(ps == 1)
    def _():
        inv = 1.0 / ((nrm_scr[...] + EPS) * l)
        for part, val in ((0, fwd * inv), (1, bwd * inv)):
            hi, lo = _split(val)
            kh_ref[:, part * width:(part + 1) * width] = hi
            kl_ref[:, part * width:(part + 1) * width] = lo


def _hy_spec_body(fh_ref, fl_ref, kh_ref, kl_ref, o_ref, *, width, n_re):
    r = pl.program_id(2)
    fh = fh_ref[...]
    kh = kh_ref[...]
    res = _dot(fh, kh) + _dot(fh, kl_ref[...]) + _dot(fl_ref[...], kh)
    sign = jnp.where(r < n_re, 1.0, -1.0)
    o_ref[...] = res[:, :width] + sign * res[:, width:]


def hyena_filters(l, hy_w1p, hy_b1, hy_w2, hy_b2, hy_w3, hy_freq, f_hi, f_lo, width, rc=512, tr=512):
    depth = hy_w3.shape[0]
    ffn_w = hy_w2.shape[1]
    rc = min(rc, l)
    tr = min(tr, l)
    n_re = l // tr
    feats = hyena_feats(l)
    dec = hyena_decay(l, width)
    wspec = lambda shape: pl.BlockSpec((None,) + shape, lambda d, o, p, c: (d, 0, 0))
    kspec = pl.BlockSpec((None, rc, 2 * width), lambda d, o, p, c: (d, c * p, o))
    k_hi, k_lo = pl.pallas_call(
        functools.partial(_hy_time_body, width=width, l=l),
        grid=(depth, 2, 2, l // rc),
        in_specs=[pl.BlockSpec((rc, HD), lambda d, o, p, c: (c, 0)),
                  pl.BlockSpec((rc, width), lambda d, o, p, c: (c, 0)),
                  wspec((HD, ffn_w)), wspec((1, ffn_w)), wspec((ffn_w, ffn_w)), wspec((1, ffn_w)),
                  pl.BlockSpec((None, ffn_w, 2 * width), lambda d, o, p, c: (d, 0, o)),
                  wspec((2, ffn_w))],
        out_specs=[kspec, kspec],
        out_shape=[jax.ShapeDtypeStruct((depth, l, 4 * width), BF16)] * 2,
        scratch_shapes=[pltpu.VMEM((1, width), F32)],
        compiler_params=_cp(("arbitrary",) * 4, 32),
        name="hyena_time_filters",
    )(feats, dec, hy_w1p, hy_b1.reshape(depth, 1, ffn_w), hy_w2, hy_b2.reshape(depth, 1, ffn_w), hy_w3, hy_freq)
    fspec = pl.BlockSpec((tr, l), lambda d, o, r: (r, 0))
    kfull = pl.BlockSpec((None, l, 2 * width), lambda d, o, r: (d, 0, o))
    return pl.pallas_call(
        functools.partial(_hy_spec_body, width=width, n_re=n_re),
        grid=(depth, 2, 2 * n_re),
        in_specs=[fspec, fspec, kfull, kfull],
        out_specs=pl.BlockSpec((None, None, tr, width), lambda d, o, r: (d, o, r, 0)),
        out_shape=jax.ShapeDtypeStruct((depth, 2, 2 * l, width), F32),
        compiler_params=_cp(("arbitrary",) * 3, 48),
        name="hyena_spectra",
    )(f_hi, f_lo, k_hi, k_lo)


def _short_conv(u, w, b):
    l = u.shape[0]
    row = lax.broadcasted_iota(jnp.int32, u.shape, 0)
    up = jnp.where(row == 0, 0.0, pltpu.roll(u, 1, 0))
    dn = jnp.where(row == l - 1, 0.0, pltpu.roll(u, l - 1, 0))
    return up * w[0:1] + u * w[1:2] + dn * w[2:3] + b


def _hyena_body(hv_ref, h1_ref, h2_ref, wv_ref, w1_ref, w2_ref, bv_ref, b1_ref, b2_ref, bias_ref,
                fre_ref, fim_ref, gre_ref, gim_ref, kre_ref, kim_ref, o_ref,
                z_scr, zb_scr, acc_scr, *, nf):
    ph = pl.program_id(2)
    j = pl.program_id(3)

    @pl.when((ph == 0) & (j == 0))
    def _():
        v = _short_conv(hv_ref[...].astype(F32), wv_ref[...], bv_ref[...])
        z_scr[...] = v
        zb_scr[...] = v.astype(BF16)

    @pl.when(j == 0)
    def _():
        acc_scr[...] = jnp.zeros_like(acc_scr)

    zb = zb_scr[...]
    zre = _dot(fre_ref[...], zb)
    zim = _dot(fim_ref[...], zb)
    kre, kim = kre_ref[...], kim_ref[...]
    yre = (zre * kre - zim * kim).astype(BF16)
    yim = (zre * kim + zim * kre).astype(BF16)
    acc_scr[...] += _dot(gre_ref[...], yre) + _dot(gim_ref[...], yim)

    @pl.when((j == nf - 1) & (ph == 0))
    def _():
        x1 = _short_conv(h1_ref[...].astype(F32), w1_ref[...], b1_ref[...])
        y = x1 * (acc_scr[...] + z_scr[...] * bias_ref[0:1])
        z_scr[...] = y
        zb_scr[...] = y.astype(BF16)

    @pl.when((j == nf - 1) & (ph == 1))
    def _():
        x2 = _short_conv(h2_ref[...].astype(F32), w2_ref[...], b2_ref[...])
        o_ref[...] = (x2 * (acc_scr[...] + z_scr[...] * bias_ref[1:2])).astype(BF16)


def hyena(u_hy, conv_w, conv_b, bias, kf, f_hi, g_hi, layer, batch, seq, row0, width, cw=512, tfq=256):
    nch = width // cw
    tfq = min(tfq, seq)
    nf = seq // tfq
    rb = row0 // seq
    hspec = lambda part: pl.BlockSpec((seq, cw), lambda b, c, p, j: (rb + b, part * nch + c))
    wspec = lambda part: pl.BlockSpec((3, cw), lambda b, c, p, j: (0, part * nch + c))
    bspec = lambda part: pl.BlockSpec((1, cw), lambda b, c, p, j: (0, part * nch + c))
    return pl.pallas_call(
        functools.partial(_hyena_body, nf=nf),
        grid=(batch, nch, 2, nf),
        in_specs=[hspec(0), hspec(1), hspec(2), wspec(0), wspec(1), wspec(2), bspec(0), bspec(1), bspec(2),
                  pl.BlockSpec((2, cw), lambda b, c, p, j: (0, c)),
                  pl.BlockSpec((tfq, seq), lambda b, c, p, j: (j, 0)),
                  pl.BlockSpec((tfq, seq), lambda b, c, p, j: (nf + j, 0)),
                  pl.BlockSpec((seq, tfq), lambda b, c, p, j: (0, j)),
                  pl.BlockSpec((seq, tfq), lambda b, c, p, j: (0, nf + j)),
                  pl.BlockSpec((None, None, tfq, cw), lambda b, c, p, j: (layer, p, j, c)),
                  pl.BlockSpec((None, None, tfq, cw), lambda b, c, p, j: (layer, p, nf + j, c))],
        out_specs=pl.BlockSpec((seq, cw), lambda b, c, p, j: (b, c)),
        out_shape=jax.ShapeDtypeStruct((batch * seq, width), BF16),
        scratch_shapes=[pltpu.VMEM((seq, cw), F32), pltpu.VMEM((seq, cw), BF16), pltpu.VMEM((seq, cw), F32)],
        compiler_params=_cp(("arbitrary",) * 4, 56),
        name="hyena",
    )(u_hy, u_hy, u_hy, conv_w, conv_w, conv_w, conv_b, conv_b, conv_b, bias,
      f_hi, f_hi, g_hi, g_hi, kf, kf)


def _hgrn_pair_level(t, s, fwd):
    x = t ^ s
    lvl = jnp.where(x == 0, 0, 32 - lax.clz(x))
    used = (s <= t) if fwd else (s >= t)
    return jnp.where(used, lvl, -1)


def _hgrn_midpoint(b3, blk, fwd):
    rows = b3.shape[1]
    r0 = blk // 2 - 1 if fwd else blk // 2
    ref = b3[:, r0:r0 + 1, :]
    if rows > blk:
        pos = lax.broadcasted_iota(jnp.int32, b3.shape, 1)
        for j in range(1, rows // blk):
            ref = jnp.where(pos >= j * blk, b3[:, j * blk + r0:j * blk + r0 + 1, :], ref)
    return ref


def _hgrn_chunk(q, k, g, v, st, tri, lvl, fwd):
    c = CHUNK
    gh, gl = _split(g)
    b = _dot(tri, gh) + _dot(tri, gl)
    tot = b[c - 1:c] if fwd else b[0:1]
    vb = v.astype(BF16)
    o = _dot_nt((q * jnp.exp2(b)).astype(BF16), st.astype(BF16))
    kd = (k * jnp.exp2(tot - b)).astype(BF16)
    st_new = st * jnp.exp2(tot) + _dot_tn(vb, kd)

    a = jnp.where(lvl == 0, jnp.sum(q * k, axis=-1, keepdims=True), 0.0)
    a = jnp.where(lvl == 1, _dot_nt((q * jnp.exp2(g)).astype(BF16), k.astype(BF16)), a)
    blk, m = 4, 2
    while blk <= c:
        rows = max(blk, SUB)
        b3, q3, k3 = (x.reshape(c // rows, rows, HD) for x in (b, q, k))
        e = jnp.exp2(-jnp.abs(b3 - _hgrn_midpoint(b3, blk, fwd)))
        qt = (q3 * e).reshape(c, HD).astype(BF16)
        kt = (k3 * e).reshape(c, HD).astype(BF16)
        a = jnp.where(lvl == m, _dot_nt(qt, kt), a)
        blk, m = blk * 2, m + 1
    return o + _dot(a.astype(BF16), vb), st_new


def _hgrn_gate(z, lb):
    e = jnp.exp(-jnp.abs(z))
    r = 1.0 / (1.0 + e)
    pos = z >= 0
    sig_pos = jnp.where(pos, r, e * r)
    sig_neg = jnp.where(pos, e * r, r)
    f = sig_pos + jnp.maximum(lb, LB_FLOOR) * sig_neg
    return jnp.log(f) * LOG2E, (1.0 - lb) * sig_neg


def _hgrn_lower_bound(lb_ref, layer, depth, direction):
    x = lb_ref[...]
    rows = [x[2 * i + direction:2 * i + direction + 1] for i in range(depth)]
    m = functools.reduce(jnp.maximum, rows)
    es = [jnp.exp(r - m) for r in rows]
    tot = functools.reduce(lambda a, b: a + b, es)
    cum = jnp.zeros_like(tot)
    for i in range(1, layer + 1):
        cum = cum + es[i]
    return jnp.maximum(cum / tot, 0.0)


def _hgrn_body(*refs, layer, depth, has_state, emit_state):
    it = iter(refs)
    lb_ref, ng_ref, q_ref, ff_ref, fb_ref, i_ref, g_ref = (next(it) for _ in range(7))
    s0_ref = next(it) if has_state else None
    o_ref = next(it)
    so_ref = next(it) if emit_state else None
    of_scr, ob_scr, st_scr, tri_scr, lvl_scr = (next(it) for _ in range(5))
    nc = q_ref.shape[0] // CHUNK
    hp = q_ref.shape[1] // HD
    ti = lax.broadcasted_iota(jnp.int32, (CHUNK, CHUNK), 0)
    si = lax.broadcasted_iota(jnp.int32, (CHUNK, CHUNK), 1)
    chains = [(h, d) for h in range(hp) for d in range(2)]
    lbs = {}
    for d in range(2):
        tri_scr[d] = jnp.where((si <= ti) if d == 0 else (si >= ti), 1.0, 0.0).astype(BF16)
        lvl_scr[d] = _hgrn_pair_level(ti, si, d == 0)
        lb_all = _hgrn_lower_bound(lb_ref, layer, depth, d)
        for h in range(hp):
            lbs[h, d] = lb_all[:, h * HD:(h + 1) * HD]
            st_scr[2 * h + d] = s0_ref[d, h].T if has_state else jnp.zeros((HD, HD), F32)

    def step(i, carry):
        for h, d in chains:
            fwd = d == 0
            ci = i if fwd else nc - 1 - i
            sl = pl.ds(pl.multiple_of(ci * CHUNK, CHUNK), CHUNK)
            cs = slice(h * HD, (h + 1) * HD)
            xq = q_ref[sl, cs].astype(F32)
            log_f, kk = _hgrn_gate((ff_ref if fwd else fb_ref)[sl, cs].astype(F32), lbs[h, d])
            o, st = _hgrn_chunk(xq * _sigmoid(xq), kk, log_f, i_ref[sl, cs].astype(F32), st_scr[2 * h + d],
                                tri_scr[d], lvl_scr[d], fwd)
            st_scr[2 * h + d] = st
            (of_scr if fwd else ob_scr)[sl, cs] = o
        return carry

    lax.fori_loop(0, nc, step, 0)
    for h, d in chains:
        if emit_state:
            so_ref[d, h] = st_scr[2 * h + d].T
    for h in range(hp):
        cs = slice(h * HD, (h + 1) * HD)
        xg = g_ref[:, cs].astype(F32)
        o_ref[:, cs] = (_rms_lanes(of_scr[:, cs] + ob_scr[:, cs], ng_ref[...]) * (xg * _sigmoid(xg))).astype(BF16)


def hgrn(u_hg, hg_lb2, hg_norm, state, layer, depth, batch, seq, row0, heads, emit_state, col0=0, hp=2):
    rb = row0 // seq
    ng = heads // hp
    w = hp * HD
    cb = col0 // w
    part = lambda p: pl.BlockSpec((seq, w), lambda b, h: (rb + b, cb + p * ng + h))
    in_specs = [pl.BlockSpec((2 * depth, w), lambda b, h: (0, h)),
                pl.BlockSpec((1, HD), lambda b, h: (0, 0)),
                part(0), part(1), part(2), part(3), part(4)]
    args = [hg_lb2, hg_norm.reshape(1, HD), u_hg, u_hg, u_hg, u_hg, u_hg]
    if state is not None:
        in_specs.append(pl.BlockSpec((None, None, 2, hp, HD, HD), lambda b, h: (b, layer, 0, h, 0, 0)))
        args.append(state)
    out_specs = [pl.BlockSpec((seq, w), lambda b, h: (b, h))]
    out_shape = [jax.ShapeDtypeStruct((batch * seq, heads * HD), BF16)]
    if emit_state:
        out_specs.append(pl.BlockSpec((None, 2, hp, HD, HD), lambda b, h: (b, 0, h, 0, 0)))
        out_shape.append(jax.ShapeDtypeStruct((batch, 2, heads, HD, HD), F32))
    return pl.pallas_call(
        functools.partial(_hgrn_body, layer=layer, depth=depth, has_state=state is not None,
                          emit_state=emit_state),
        grid=(batch, ng),
        in_specs=in_specs,
        out_specs=out_specs,
        out_shape=out_shape,
        scratch_shapes=[pltpu.VMEM((seq, w), F32), pltpu.VMEM((seq, w), F32),
                        pltpu.VMEM((2 * hp, HD, HD), F32),
                        pltpu.VMEM((2, CHUNK, CHUNK), BF16), pltpu.VMEM((2, CHUNK, CHUNK), jnp.int32)],
        compiler_params=_cp(("arbitrary", "arbitrary"), 48),
        name="hgrn",
    )(*args)


def _merge_body(*refs, n_ctx_tiles):
    o_refs, (gt_ref, x_ref, mod_ref, wb_ref, wo_ref, out_ref) = refs[:2 * N_BRANCH], refs[2 * N_BRANCH:]
    d = x_ref.shape[1]
    is_ctx = pl.program_id(0) < n_ctx_tiles
    acc = None
    for n in range(N_BRANCH):
        o = jnp.where(is_ctx, o_refs[2 * n][...], o_refs[2 * n + 1][...])
        y = gt_ref[:, n * d:(n + 1) * d].astype(F32) * _dot(o, wb_ref[n])
        acc = y if acc is None else acc + y
    out = _dot(acc.astype(BF16), wo_ref[...])
    out_ref[...] = x_ref[...] + mod_ref[...][5:6] * out


def merge(branch_outs, gates, x, mod_l, w_branch, w_out, t_ctx, l_lat, tm=256):
    t, d = x.shape
    mw = w_branch.shape[1]
    nct = t_ctx // tm
    row = functools.partial(_mod_row, tm=tm, t_ctx=t_ctx, l_lat=l_lat)
    cspec = pl.BlockSpec((tm, mw), lambda i: (jnp.minimum(i, nct - 1), 0))
    lspec = pl.BlockSpec((tm, mw), lambda i: (jnp.maximum(i - nct, 0), 0))
    return pl.pallas_call(
        functools.partial(_merge_body, n_ctx_tiles=nct),
        grid=(t // tm,),
        in_specs=[cspec, lspec] * N_BRANCH + [
                  pl.BlockSpec((tm, N_BRANCH * d), lambda i: (i, 0)),
                  pl.BlockSpec((tm, d), lambda i: (i, 0)),
                  pl.BlockSpec((None, N_MOD, d), lambda i: (row(i), 0, 0)),
                  pl.BlockSpec((N_BRANCH, mw, d), lambda i: (0, 0, 0), pipeline_mode=pl.Buffered(1)),
                  pl.BlockSpec((d, d), lambda i: (0, 0), pipeline_mode=pl.Buffered(1))],
        out_specs=pl.BlockSpec((tm, d), lambda i: (i, 0)),
        out_shape=jax.ShapeDtypeStruct((t, d), F32),
        compiler_params=_cp(("arbitrary",), 56),
        name="merge",
    )(*[o for pair in branch_outs for o in pair], gates, x, mod_l, w_branch, w_out)


def _final_norm_body(x_ref, g_ref, o_ref):
    o_ref[...] = _rms_lanes(x_ref[...], g_ref[...])


def final_norm(x, g, tm=512):
    t, d = x.shape
    return pl.pallas_call(
        _final_norm_body,
        grid=(t // tm,),
        in_specs=[pl.BlockSpec((tm, d), lambda i: (i, 0)), pl.BlockSpec((1, d), lambda i: (0, 0))],
        out_specs=pl.BlockSpec((tm, d), lambda i: (i, 0)),
        out_shape=jax.ShapeDtypeStruct((t, d), F32),
        compiler_params=_cp(("arbitrary",), 32),
        name="final_norm",
    )(x, g.reshape(1, d))


def kernel(x_prompt, x_sample, c, cache_a_k, cache_a_v, cache_b_k, cache_b_v, state_hgrn, c_ctx, w_mod, b_mod, norm_g, w_ffn1_gu, w_ffn1_down, w_ffn2_gu, w_ffn2_down, w_in, w_branch, w_out, a_sink, b_lambda, b_subln, hy_conv_w, hy_conv_b, hy_w1, hy_b1, hy_w2, hy_b2, hy_w3, hy_freq, hy_bias, hg_lb, hg_norm, final_g):
    batch, seq, d = x_prompt.shape
    dec_batch, dec_seq, _ = x_sample.shape
    depth = w_mod.shape[0]
    mix_w = w_branch.shape[2]
    a_heads = a_sink.shape[1]
    a_kvh = cache_a_k.shape[3]
    a_group = a_heads // a_kvh
    b_heads = cache_b_k.shape[3]
    hg_heads = state_hgrn.shape[3]
    t_ctx, t_lat = batch * seq, dec_batch * dec_seq

    n_attn = (a_heads + 2 * a_kvh + 3 * b_heads) * HD
    b_col0 = (a_heads + 2 * a_kvh) * HD

    x = jnp.concatenate([x_prompt.reshape(t_ctx, d), x_sample.reshape(t_lat, d)], axis=0)
    cond = jnp.concatenate([c_ctx[None, :], c], axis=0)
    cond = jnp.pad(cond, ((0, MOD_ROWS - cond.shape[0]), (0, 0)))
    mod = modulation(cond, w_mod, b_mod).reshape(depth, MOD_ROWS, N_MOD, d)

    rope_a = rope_tables(dec_seq, HD, 1)
    rope_b = rope_tables(dec_seq, HD // 2, 2)
    fc_hi, fc_lo, gc_hi = dft_tables(seq)
    fl_hi, fl_lo, gl_hi = dft_tables(dec_seq)
    w1p = jnp.pad(hy_w1, ((0, 0), (0, HD - hy_w1.shape[1]), (0, 0)))
    kf_ctx = hyena_filters(seq, w1p, hy_b1, hy_w2, hy_b2, hy_w3, hy_freq, fc_hi, fc_lo, mix_w)
    kf_lat = hyena_filters(dec_seq, w1p, hy_b1, hy_w2, hy_b2, hy_w3, hy_freq, fl_hi, fl_lo, mix_w)

    cak = cache_a_k.reshape(dec_batch, depth, -1, a_kvh * HD)
    cav = cache_a_v.reshape(dec_batch, depth, -1, a_kvh * HD)
    cbk = cache_b_k.reshape(dec_batch, depth, -1, b_heads * HD)
    cbv = cache_b_v.reshape(dec_batch, depth, -1, b_heads * HD)
    hg_lb2 = hg_lb.reshape(depth * 2, mix_w)

    ak_l, av_l, bk_l, bv_l, st_l = [], [], [], [], []
    for l in range(depth):
        lam_init = 0.8 - 0.6 * math.exp(-0.3 * l)
        mod_l = mod[l]
        x, h_mix = ffn(x, mod_l, norm_g[l, 0:1], w_ffn1_gu[l].astype(BF16), w_ffn1_down[l].astype(BF16), 0,
                       t_ctx, dec_seq, next_g_row=norm_g[l, 1:2])
        n_rec = n_attn + 8 * mix_w
        u_attn = proj(h_mix, w_in[l, :, :n_attn].astype(BF16), F32, False)
        u_rec = proj(h_mix, w_in[l, :, n_attn:n_rec].astype(BF16), BF16, False)
        gates = proj(h_mix, w_in[l, :, n_rec:].astype(BF16), BF16, True)

        uc = u_attn[:t_ctx]
        k0 = a_heads * HD
        ak_l.append(uc[:, k0:k0 + a_kvh * HD].reshape(batch, seq, a_kvh, HD))
        av_l.append(uc[:, k0 + a_kvh * HD:b_col0].reshape(batch, seq, a_kvh, HD))
        k1 = b_col0 + b_heads * HD
        bk_l.append(uc[:, k1:k1 + b_heads * HD].reshape(batch, seq, b_heads, 2, HD // 2))
        bv_l.append(uc[:, k1 + b_heads * HD:].reshape(batch, seq, b_heads, HD))

        oa_c = attn_a_ctx(u_attn, a_sink[l], batch, seq, a_kvh, a_group)
        oa_s = attn_a_lat(u_attn, cak, cav, a_sink[l], rope_a[0], rope_a[1], l, dec_batch, dec_seq, t_ctx,
                          a_kvh, a_group)
        ob_c = attn_b_ctx(u_attn, b_lambda[l], b_subln[l], lam_init, batch, seq, b_heads, b_col0)
        ob_s = attn_b_lat(u_attn, cbk, cbv, b_lambda[l], b_subln[l], rope_b[0], rope_b[1], lam_init, l,
                          dec_batch, dec_seq, t_ctx, b_heads, b_col0)
        cb = hy_conv_b[l].reshape(1, -1)
        oc_c = hyena(u_rec, hy_conv_w[l], cb, hy_bias[l], kf_ctx, fc_hi, gc_hi, l, batch, seq, 0, mix_w)
        oc_s = hyena(u_rec, hy_conv_w[l], cb, hy_bias[l], kf_lat, fl_hi, gl_hi, l, dec_batch, dec_seq, t_ctx, mix_w)
        od_c, st = hgrn(u_rec, hg_lb2, hg_norm[l], None, l, depth, batch, seq, 0, hg_heads, True, col0=3 * mix_w)
        od_s, = hgrn(u_rec, hg_lb2, hg_norm[l], state_hgrn, l, depth, dec_batch, dec_seq, t_ctx, hg_heads, False,
                     col0=3 * mix_w)
        st_l.append(st)

        x = merge(((oa_c, oa_s), (ob_c, ob_s), (oc_c, oc_s), (od_c, od_s)), gates, x, mod_l,
                  w_branch[l].astype(BF16), w_out[l].astype(BF16), t_ctx, dec_seq)
        x = ffn(x, mod_l, norm_g[l, 2:3], w_ffn2_gu[l].astype(BF16), w_ffn2_down[l].astype(BF16), 2,
                t_ctx, dec_seq)

    y = final_norm(x, final_g)
    return (y[:t_ctx].reshape(batch, seq, d), y[t_ctx:].reshape(dec_batch, dec_seq, d),
            jnp.stack(ak_l, axis=1), jnp.stack(av_l, axis=1), jnp.stack(bk_l, axis=1), jnp.stack(bv_l, axis=1),
            jnp.stack(st_l, axis=1))
```

```python
import functools
import math

import jax
import jax.numpy as jnp
import numpy as np
from jax import lax
from jax.experimental import pallas as pl
from jax.experimental.pallas import tpu as pltpu

F32 = jnp.float32
BF16 = jnp.bfloat16

EPS = 1e-6
NEG = -1e30
LB_FLOOR = 1e-30
ROPE_BASE = 10000.0
GRID_W = 64
N_MOD = 9
N_BRANCH = 4
HD = 128
A_WINDOW = 128
HY_EMB = 33
HY_TARGET, HY_FAST, HY_SLOW = 1e-2, 0.3, 1.5
CHUNK = 128
SUB = 8
LOG2E = 1.4426950408889634
MOD_ROWS = 16
VMEM_MB = 2 ** 20


def _cp(sem, vmem_mb):
    return pltpu.CompilerParams(dimension_semantics=sem, vmem_limit_bytes=vmem_mb * VMEM_MB)


def _dot(a, b):
    return jnp.dot(a, b, preferred_element_type=F32)


def _dot_nt(a, b):
    return lax.dot_general(a, b, (((1,), (1,)), ((), ())), preferred_element_type=F32)


def _dot_tn(a, b):
    return lax.dot_general(a, b, (((0,), (0,)), ((), ())), preferred_element_type=F32)


def _split(a):
    hi = a.astype(BF16)
    lo = (a - hi.astype(F32)).astype(BF16)
    return hi, lo


def _dot3(a, b):
    ah, al = _split(a)
    bh, bl = _split(b)
    return _dot(ah, bh) + _dot(ah, bl) + _dot(al, bh)


def _sigmoid(x):
    return jax.nn.sigmoid(x)


def _norm_mod(x, g, sc, sh):
    y = x * lax.rsqrt(jnp.mean(x * x, axis=-1, keepdims=True) + EPS)
    return (y * g) * (1.0 + sc) + sh


def _rms_lanes(x, g):
    return x * lax.rsqrt(jnp.mean(x * x, axis=-1, keepdims=True) + EPS) * g


def _mod_body(c_ref, w_ref, b_ref, o_ref):
    c = c_ref[...]
    a = (c * _sigmoid(c)).astype(BF16)
    o_ref[...] = _dot(a, w_ref[...].astype(BF16)) + b_ref[...]


def modulation(cond, w_mod, b_mod, tn=1024):
    depth, d, n = w_mod.shape
    return pl.pallas_call(
        _mod_body,
        grid=(depth, n // tn),
        in_specs=[pl.BlockSpec((MOD_ROWS, d), lambda l, j: (0, 0)),
                  pl.BlockSpec((None, d, tn), lambda l, j: (l, 0, j)),
                  pl.BlockSpec((None, 1, tn), lambda l, j: (l, 0, j))],
        out_specs=pl.BlockSpec((None, MOD_ROWS, tn), lambda l, j: (l, 0, j)),
        out_shape=jax.ShapeDtypeStruct((depth, MOD_ROWS, n), F32),
        compiler_params=_cp(("arbitrary", "arbitrary"), 40),
        name="modulation",
    )(cond, w_mod, b_mod.reshape(depth, 1, n))


def _mod_row(i, tm, t_ctx, l_lat):
    start = i * tm
    return jnp.where(start < t_ctx, 0, 1 + (start - t_ctx) // l_lat)


def _ffn_body(*refs, sub, nj, next_sub):
    if next_sub is None:
        x_ref, mod_ref, g_ref, wg_ref, wu_ref, wd_ref, o_ref, h_scr = refs
    else:
        x_ref, mod_ref, g_ref, gn_ref, wg_ref, wu_ref, wd_ref, o_ref, hn_ref, h_scr = refs
    j = pl.program_id(1)

    @pl.when(j == 0)
    def _():
        m = mod_ref[...]
        h = _norm_mod(x_ref[...], g_ref[...], m[3 * sub + 1:3 * sub + 2], m[3 * sub:3 * sub + 1])
        h_scr[...] = h.astype(BF16)
        o_ref[...] = jnp.zeros_like(o_ref)

    h = h_scr[...]
    tf = wg_ref.shape[1]
    hc = tf // 2
    part = None
    for c0 in range(0, tf, hc):
        a = _dot(h, wg_ref[:, c0:c0 + hc])
        b = _dot(h, wu_ref[:, c0:c0 + hc])
        act = (a * _sigmoid(a) * b).astype(BF16)
        y = _dot(act, wd_ref[c0:c0 + hc, :])
        part = y if part is None else part + y
    o_ref[...] += part

    @pl.when(j == nj - 1)
    def _():
        m = mod_ref[...]
        y = x_ref[...] + 0.5 * m[3 * sub + 2:3 * sub + 3] * o_ref[...]
        o_ref[...] = y
        if next_sub is not None:
            hn_ref[...] = _norm_mod(y, gn_ref[...], m[3 * next_sub + 1:3 * next_sub + 2],
                                    m[3 * next_sub:3 * next_sub + 1]).astype(BF16)


def ffn(x, mod_l, g_row, w_gu, w_dn, sub, t_ctx, l_lat, next_g_row=None, tm=512, tf=512):
    t, d = x.shape
    dff = w_dn.shape[0]
    nj = dff // tf
    emit = next_g_row is not None
    row = functools.partial(_mod_row, tm=tm, t_ctx=t_ctx, l_lat=l_lat)
    gspec = pl.BlockSpec((1, d), lambda i, j: (0, 0))
    xspec = pl.BlockSpec((tm, d), lambda i, j: (i, 0))
    outs = pl.pallas_call(
        functools.partial(_ffn_body, sub=sub, nj=nj, next_sub=sub + 1 if emit else None),
        grid=(t // tm, nj),
        in_specs=[xspec,
                  pl.BlockSpec((None, N_MOD, d), lambda i, j: (row(i), 0, 0)),
                  gspec] + ([gspec] if emit else []) + [
                  pl.BlockSpec((d, tf), lambda i, j: (0, j)),
                  pl.BlockSpec((d, tf), lambda i, j: (0, j + nj)),
                  pl.BlockSpec((tf, d), lambda i, j: (j, 0))],
        out_specs=[xspec] + ([xspec] if emit else []),
        out_shape=[jax.ShapeDtypeStruct((t, d), F32)] + ([jax.ShapeDtypeStruct((t, d), BF16)] if emit else []),
        scratch_shapes=[pltpu.VMEM((tm, d), BF16)],
        compiler_params=_cp(("arbitrary", "arbitrary"), 56),
        name="ffn",
    )(x, mod_l, g_row, *([next_g_row] if emit else []), w_gu, w_gu, w_dn)
    return outs if emit else outs[0]


def _proj_body(h_ref, w_ref, o_ref, *, gate):
    r = _dot(h_ref[...], w_ref[...])
    o_ref[...] = (_sigmoid(r) if gate else r).astype(o_ref.dtype)


def proj(h, w, out_dtype, gate, tm=2048, tn=512):
    t, d = h.shape
    n = w.shape[1]
    while t % tm:
        tm //= 2
    return pl.pallas_call(
        functools.partial(_proj_body, gate=gate),
        grid=(t // tm, n // tn),
        in_specs=[pl.BlockSpec((tm, d), lambda i, j: (i, 0)),
                  pl.BlockSpec((d, tn), lambda i, j: (0, j))],
        out_specs=pl.BlockSpec((tm, tn), lambda i, j: (i, j)),
        out_shape=jax.ShapeDtypeStruct((t, n), out_dtype),
        compiler_params=_cp(("arbitrary", "arbitrary"), 56),
        name="proj",
    )(h, w)


def rope_tables(l, rot_dim, reps):
    n_rows = l // GRID_W
    rows = jnp.broadcast_to(jnp.arange(n_rows, dtype=F32)[:, None], (n_rows, GRID_W)).reshape(-1)
    cols = jnp.broadcast_to(jnp.arange(GRID_W, dtype=F32)[None, :], (n_rows, GRID_W)).reshape(-1)
    axis_dim = rot_dim // 2
    inv = ROPE_BASE ** (-jnp.arange(0, axis_dim, 2, dtype=F32) / axis_dim)
    ang = jnp.concatenate([rows[:, None] * inv, cols[:, None] * inv], axis=-1)
    cos = jnp.repeat(jnp.cos(ang), 2, axis=-1)
    sin = jnp.repeat(jnp.sin(ang), 2, axis=-1)
    sign = jnp.tile(jnp.array([-1.0, 1.0], F32), rot_dim // 2)
    return jnp.tile(cos, (1, reps)), jnp.tile(sin * sign, (1, reps))


def _rope(x, c, s):
    lane = lax.broadcasted_iota(jnp.int32, x.shape, 1)
    nxt = pltpu.roll(x, x.shape[1] - 1, 1)
    prv = pltpu.roll(x, 1, 1)
    return x * c + jnp.where((lane & 1) == 0, nxt, prv) * s


def _attn_a_ctx_body(sink_ref, q_ref, k_ref, v_ref, o_ref, *, group):
    kh = pl.program_id(1)
    scale = HD ** -0.5
    k = k_ref[...].astype(BF16)
    v = v_ref[...].astype(BF16)
    for g in range(group):
        q = q_ref[:, g * HD:(g + 1) * HD].astype(BF16)
        s = _dot_nt(q, k) * scale
        sink = sink_ref[kh * group + g]
        m = jnp.maximum(jnp.max(s, axis=-1, keepdims=True), sink)
        p = jnp.exp(s - m)
        den = jnp.sum(p, axis=-1, keepdims=True) + jnp.exp(sink - m)
        o_ref[:, g * HD:(g + 1) * HD] = (_dot(p.astype(BF16), v) / den).astype(BF16)


def attn_a_ctx(u_attn, sink, batch, seq, kvh, group):
    qw = group * HD
    kcol = kvh * group
    vcol = kcol + kvh
    return pl.pallas_call(
        functools.partial(_attn_a_ctx_body, group=group),
        grid=(batch, kvh),
        in_specs=[pl.BlockSpec(memory_space=pltpu.SMEM),
                  pl.BlockSpec((seq, qw), lambda b, h: (b, h)),
                  pl.BlockSpec((seq, HD), lambda b, h: (b, kcol + h)),
                  pl.BlockSpec((seq, HD), lambda b, h: (b, vcol + h))],
        out_specs=pl.BlockSpec((seq, qw), lambda b, h: (b, h)),
        out_shape=jax.ShapeDtypeStruct((batch * seq, kvh * qw), BF16),
        compiler_params=_cp(("arbitrary", "arbitrary"), 32),
        name="attn_a_ctx",
    )(sink, u_attn, u_attn, u_attn)


def _attn_a_lat_body(sink_ref, q_ref, k0_ref, k1_ref, k2_ref, v0_ref, v1_ref, v2_ref, kc_ref, vc_ref,
                     cq_ref, sq_ref, c0_ref, s0_ref, c2_ref, s2_ref, o_ref, *, group, kvh, seq):
    n = pl.program_id(1)
    blk = q_ref.shape[0]
    scale = HD ** -0.5
    cq, sq = cq_ref[...], sq_ref[...]
    qi = lax.broadcasted_iota(jnp.int32, (blk, 3 * blk), 0)
    kj = lax.broadcasted_iota(jnp.int32, (blk, 3 * blk), 1)
    kpos = (n - 1) * blk + kj
    qpos = n * blk + qi
    valid = (kpos >= 0) & (kpos < seq) & (jnp.abs(qpos - kpos) <= A_WINDOW)
    for kh in range(kvh):
        ks = slice(kh * HD, (kh + 1) * HD)
        kw = jnp.concatenate([_rope(k0_ref[:, ks], c0_ref[...], s0_ref[...]),
                              _rope(k1_ref[:, ks], cq, sq),
                              _rope(k2_ref[:, ks], c2_ref[...], s2_ref[...])], axis=0).astype(BF16)
        vw = jnp.concatenate([v0_ref[:, ks], v1_ref[:, ks], v2_ref[:, ks]], axis=0).astype(BF16)
        kc = kc_ref[:, ks].astype(BF16)
        vc = vc_ref[:, ks].astype(BF16)
        for g in range(group):
            hs = slice((kh * group + g) * HD, (kh * group + g + 1) * HD)
            q = _rope(q_ref[:, hs], cq, sq).astype(BF16)
            s_loc = jnp.where(valid, _dot_nt(q, kw) * scale, NEG)
            s_ctx = _dot_nt(q, kc) * scale
            sink = sink_ref[kh * group + g]
            m = jnp.maximum(jnp.maximum(jnp.max(s_loc, axis=-1, keepdims=True),
                                        jnp.max(s_ctx, axis=-1, keepdims=True)), sink)
            p_loc = jnp.exp(s_loc - m)
            p_ctx = jnp.exp(s_ctx - m)
            den = (jnp.sum(p_loc, axis=-1, keepdims=True) + jnp.sum(p_ctx, axis=-1, keepdims=True)
                   + jnp.exp(sink - m))
            o = _dot(p_ctx.astype(BF16), vc) + _dot(p_loc.astype(BF16), vw)
            o_ref[:, hs] = (o / den).astype(BF16)


def attn_a_lat(u_attn, cache_k, cache_v, sink, rope_c, rope_s, layer, batch, seq, row0, kvh, group, blk=256):
    nb = seq // blk
    rb0 = row0 // blk
    qw = kvh * group * HD
    kw = kvh * HD
    kcol = qw // kw
    vcol = kcol + 1
    past = cache_k.shape[2]

    def rows(b, n):
        return rb0 + b * nb + n

    prev = lambda n: jnp.maximum(n - 1, 0)
    nxt = lambda n: jnp.minimum(n + 1, nb - 1)
    tab = lambda f: pl.BlockSpec((blk, HD), lambda b, n: (f(n), 0))
    kv = lambda f, col: pl.BlockSpec((blk, kw), lambda b, n: (rows(b, f(n)), col))
    same = lambda n: n
    return pl.pallas_call(
        functools.partial(_attn_a_lat_body, group=group, kvh=kvh, seq=seq),
        grid=(batch, nb),
        in_specs=[pl.BlockSpec(memory_space=pltpu.SMEM),
                  pl.BlockSpec((blk, qw), lambda b, n: (rows(b, n), 0)),
                  kv(prev, kcol), kv(same, kcol), kv(nxt, kcol),
                  kv(prev, vcol), kv(same, vcol), kv(nxt, vcol),
                  pl.BlockSpec((None, None, past, kw), lambda b, n: (b, layer, 0, 0)),
                  pl.BlockSpec((None, None, past, kw), lambda b, n: (b, layer, 0, 0)),
                  tab(same), tab(same), tab(prev), tab(prev), tab(nxt), tab(nxt)],
        out_specs=pl.BlockSpec((blk, qw), lambda b, n: (b * nb + n, 0)),
        out_shape=jax.ShapeDtypeStruct((batch * seq, qw), BF16),
        compiler_params=_cp(("arbitrary", "arbitrary"), 40),
        name="attn_a_lat",
    )(sink, u_attn, u_attn, u_attn, u_attn, u_attn, u_attn, u_attn, cache_k, cache_v,
      rope_c, rope_s, rope_c, rope_s, rope_c, rope_s)


def _lambda(lam_ref, lam_init):
    lw = lam_ref[...]
    return (jnp.exp(jnp.sum(lw[0:1] * lw[1:2], axis=-1, keepdims=True))
            - jnp.exp(jnp.sum(lw[2:3] * lw[3:4], axis=-1, keepdims=True)) + lam_init)


def _softmax_parts(parts):
    m = functools.reduce(jnp.maximum, [jnp.max(s, axis=-1, keepdims=True) for s in parts])
    ps = [jnp.exp(s - m) for s in parts]
    inv = 1.0 / functools.reduce(lambda a, b: a + b, [jnp.sum(p, axis=-1, keepdims=True) for p in ps])
    return [p * inv for p in ps]


def _attn_b_ctx_body(lam_ref, g_ref, q_ref, k_ref, v_ref, o_ref, *, lam_init):
    hd = HD // 2
    scale = hd ** -0.5
    lam = _lambda(lam_ref, lam_init)
    q = q_ref[...].astype(BF16)
    k = k_ref[...].astype(BF16)
    p0, = _softmax_parts([_dot_nt(q[:, :hd], k[:, :hd]) * scale])
    p1, = _softmax_parts([_dot_nt(q[:, hd:], k[:, hd:]) * scale])
    o = _dot((p0 - lam * p1).astype(BF16), v_ref[...].astype(BF16))
    o_ref[...] = (_rms_lanes(o, g_ref[...]) * (1.0 - lam_init)).astype(BF16)


def attn_b_ctx(u_attn, b_lambda, b_subln, lam_init, batch, seq, heads, col0):
    qcol = col0 // HD
    kcol = qcol + heads
    vcol = kcol + heads
    return pl.pallas_call(
        functools.partial(_attn_b_ctx_body, lam_init=lam_init),
        grid=(batch, heads),
        in_specs=[pl.BlockSpec(b_lambda.shape, lambda b, h: (0, 0)),
                  pl.BlockSpec((1, HD), lambda b, h: (0, 0)),
                  pl.BlockSpec((seq, HD), lambda b, h: (b, qcol + h)),
                  pl.BlockSpec((seq, HD), lambda b, h: (b, kcol + h)),
                  pl.BlockSpec((seq, HD), lambda b, h: (b, vcol + h))],
        out_specs=pl.BlockSpec((seq, HD), lambda b, h: (b, h)),
        out_shape=jax.ShapeDtypeStruct((batch * seq, heads * HD), BF16),
        compiler_params=_cp(("arbitrary", "arbitrary"), 32),
        name="attn_b_ctx",
    )(b_lambda, b_subln.reshape(1, HD), u_attn, u_attn, u_attn)


def _attn_b_lat_body(lam_ref, g_ref, q_ref, k_ref, v_ref, kc_ref, vc_ref, cq_ref, sq_ref, ck_ref, sk_ref,
                     o_ref, k_scr, kc_scr, v_scr, *, lam_init):
    hd = HD // 2
    scale = hd ** -0.5
    nq = pl.program_id(2)

    @pl.when(nq == 0)
    def _():
        k_scr[...] = _rope(k_ref[...], ck_ref[...], sk_ref[...]).T.astype(BF16)
        kc_scr[...] = kc_ref[...].T.astype(BF16)
        v_scr[...] = v_ref[...].astype(BF16)

    lam = _lambda(lam_ref, lam_init)
    k = k_scr[...]
    v = v_scr[...]
    kc = kc_scr[...]
    vc = vc_ref[...].astype(BF16)
    q = _rope(q_ref[...], cq_ref[...], sq_ref[...]) * (scale * LOG2E)
    lane = lax.broadcasted_iota(jnp.int32, q.shape, 1)
    outs = []
    for c in range(2):
        qc = jnp.where((lane >= hd) == (c == 1), q, 0.0).astype(BF16)
        s_ctx = _dot(qc, kc)
        s_lat = _dot(qc, k)
        m = jnp.maximum(jnp.max(s_ctx, axis=-1, keepdims=True), jnp.max(s_lat, axis=-1, keepdims=True))
        p_ctx = jnp.exp2(s_ctx - m)
        p_lat = jnp.exp2(s_lat - m)
        den = jnp.sum(p_ctx, axis=-1, keepdims=True) + jnp.sum(p_lat, axis=-1, keepdims=True)
        outs.append((_dot(p_ctx.astype(BF16), vc) + _dot(p_lat.astype(BF16), v)) / den)
    o = outs[0] - lam * outs[1]
    o_ref[...] = (_rms_lanes(o, g_ref[...]) * (1.0 - lam_init)).astype(BF16)


def attn_b_lat(u_attn, cache_k, cache_v, b_lambda, b_subln, rope_c, rope_s, lam_init, layer,
               batch, seq, row0, heads, col0, tq=256):
    qcol = col0 // HD
    kcol = qcol + heads
    vcol = kcol + heads
    nq = seq // tq
    past = cache_k.shape[2]
    rq0 = row0 // tq
    rs0 = row0 // seq
    return pl.pallas_call(
        functools.partial(_attn_b_lat_body, lam_init=lam_init),
        grid=(batch, heads, nq),
        in_specs=[pl.BlockSpec(b_lambda.shape, lambda b, h, n: (0, 0)),
                  pl.BlockSpec((1, HD), lambda b, h, n: (0, 0)),
                  pl.BlockSpec((tq, HD), lambda b, h, n: (rq0 + b * nq + n, qcol + h)),
                  pl.BlockSpec((seq, HD), lambda b, h, n: (rs0 + b, kcol + h)),
                  pl.BlockSpec((seq, HD), lambda b, h, n: (rs0 + b, vcol + h)),
                  pl.BlockSpec((None, None, past, HD), lambda b, h, n: (b, layer, 0, h)),
                  pl.BlockSpec((None, None, past, HD), lambda b, h, n: (b, layer, 0, h)),
                  pl.BlockSpec((tq, HD), lambda b, h, n: (n, 0)),
                  pl.BlockSpec((tq, HD), lambda b, h, n: (n, 0)),
                  pl.BlockSpec((seq, HD), lambda b, h, n: (0, 0)),
                  pl.BlockSpec((seq, HD), lambda b, h, n: (0, 0))],
        out_specs=pl.BlockSpec((tq, HD), lambda b, h, n: (b * nq + n, h)),
        out_shape=jax.ShapeDtypeStruct((batch * seq, heads * HD), BF16),
        scratch_shapes=[pltpu.VMEM((HD, seq), BF16), pltpu.VMEM((HD, past), BF16), pltpu.VMEM((seq, HD), BF16)],
        compiler_params=_cp(("arbitrary", "arbitrary", "arbitrary"), 40),
        name="attn_b_lat",
    )(b_lambda, b_subln.reshape(1, HD), u_attn, u_attn, u_attn, cache_k, cache_v,
      rope_c, rope_s, rope_c, rope_s)


def dft_tables(l):
    f = jnp.arange(l, dtype=jnp.int32)[:, None]
    s = jnp.arange(l, dtype=jnp.int32)[None, :]
    m = ((2 * f + 1) * s) % (4 * l)
    ang = m.astype(F32) * (math.pi / (2 * l))
    fm = jnp.concatenate([jnp.cos(ang), -jnp.sin(ang)], axis=0)
    hi = fm.astype(BF16)
    lo = (fm - hi.astype(F32)).astype(BF16)
    return hi, lo, hi.T


def hyena_feats(l):
    t = jnp.linspace(0.0, 1.0, l, dtype=F32)[:, None]
    bands = (HY_EMB - 1) // 2
    w = 2.0 * math.pi * jnp.arange(l, dtype=F32)[:, None] / l
    fr = jnp.linspace(1e-4, bands - 1, bands, dtype=F32)[None, :]
    feats = jnp.concatenate([t, jnp.cos(fr * w), -jnp.sin(fr * w)], axis=-1)
    return jnp.pad(feats, ((0, 0), (0, HD - HY_EMB)))


def hyena_decay(l, width):
    t = jnp.linspace(0.0, 1.0, l, dtype=F32)[:, None]
    deltas = jnp.abs(jnp.linspace(math.log(HY_TARGET) / HY_SLOW, math.log(HY_TARGET) / HY_FAST, width, dtype=F32))
    return jnp.exp(-t * deltas)


def _hy_time_body(feat_ref, dec_ref, w1_ref, b1_ref, w2_ref, b2_ref, w3_ref, fr_ref, kh_ref, kl_ref, nrm_scr,
                  *, width, l):
    ps = pl.program_id(2)
    c = pl.program_id(3)
    fr = fr_ref[...]
    h = jnp.sin(fr[0:1] * (_dot3(feat_ref[...], w1_ref[...]) + b1_ref[...]))
    h = jnp.sin(fr[1:2] * (_dot3(h, w2_ref[...]) + b2_ref[...]))
    dec = dec_ref[...]
    row = lax.broadcasted_iota(jnp.int32, dec.shape, 0)
    fwd = _dot3(h, w3_ref[:, :width]) * dec
    bwd = jnp.where((row == 0) & (c == 0), 0.0, _dot3(h, w3_ref[:, width:]) * dec)

    @pl.when((ps == 0) & (c == 0))
    def _():
        nrm_scr[...] = jnp.zeros_like(nrm_scr)

    @pl.when(ps == 0)
    def _():
        nrm_scr[...] += (jnp.sum(jnp.abs(fwd), axis=0, keepdims=True)
                         + jnp.sum(jnp.abs(bwd), axis=0, keepdims=True))

    @pl.when(ps == 1)
    def _():
        inv = 1.0 / ((nrm_scr[...] + EPS) * l)
        for part, val in ((0, fwd * inv), (1, bwd * inv)):
            hi, lo = _split(val)
            kh_ref[:, part * width:(part + 1) * width] = hi
            kl_ref[:, part * width:(part + 1) * width] = lo


def _hy_spec_body(fh_ref, fl_ref, kh_ref, kl_ref, o_ref, *, width, n_re):
    r = pl.program_id(2)
    fh = fh_ref[...]
    kh = kh_ref[...]
    res = _dot(fh, kh) + _dot(fh, kl_ref[...]) + _dot(fl_ref[...], kh)
    sign = jnp.where(r < n_re, 1.0, -1.0)
    o_ref[...] = res[:, :width] + sign * res[:, width:]


def hyena_filters(l, hy_w1p, hy_b1, hy_w2, hy_b2, hy_w3, hy_freq, f_hi, f_lo, width, rc=512, tr=512):
    depth = hy_w3.shape[0]
    ffn_w = hy_w2.shape[1]
    rc = min(rc, l)
    tr = min(tr, l)
    n_re = l // tr
    feats = hyena_feats(l)
    dec = hyena_decay(l, width)
    wspec = lambda shape: pl.BlockSpec((None,) + shape, lambda d, o, p, c: (d, 0, 0))
    kspec = pl.BlockSpec((None, rc, 2 * width), lambda d, o, p, c: (d, c * p, o))
    k_hi, k_lo = pl.pallas_call(
        functools.partial(_hy_time_body, width=width, l=l),
        grid=(depth, 2, 2, l // rc),
        in_specs=[pl.BlockSpec((rc, HD), lambda d, o, p, c: (c, 0)),
                  pl.BlockSpec((rc, width), lambda d, o, p, c: (c, 0)),
                  wspec((HD, ffn_w)), wspec((1, ffn_w)), wspec((ffn_w, ffn_w)), wspec((1, ffn_w)),
                  pl.BlockSpec((None, ffn_w, 2 * width), lambda d, o, p, c: (d, 0, o)),
                  wspec((2, ffn_w))],
        out_specs=[kspec, kspec],
        out_shape=[jax.ShapeDtypeStruct((depth, l, 4 * width), BF16)] * 2,
        scratch_shapes=[pltpu.VMEM((1, width), F32)],
        compiler_params=_cp(("arbitrary",) * 4, 32),
        name="hyena_time_filters",
    )(feats, dec, hy_w1p, hy_b1.reshape(depth, 1, ffn_w), hy_w2, hy_b2.reshape(depth, 1, ffn_w), hy_w3, hy_freq)
    fspec = pl.BlockSpec((tr, l), lambda d, o, r: (r, 0))
    kfull = pl.BlockSpec((None, l, 2 * width), lambda d, o, r: (d, 0, o))
    return pl.pallas_call(
        functools.partial(_hy_spec_body, width=width, n_re=n_re),
        grid=(depth, 2, 2 * n_re),
        in_specs=[fspec, fspec, kfull, kfull],
        out_specs=pl.BlockSpec((None, None, tr, width), lambda d, o, r: (d, o, r, 0)),
        out_shape=jax.ShapeDtypeStruct((depth, 2, 2 * l, width), F32),
        compiler_params=_cp(("arbitrary",) * 3, 48),
        name="hyena_spectra",
    )(f_hi, f_lo, k_hi, k_lo)


def _short_conv(u, w, b):
    l = u.shape[0]
    row = lax.broadcasted_iota(jnp.int32, u.shape, 0)
    up = jnp.where(row == 0, 0.0, pltpu.roll(u, 1, 0))
    dn = jnp.where(row == l - 1, 0.0, pltpu.roll(u, l - 1, 0))
    return up * w[0:1] + u * w[1:2] + dn * w[2:3] + b


def _hyena_body(hv_ref, h1_ref, h2_ref, wv_ref, w1_ref, w2_ref, bv_ref, b1_ref, b2_ref, bias_ref,
                fre_ref, fim_ref, gre_ref, gim_ref, kre_ref, kim_ref, o_ref,
                z_scr, zb_scr, acc_scr, *, nf):
    ph = pl.program_id(2)
    j = pl.program_id(3)

    @pl.when((ph == 0) & (j == 0))
    def _():
        v = _short_conv(hv_ref[...].astype(F32), wv_ref[...], bv_ref[...])
        z_scr[...] = v
        zb_scr[...] = v.astype(BF16)

    @pl.when(j == 0)
    def _():
        acc_scr[...] = jnp.zeros_like(acc_scr)

    zb = zb_scr[...]
    zre = _dot(fre_ref[...], zb)
    zim = _dot(fim_ref[...], zb)
    kre, kim = kre_ref[...], kim_ref[...]
    yre = (zre * kre - zim * kim).astype(BF16)
    yim = (zre * kim + zim * kre).astype(BF16)
    acc_scr[...] += _dot(gre_ref[...], yre) + _dot(gim_ref[...], yim)

    @pl.when((j == nf - 1) & (ph == 0))
    def _():
        x1 = _short_conv(h1_ref[...].astype(F32), w1_ref[...], b1_ref[...])
        y = x1 * (acc_scr[...] + z_scr[...] * bias_ref[0:1])
        z_scr[...] = y
        zb_scr[...] = y.astype(BF16)

    @pl.when((j == nf - 1) & (ph == 1))
    def _():
        x2 = _short_conv(h2_ref[...].astype(F32), w2_ref[...], b2_ref[...])
        o_ref[...] = (x2 * (acc_scr[...] + z_scr[...] * bias_ref[1:2])).astype(BF16)


def hyena(u_hy, conv_w, conv_b, bias, kf, f_hi, g_hi, layer, batch, seq, row0, width, cw=512, tfq=256):
    nch = width // cw
    tfq = min(tfq, seq)
    nf = seq // tfq
    rb = row0 // seq
    hspec = lambda part: pl.BlockSpec((seq, cw), lambda b, c, p, j: (rb + b, part * nch + c))
    wspec = lambda part: pl.BlockSpec((3, cw), lambda b, c, p, j: (0, part * nch + c))
    bspec = lambda part: pl.BlockSpec((1, cw), lambda b, c, p, j: (0, part * nch + c))
    return pl.pallas_call(
        functools.partial(_hyena_body, nf=nf),
        grid=(batch, nch, 2, nf),
        in_specs=[hspec(0), hspec(1), hspec(2), wspec(0), wspec(1), wspec(2), bspec(0), bspec(1), bspec(2),
                  pl.BlockSpec((2, cw), lambda b, c, p, j: (0, c)),
                  pl.BlockSpec((tfq, seq), lambda b, c, p, j: (j, 0)),
                  pl.BlockSpec((tfq, seq), lambda b, c, p, j: (nf + j, 0)),
                  pl.BlockSpec((seq, tfq), lambda b, c, p, j: (0, j)),
                  pl.BlockSpec((seq, tfq), lambda b, c, p, j: (0, nf + j)),
                  pl.BlockSpec((None, None, tfq, cw), lambda b, c, p, j: (layer, p, j, c)),
                  pl.BlockSpec((None, None, tfq, cw), lambda b, c, p, j: (layer, p, nf + j, c))],
        out_specs=pl.BlockSpec((seq, cw), lambda b, c, p, j: (b, c)),
        out_shape=jax.ShapeDtypeStruct((batch * seq, width), BF16),
        scratch_shapes=[pltpu.VMEM((seq, cw), F32), pltpu.VMEM((seq, cw), BF16), pltpu.VMEM((seq, cw), F32)],
        compiler_params=_cp(("arbitrary",) * 4, 56),
        name="hyena",
    )(u_hy, u_hy, u_hy, conv_w, conv_w, conv_w, conv_b, conv_b, conv_b, bias,
      f_hi, f_hi, g_hi, g_hi, kf, kf)


def _hgrn_pair_level(t, s, fwd):
    x = t ^ s
    lvl = jnp.where(x == 0, 0, 32 - lax.clz(x))
    used = (s <= t) if fwd else (s >= t)
    return jnp.where(used, lvl, -1)


def _hgrn_midpoint(b3, blk, fwd):
    rows = b3.shape[1]
    r0 = blk // 2 - 1 if fwd else blk // 2
    ref = b3[:, r0:r0 + 1, :]
    if rows > blk:
        pos = lax.broadcasted_iota(jnp.int32, b3.shape, 1)
        for j in range(1, rows // blk):
            ref = jnp.where(pos >= j * blk, b3[:, j * blk + r0:j * blk + r0 + 1, :], ref)
    return ref


def _hgrn_chunk(q, k, g, v, st, tri, lvl, fwd):
    c = CHUNK
    gh, gl = _split(g)
    b = _dot(tri, gh) + _dot(tri, gl)
    tot = b[c - 1:c] if fwd else b[0:1]
    vb = v.astype(BF16)
    o = _dot_nt((q * jnp.exp2(b)).astype(BF16), st.astype(BF16))
    kd = (k * jnp.exp2(tot - b)).astype(BF16)
    st_new = st * jnp.exp2(tot) + _dot_tn(vb, kd)

    a = jnp.where(lvl == 0, jnp.sum(q * k, axis=-1, keepdims=True), 0.0)
    a = jnp.where(lvl == 1, _dot_nt((q * jnp.exp2(g)).astype(BF16), k.astype(BF16)), a)
    blk, m = 4, 2
    while blk <= c:
        rows = max(blk, SUB)
        b3, q3, k3 = (x.reshape(c // rows, rows, HD) for x in (b, q, k))
        e = jnp.exp2(-jnp.abs(b3 - _hgrn_midpoint(b3, blk, fwd)))
        qt = (q3 * e).reshape(c, HD).astype(BF16)
        kt = (k3 * e).reshape(c, HD).astype(BF16)
        a = jnp.where(lvl == m, _dot_nt(qt, kt), a)
        blk, m = blk * 2, m + 1
    return o + _dot(a.astype(BF16), vb), st_new


def _hgrn_gate(z, lb):
    e = jnp.exp(-jnp.abs(z))
    r = 1.0 / (1.0 + e)
    pos = z >= 0
    sig_pos = jnp.where(pos, r, e * r)
    sig_neg = jnp.where(pos, e * r, r)
    f = sig_pos + jnp.maximum(lb, LB_FLOOR) * sig_neg
    return jnp.log(f) * LOG2E, (1.0 - lb) * sig_neg


def _hgrn_lower_bound(lb_ref, layer, depth, direction):
    x = lb_ref[...]
    rows = [x[2 * i + direction:2 * i + direction + 1] for i in range(depth)]
    m = functools.reduce(jnp.maximum, rows)
    es = [jnp.exp(r - m) for r in rows]
    tot = functools.reduce(lambda a, b: a + b, es)
    cum = jnp.zeros_like(tot)
    for i in range(1, layer + 1):
        cum = cum + es[i]
    return jnp.maximum(cum / tot, 0.0)


def _hgrn_body(*refs, layer, depth, has_state, emit_state):
    it = iter(refs)
    lb_ref, ng_ref, q_ref, ff_ref, fb_ref, i_ref, g_ref = (next(it) for _ in range(7))
    s0_ref = next(it) if has_state else None
    o_ref = next(it)
    so_ref = next(it) if emit_state else None
    of_scr, ob_scr, st_scr, tri_scr, lvl_scr = (next(it) for _ in range(5))
    nc = q_ref.shape[0] // CHUNK
    hp = q_ref.shape[1] // HD
    ti = lax.broadcasted_iota(jnp.int32, (CHUNK, CHUNK), 0)
    si = lax.broadcasted_iota(jnp.int32, (CHUNK, CHUNK), 1)
    chains = [(h, d) for h in range(hp) for d in range(2)]
    lbs = {}
    for d in range(2):
        tri_scr[d] = jnp.where((si <= ti) if d == 0 else (si >= ti), 1.0, 0.0).astype(BF16)
        lvl_scr[d] = _hgrn_pair_level(ti, si, d == 0)
        lb_all = _hgrn_lower_bound(lb_ref, layer, depth, d)
        for h in range(hp):
            lbs[h, d] = lb_all[:, h * HD:(h + 1) * HD]
            st_scr[2 * h + d] = s0_ref[d, h].T if has_state else jnp.zeros((HD, HD), F32)

    def step(i, carry):
        for h, d in chains:
            fwd = d == 0
            ci = i if fwd else nc - 1 - i
            sl = pl.ds(pl.multiple_of(ci * CHUNK, CHUNK), CHUNK)
            cs = slice(h * HD, (h + 1) * HD)
            xq = q_ref[sl, cs].astype(F32)
            log_f, kk = _hgrn_gate((ff_ref if fwd else fb_ref)[sl, cs].astype(F32), lbs[h, d])
            o, st = _hgrn_chunk(xq * _sigmoid(xq), kk, log_f, i_ref[sl, cs].astype(F32), st_scr[2 * h + d],
                                tri_scr[d], lvl_scr[d], fwd)
            st_scr[2 * h + d] = st
            (of_scr if fwd else ob_scr)[sl, cs] = o
        return carry

    lax.fori_loop(0, nc, step, 0)
    for h, d in chains:
        if emit_state:
            so_ref[d, h] = st_scr[2 * h + d].T
    for h in range(hp):
        cs = slice(h * HD, (h + 1) * HD)
        xg = g_ref[:, cs].astype(F32)
        o_ref[:, cs] = (_rms_lanes(of_scr[:, cs] + ob_scr[:, cs], ng_ref[...]) * (xg * _sigmoid(xg))).astype(BF16)


def hgrn(u_hg, hg_lb2, hg_norm, state, layer, depth, batch, seq, row0, heads, emit_state, col0=0, hp=4):
    rb = row0 // seq
    ng = heads // hp
    w = hp * HD
    cb = col0 // w
    part = lambda p: pl.BlockSpec((seq, w), lambda b, h: (rb + b, cb + p * ng + h))
    in_specs = [pl.BlockSpec((2 * depth, w), lambda b, h: (0, h)),
                pl.BlockSpec((1, HD), lambda b, h: (0, 0)),
                part(0), part(1), part(2), part(3), part(4)]
    args = [hg_lb2, hg_norm.reshape(1, HD), u_hg, u_hg, u_hg, u_hg, u_hg]
    if state is not None:
        in_specs.append(pl.BlockSpec((None, None, 2, hp, HD, HD), lambda b, h: (b, layer, 0, h, 0, 0)))
        args.append(state)
    out_specs = [pl.BlockSpec((seq, w), lambda b, h: (b, h))]
    out_shape = [jax.ShapeDtypeStruct((batch * seq, heads * HD), BF16)]
    if emit_state:
        out_specs.append(pl.BlockSpec((None, 2, hp, HD, HD), lambda b, h: (b, 0, h, 0, 0)))
        out_shape.append(jax.ShapeDtypeStruct((batch, 2, heads, HD, HD), F32))
    return pl.pallas_call(
        functools.partial(_hgrn_body, layer=layer, depth=depth, has_state=state is not None,
                          emit_state=emit_state),
        grid=(batch, ng),
        in_specs=in_specs,
        out_specs=out_specs,
        out_shape=out_shape,
        scratch_shapes=[pltpu.VMEM((seq, w), F32), pltpu.VMEM((seq, w), F32),
                        pltpu.VMEM((2 * hp, HD, HD), F32),
                        pltpu.VMEM((2, CHUNK, CHUNK), BF16), pltpu.VMEM((2, CHUNK, CHUNK), jnp.int32)],
        compiler_params=_cp(("arbitrary", "arbitrary"), 48),
        name="hgrn",
    )(*args)


def _merge_body(*refs, n_ctx_tiles):
    o_refs, (gt_ref, x_ref, mod_ref, wb_ref, wo_ref, out_ref) = refs[:2 * N_BRANCH], refs[2 * N_BRANCH:]
    d = x_ref.shape[1]
    is_ctx = pl.program_id(0) < n_ctx_tiles
    acc = None
    for n in range(N_BRANCH):
        o = jnp.where(is_ctx, o_refs[2 * n][...], o_refs[2 * n + 1][...])
        y = gt_ref[:, n * d:(n + 1) * d].astype(F32) * _dot(o, wb_ref[n])
        acc = y if acc is None else acc + y
    out = _dot(acc.astype(BF16), wo_ref[...])
    out_ref[...] = x_ref[...] + mod_ref[...][5:6] * out


def merge(branch_outs, gates, x, mod_l, w_branch, w_out, t_ctx, l_lat, tm=256):
    t, d = x.shape
    mw = w_branch.shape[1]
    nct = t_ctx // tm
    row = functools.partial(_mod_row, tm=tm, t_ctx=t_ctx, l_lat=l_lat)
    cspec = pl.BlockSpec((tm, mw), lambda i: (jnp.minimum(i, nct - 1), 0))
    lspec = pl.BlockSpec((tm, mw), lambda i: (jnp.maximum(i - nct, 0), 0))
    return pl.pallas_call(
        functools.partial(_merge_body, n_ctx_tiles=nct),
        grid=(t // tm,),
        in_specs=[cspec, lspec] * N_BRANCH + [
                  pl.BlockSpec((tm, N_BRANCH * d), lambda i: (i, 0)),
                  pl.BlockSpec((tm, d), lambda i: (i, 0)),
                  pl.BlockSpec((None, N_MOD, d), lambda i: (row(i), 0, 0)),
                  pl.BlockSpec((N_BRANCH, mw, d), lambda i: (0, 0, 0), pipeline_mode=pl.Buffered(1)),
                  pl.BlockSpec((d, d), lambda i: (0, 0), pipeline_mode=pl.Buffered(1))],
        out_specs=pl.BlockSpec((tm, d), lambda i: (i, 0)),
        out_shape=jax.ShapeDtypeStruct((t, d), F32),
        compiler_params=_cp(("arbitrary",), 56),
        name="merge",
    )(*[o for pair in branch_outs for o in pair], gates, x, mod_l, w_branch, w_out)


def _final_norm_body(x_ref, g_ref, o_ref):
    o_ref[...] = _rms_lanes(x_ref[...], g_ref[...])


def final_norm(x, g, tm=512):
    t, d = x.shape
    return pl.pallas_call(
        _final_norm_body,
        grid=(t // tm,),
        in_specs=[pl.BlockSpec((tm, d), lambda i: (i, 0)), pl.BlockSpec((1, d), lambda i: (0, 0))],
        out_specs=pl.BlockSpec((tm, d), lambda i: (i, 0)),
        out_shape=jax.ShapeDtypeStruct((t, d), F32),
        compiler_params=_cp(("arbitrary",), 32),
        name="final_norm",
    )(x, g.reshape(1, d))


def kernel(x_prompt, x_sample, c, cache_a_k, cache_a_v, cache_b_k, cache_b_v, state_hgrn, c_ctx, w_mod, b_mod, norm_g, w_ffn1_gu, w_ffn1_down, w_ffn2_gu, w_ffn2_down, w_in, w_branch, w_out, a_sink, b_lambda, b_subln, hy_conv_w, hy_conv_b, hy_w1, hy_b1, hy_w2, hy_b2, hy_w3, hy_freq, hy_bias, hg_lb, hg_norm, final_g):
    batch, seq, d = x_prompt.shape
    dec_batch, dec_seq, _ = x_sample.shape
    depth = w_mod.shape[0]
    mix_w = w_branch.shape[2]
    a_heads = a_sink.shape[1]
    a_kvh = cache_a_k.shape[3]
    a_group = a_heads // a_kvh
    b_heads = cache_b_k.shape[3]
    hg_heads = state_hgrn.shape[3]
    t_ctx, t_lat = batch * seq, dec_batch * dec_seq

    n_attn = (a_heads + 2 * a_kvh + 3 * b_heads) * HD
    b_col0 = (a_heads + 2 * a_kvh) * HD

    x = jnp.concatenate([x_prompt.reshape(t_ctx, d), x_sample.reshape(t_lat, d)], axis=0)
    cond = jnp.concatenate([c_ctx[None, :], c], axis=0)
    cond = jnp.pad(cond, ((0, MOD_ROWS - cond.shape[0]), (0, 0)))
    mod = modulation(cond, w_mod, b_mod).reshape(depth, MOD_ROWS, N_MOD, d)

    rope_a = rope_tables(dec_seq, HD, 1)
    rope_b = rope_tables(dec_seq, HD // 2, 2)
    fc_hi, fc_lo, gc_hi = dft_tables(seq)
    fl_hi, fl_lo, gl_hi = dft_tables(dec_seq)
    w1p = jnp.pad(hy_w1, ((0, 0), (0, HD - hy_w1.shape[1]), (0, 0)))
    kf_ctx = hyena_filters(seq, w1p, hy_b1, hy_w2, hy_b2, hy_w3, hy_freq, fc_hi, fc_lo, mix_w)
    kf_lat = hyena_filters(dec_seq, w1p, hy_b1, hy_w2, hy_b2, hy_w3, hy_freq, fl_hi, fl_lo, mix_w)

    cak = cache_a_k.reshape(dec_batch, depth, -1, a_kvh * HD)
    cav = cache_a_v.reshape(dec_batch, depth, -1, a_kvh * HD)
    cbk = cache_b_k.reshape(dec_batch, depth, -1, b_heads * HD)
    cbv = cache_b_v.reshape(dec_batch, depth, -1, b_heads * HD)
    hg_lb2 = hg_lb.reshape(depth * 2, mix_w)

    ak_l, av_l, bk_l, bv_l, st_l = [], [], [], [], []
    for l in range(depth):
        lam_init = 0.8 - 0.6 * math.exp(-0.3 * l)
        mod_l = mod[l]
        x, h_mix = ffn(x, mod_l, norm_g[l, 0:1], w_ffn1_gu[l].astype(BF16), w_ffn1_down[l].astype(BF16), 0,
                       t_ctx, dec_seq, next_g_row=norm_g[l, 1:2])
        n_rec = n_attn + 8 * mix_w
        u_attn = proj(h_mix, w_in[l, :, :n_attn].astype(BF16), F32, False)
        u_rec = proj(h_mix, w_in[l, :, n_attn:n_rec].astype(BF16), BF16, False)
        gates = proj(h_mix, w_in[l, :, n_rec:].astype(BF16), BF16, True)

        uc = u_attn[:t_ctx]
        k0 = a_heads * HD
        ak_l.append(uc[:, k0:k0 + a_kvh * HD].reshape(batch, seq, a_kvh, HD))
        av_l.append(uc[:, k0 + a_kvh * HD:b_col0].reshape(batch, seq, a_kvh, HD))
        k1 = b_col0 + b_heads * HD
        bk_l.append(uc[:, k1:k1 + b_heads * HD].reshape(batch, seq, b_heads, 2, HD // 2))
        bv_l.append(uc[:, k1 + b_heads * HD:].reshape(batch, seq, b_heads, HD))

        oa_c = attn_a_ctx(u_attn, a_sink[l], batch, seq, a_kvh, a_group)
        oa_s = attn_a_lat(u_attn, cak, cav, a_sink[l], rope_a[0], rope_a[1], l, dec_batch, dec_seq, t_ctx,
                          a_kvh, a_group)
        ob_c = attn_b_ctx(u_attn, b_lambda[l], b_subln[l], lam_init, batch, seq, b_heads, b_col0)
        ob_s = attn_b_lat(u_attn, cbk, cbv, b_lambda[l], b_subln[l], rope_b[0], rope_b[1], lam_init, l,
                          dec_batch, dec_seq, t_ctx, b_heads, b_col0)
        cb = hy_conv_b[l].reshape(1, -1)
        oc_c = hyena(u_rec, hy_conv_w[l], cb, hy_bias[l], kf_ctx, fc_hi, gc_hi, l, batch, seq, 0, mix_w)
        oc_s = hyena(u_rec, hy_conv_w[l], cb, hy_bias[l], kf_lat, fl_hi, gl_hi, l, dec_batch, dec_seq, t_ctx, mix_w)
        od_c, st = hgrn(u_rec, hg_lb2, hg_norm[l], None, l, depth, batch, seq, 0, hg_heads, True, col0=3 * mix_w)
        od_s, = hgrn(u_rec, hg_lb2, hg_norm[l], state_hgrn, l, depth, dec_batch, dec_seq, t_ctx, hg_heads, False,
                     col0=3 * mix_w)
        st_l.append(st)

        x = merge(((oa_c, oa_s), (ob_c, ob_s), (oc_c, oc_s), (od_c, od_s)), gates, x, mod_l,
                  w_branch[l].astype(BF16), w_out[l].astype(BF16), t_ctx, dec_seq)
        x = ffn(x, mod_l, norm_g[l, 2:3], w_ffn2_gu[l].astype(BF16), w_ffn2_down[l].astype(BF16), 2,
                t_ctx, dec_seq)

    y = final_norm(x, final_g)
    return (y[:t_ctx].reshape(batch, seq, d), y[t_ctx:].reshape(dec_batch, dec_seq, d),
            jnp.stack(ak_l, axis=1), jnp.stack(av_l, axis=1), jnp.stack(bk_l, axis=1), jnp.stack(bv_l, axis=1),
            jnp.stack(st_l, axis=1))
```

```python
import functools
import math

import jax
import jax.numpy as jnp
import numpy as np
from jax import lax
from jax.experimental import pallas as pl
from jax.experimental.pallas import tpu as pltpu

F32 = jnp.float32
BF16 = jnp.bfloat16

EPS = 1e-6
NEG = -1e30
LB_FLOOR = 1e-30
ROPE_BASE = 10000.0
GRID_W = 64
N_MOD = 9
N_BRANCH = 4
HD = 128
A_WINDOW = 128
HY_EMB = 33
HY_TARGET, HY_FAST, HY_SLOW = 1e-2, 0.3, 1.5
CHUNK = 128
SUB = 8
LOG2E = 1.4426950408889634
MOD_ROWS = 16
VMEM_MB = 2 ** 20


def _cp(sem, vmem_mb):
    return pltpu.CompilerParams(dimension_semantics=sem, vmem_limit_bytes=vmem_mb * VMEM_MB)


def _dot(a, b):
    return jnp.dot(a, b, preferred_element_type=F32)


def _dot_nt(a, b):
    return lax.dot_general(a, b, (((1,), (1,)), ((), ())), preferred_element_type=F32)


def _dot_tn(a, b):
    return lax.dot_general(a, b, (((0,), (0,)), ((), ())), preferred_element_type=F32)


def _split(a):
    hi = a.astype(BF16)
    lo = (a - hi.astype(F32)).astype(BF16)
    return hi, lo


def _dot3(a, b):
    ah, al = _split(a)
    bh, bl = _split(b)
    return _dot(ah, bh) + _dot(ah, bl) + _dot(al, bh)


def _sigmoid(x):
    return 0.5 * jnp.tanh(0.5 * x) + 0.5


def _norm_mod(x, g, sc, sh):
    y = x * lax.rsqrt(jnp.mean(x * x, axis=-1, keepdims=True) + EPS)
    return (y * g) * (1.0 + sc) + sh


def _norm_mod_cols(load, keep, reload, store, shape, g, sc, sh):
    n, d = shape
    tiles = [slice(t, t + HD) for t in range(0, d, HD)]
    ssq = jnp.zeros((n, HD), F32)
    for cols in tiles:
        x = load(cols)
        if keep is not None:
            keep(cols, x)
        ssq = ssq + x * x
    r = lax.rsqrt(jnp.sum(ssq, axis=-1, keepdims=True) * (1.0 / d) + EPS)
    gs = g * (1.0 + sc)
    for cols in tiles:
        store(cols, (reload(cols) * r) * gs[:, cols] + sh[:, cols])


def _rms_lanes(x, g):
    return x * lax.rsqrt(jnp.mean(x * x, axis=-1, keepdims=True) + EPS) * g


def _mod_body(c_ref, w_ref, b_ref, o_ref):
    c = c_ref[...]
    a = (c * _sigmoid(c)).astype(BF16)
    o_ref[...] = _dot(a, w_ref[...].astype(BF16)) + b_ref[...]


def modulation(cond, w_mod, b_mod, tn=1024):
    depth, d, n = w_mod.shape
    return pl.pallas_call(
        _mod_body,
        grid=(depth, n // tn),
        in_specs=[pl.BlockSpec((MOD_ROWS, d), lambda l, j: (0, 0)),
                  pl.BlockSpec((None, d, tn), lambda l, j: (l, 0, j)),
                  pl.BlockSpec((None, 1, tn), lambda l, j: (l, 0, j))],
        out_specs=pl.BlockSpec((None, MOD_ROWS, tn), lambda l, j: (l, 0, j)),
        out_shape=jax.ShapeDtypeStruct((depth, MOD_ROWS, n), F32),
        compiler_params=_cp(("arbitrary", "arbitrary"), 40),
        name="modulation",
    )(cond, w_mod, b_mod.reshape(depth, 1, n))


def _mod_row(i, tm, t_ctx, l_lat):
    start = i * tm
    return jnp.where(start < t_ctx, 0, 1 + (start - t_ctx) // l_lat)


def _ffn_body(*refs, sub, nj, next_sub):
    if next_sub is None:
        x_ref, mod_ref, g_ref, wg_ref, wu_ref, wd_ref, o_ref, h_scr = refs
    else:
        x_ref, mod_ref, g_ref, gn_ref, wg_ref, wu_ref, wd_ref, o_ref, hn_ref, h_scr = refs
    j = pl.program_id(1)

    @pl.when(j == 0)
    def _():
        m = mod_ref[...]
        load = lambda cols: x_ref[:, cols]

        def store(cols, h):
            h_scr[:, cols] = h.astype(BF16)

        _norm_mod_cols(load, None, load, store, x_ref.shape, g_ref[...],
                       m[3 * sub + 1:3 * sub + 2], m[3 * sub:3 * sub + 1])
        o_ref[...] = jnp.zeros_like(o_ref)

    h = h_scr[...]
    tf = wg_ref.shape[1]
    hc = tf // 2
    part = None
    for c0 in range(0, tf, hc):
        a = _dot(h, wg_ref[:, c0:c0 + hc])
        b = _dot(h, wu_ref[:, c0:c0 + hc])
        act = (a * _sigmoid(a) * b).astype(BF16)
        y = _dot(act, wd_ref[c0:c0 + hc, :])
        part = y if part is None else part + y
    o_ref[...] += part

    @pl.when(j == nj - 1)
    def _():
        m = mod_ref[...]
        ga = 0.5 * m[3 * sub + 2:3 * sub + 3]
        if next_sub is None:
            o_ref[...] = x_ref[...] + ga * o_ref[...]
        else:
            def keep(cols, y):
                o_ref[:, cols] = y

            def store(cols, h):
                hn_ref[:, cols] = h.astype(BF16)

            _norm_mod_cols(lambda cols: x_ref[:, cols] + ga[:, cols] * o_ref[:, cols], keep,
                           lambda cols: o_ref[:, cols], store, x_ref.shape, gn_ref[...],
                           m[3 * next_sub + 1:3 * next_sub + 2], m[3 * next_sub:3 * next_sub + 1])


def ffn(x, mod_l, g_row, w_gu, w_dn, sub, t_ctx, l_lat, next_g_row=None, tm=512, tf=512):
    t, d = x.shape
    dff = w_dn.shape[0]
    nj = dff // tf
    emit = next_g_row is not None
    row = functools.partial(_mod_row, tm=tm, t_ctx=t_ctx, l_lat=l_lat)
    gspec = pl.BlockSpec((1, d), lambda i, j: (0, 0))
    xspec = pl.BlockSpec((tm, d), lambda i, j: (i, 0))
    outs = pl.pallas_call(
        functools.partial(_ffn_body, sub=sub, nj=nj, next_sub=sub + 1 if emit else None),
        grid=(t // tm, nj),
        in_specs=[xspec,
                  pl.BlockSpec((None, N_MOD, d), lambda i, j: (row(i), 0, 0)),
                  gspec] + ([gspec] if emit else []) + [
                  pl.BlockSpec((d, tf), lambda i, j: (0, j)),
                  pl.BlockSpec((d, tf), lambda i, j: (0, j + nj)),
                  pl.BlockSpec((tf, d), lambda i, j: (j, 0))],
        out_specs=[xspec] + ([xspec] if emit else []),
        out_shape=[jax.ShapeDtypeStruct((t, d), F32)] + ([jax.ShapeDtypeStruct((t, d), BF16)] if emit else []),
        scratch_shapes=[pltpu.VMEM((tm, d), BF16)],
        compiler_params=_cp(("arbitrary", "arbitrary"), 56),
        name="ffn",
    )(x, mod_l, g_row, *([next_g_row] if emit else []), w_gu, w_gu, w_dn)
    return outs if emit else outs[0]


def _proj_body(h_ref, w_ref, o_ref, *, gate):
    r = _dot(h_ref[...], w_ref[...])
    o_ref[...] = (_sigmoid(r) if gate else r).astype(o_ref.dtype)


def proj(h, w, out_dtype, gate, tm=2048, tn=512):
    t, d = h.shape
    n = w.shape[1]
    while t % tm:
        tm //= 2
    return pl.pallas_call(
        functools.partial(_proj_body, gate=gate),
        grid=(t // tm, n // tn),
        in_specs=[pl.BlockSpec((tm, d), lambda i, j: (i, 0)),
                  pl.BlockSpec((d, tn), lambda i, j: (0, j))],
        out_specs=pl.BlockSpec((tm, tn), lambda i, j: (i, j)),
        out_shape=jax.ShapeDtypeStruct((t, n), out_dtype),
        compiler_params=_cp(("arbitrary", "arbitrary"), 56),
        name="proj",
    )(h, w)


def rope_tables(l, rot_dim, reps):
    n_rows = l // GRID_W
    rows = jnp.broadcast_to(jnp.arange(n_rows, dtype=F32)[:, None], (n_rows, GRID_W)).reshape(-1)
    cols = jnp.broadcast_to(jnp.arange(GRID_W, dtype=F32)[None, :], (n_rows, GRID_W)).reshape(-1)
    axis_dim = rot_dim // 2
    inv = ROPE_BASE ** (-jnp.arange(0, axis_dim, 2, dtype=F32) / axis_dim)
    ang = jnp.concatenate([rows[:, None] * inv, cols[:, None] * inv], axis=-1)
    cos = jnp.repeat(jnp.cos(ang), 2, axis=-1)
    sin = jnp.repeat(jnp.sin(ang), 2, axis=-1)
    sign = jnp.tile(jnp.array([-1.0, 1.0], F32), rot_dim // 2)
    return jnp.tile(cos, (1, reps)), jnp.tile(sin * sign, (1, reps))


def _rope(x, c, s):
    lane = lax.broadcasted_iota(jnp.int32, x.shape, 1)
    nxt = pltpu.roll(x, x.shape[1] - 1, 1)
    prv = pltpu.roll(x, 1, 1)
    return x * c + jnp.where((lane & 1) == 0, nxt, prv) * s


def _attn_a_ctx_body(sink_ref, q_ref, k_ref, v_ref, o_ref, *, group, kvh):
    scale = HD ** -0.5
    for kh in range(kvh):
        k = k_ref[:, kh * HD:(kh + 1) * HD].astype(BF16)
        v = v_ref[:, kh * HD:(kh + 1) * HD].astype(BF16)
        for g in range(group):
            hs = slice((kh * group + g) * HD, (kh * group + g + 1) * HD)
            q = q_ref[:, hs].astype(BF16)
            s = _dot_nt(q, k) * scale
            sink = sink_ref[kh * group + g]
            m = jnp.maximum(jnp.max(s, axis=-1, keepdims=True), sink)
            p = jnp.exp(s - m)
            den = jnp.sum(p, axis=-1, keepdims=True) + jnp.exp(sink - m)
            o_ref[:, hs] = (_dot(p.astype(BF16), v) / den).astype(BF16)


def attn_a_ctx(u_attn, sink, batch, seq, kvh, group):
    qw = kvh * group * HD
    kw = kvh * HD
    kcol = qw // kw
    return pl.pallas_call(
        functools.partial(_attn_a_ctx_body, group=group, kvh=kvh),
        grid=(batch,),
        in_specs=[pl.BlockSpec(memory_space=pltpu.SMEM),
                  pl.BlockSpec((seq, qw), lambda b: (b, 0)),
                  pl.BlockSpec((seq, kw), lambda b: (b, kcol)),
                  pl.BlockSpec((seq, kw), lambda b: (b, kcol + 1))],
        out_specs=pl.BlockSpec((seq, qw), lambda b: (b, 0)),
        out_shape=jax.ShapeDtypeStruct((batch * seq, qw), BF16),
        compiler_params=_cp(("arbitrary",), 32),
        name="attn_a_ctx",
    )(sink, u_attn, u_attn, u_attn)


def _attn_a_lat_body(sink_ref, q_ref, k0_ref, k1_ref, k2_ref, v0_ref, v1_ref, v2_ref, kc_ref, vc_ref,
                     cq_ref, sq_ref, c0_ref, s0_ref, c2_ref, s2_ref, o_ref, *, group, kvh, seq):
    n = pl.program_id(1)
    blk = q_ref.shape[0]
    scale = HD ** -0.5
    cq, sq = cq_ref[...], sq_ref[...]
    qi = lax.broadcasted_iota(jnp.int32, (blk, 3 * blk), 0)
    kj = lax.broadcasted_iota(jnp.int32, (blk, 3 * blk), 1)
    kpos = (n - 1) * blk + kj
    qpos = n * blk + qi
    valid = (kpos >= 0) & (kpos < seq) & (jnp.abs(qpos - kpos) <= A_WINDOW)
    for kh in range(kvh):
        ks = slice(kh * HD, (kh + 1) * HD)
        kw = jnp.concatenate([_rope(k0_ref[:, ks], c0_ref[...], s0_ref[...]),
                              _rope(k1_ref[:, ks], cq, sq),
                              _rope(k2_ref[:, ks], c2_ref[...], s2_ref[...])], axis=0).astype(BF16)
        vw = jnp.concatenate([v0_ref[:, ks], v1_ref[:, ks], v2_ref[:, ks]], axis=0).astype(BF16)
        kc = kc_ref[:, ks].astype(BF16)
        vc = vc_ref[:, ks].astype(BF16)
        for g in range(group):
            hs = slice((kh * group + g) * HD, (kh * group + g + 1) * HD)
            q = _rope(q_ref[:, hs], cq, sq).astype(BF16)
            s_loc = jnp.where(valid, _dot_nt(q, kw) * scale, NEG)
            s_ctx = _dot_nt(q, kc) * scale
            sink = sink_ref[kh * group + g]
            m = jnp.maximum(jnp.maximum(jnp.max(s_loc, axis=-1, keepdims=True),
                                        jnp.max(s_ctx, axis=-1, keepdims=True)), sink)
            p_loc = jnp.exp(s_loc - m)
            p_ctx = jnp.exp(s_ctx - m)
            den = (jnp.sum(p_loc, axis=-1, keepdims=True) + jnp.sum(p_ctx, axis=-1, keepdims=True)
                   + jnp.exp(sink - m))
            o = _dot(p_ctx.astype(BF16), vc) + _dot(p_loc.astype(BF16), vw)
            o_ref[:, hs] = (o / den).astype(BF16)


def attn_a_lat(u_attn, cache_k, cache_v, sink, rope_c, rope_s, layer, batch, seq, row0, kvh, group, blk=256):
    nb = seq // blk
    rb0 = row0 // blk
    qw = kvh * group * HD
    kw = kvh * HD
    kcol = qw // kw
    vcol = kcol + 1
    past = cache_k.shape[2]

    def rows(b, n):
        return rb0 + b * nb + n

    prev = lambda n: jnp.maximum(n - 1, 0)
    nxt = lambda n: jnp.minimum(n + 1, nb - 1)
    tab = lambda f: pl.BlockSpec((blk, HD), lambda b, n: (f(n), 0))
    kv = lambda f, col: pl.BlockSpec((blk, kw), lambda b, n: (rows(b, f(n)), col))
    same = lambda n: n
    return pl.pallas_call(
        functools.partial(_attn_a_lat_body, group=group, kvh=kvh, seq=seq),
        grid=(batch, nb),
        in_specs=[pl.BlockSpec(memory_space=pltpu.SMEM),
                  pl.BlockSpec((blk, qw), lambda b, n: (rows(b, n), 0)),
                  kv(prev, kcol), kv(same, kcol), kv(nxt, kcol),
                  kv(prev, vcol), kv(same, vcol), kv(nxt, vcol),
                  pl.BlockSpec((None, None, past, kw), lambda b, n: (b, layer, 0, 0)),
                  pl.BlockSpec((None, None, past, kw), lambda b, n: (b, layer, 0, 0)),
                  tab(same), tab(same), tab(prev), tab(prev), tab(nxt), tab(nxt)],
        out_specs=pl.BlockSpec((blk, qw), lambda b, n: (b * nb + n, 0)),
        out_shape=jax.ShapeDtypeStruct((batch * seq, qw), BF16),
        compiler_params=_cp(("arbitrary", "arbitrary"), 40),
        name="attn_a_lat",
    )(sink, u_attn, u_attn, u_attn, u_attn, u_attn, u_attn, u_attn, cache_k, cache_v,
      rope_c, rope_s, rope_c, rope_s, rope_c, rope_s)


def _lambda(lam_ref, lam_init):
    lw = lam_ref[...]
    return (jnp.exp(jnp.sum(lw[0:1] * lw[1:2], axis=-1, keepdims=True))
            - jnp.exp(jnp.sum(lw[2:3] * lw[3:4], axis=-1, keepdims=True)) + lam_init)


def _softmax_parts(parts):
    m = functools.reduce(jnp.maximum, [jnp.max(s, axis=-1, keepdims=True) for s in parts])
    ps = [jnp.exp(s - m) for s in parts]
    inv = 1.0 / functools.reduce(lambda a, b: a + b, [jnp.sum(p, axis=-1, keepdims=True) for p in ps])
    return [p * inv for p in ps]


def _attn_b_ctx_body(lam_ref, g_ref, q_ref, k_ref, v_ref, o_ref, *, lam_init):
    hd = HD // 2
    scale = hd ** -0.5
    lam = _lambda(lam_ref, lam_init)
    for h in range(q_ref.shape[1] // HD):
        hs = slice(h * HD, (h + 1) * HD)
        q = q_ref[:, hs].astype(BF16)
        k = k_ref[:, hs].astype(BF16)
        p0, = _softmax_parts([_dot_nt(q[:, :hd], k[:, :hd]) * scale])
        p1, = _softmax_parts([_dot_nt(q[:, hd:], k[:, hd:]) * scale])
        o = _dot((p0 - lam * p1).astype(BF16), v_ref[:, hs].astype(BF16))
        o_ref[:, hs] = (_rms_lanes(o, g_ref[...]) * (1.0 - lam_init)).astype(BF16)


def attn_b_ctx(u_attn, b_lambda, b_subln, lam_init, batch, seq, heads, col0):
    w = heads * HD
    qcol = col0 // w
    return pl.pallas_call(
        functools.partial(_attn_b_ctx_body, lam_init=lam_init),
        grid=(batch,),
        in_specs=[pl.BlockSpec(b_lambda.shape, lambda b: (0, 0)),
                  pl.BlockSpec((1, HD), lambda b: (0, 0)),
                  pl.BlockSpec((seq, w), lambda b: (b, qcol)),
                  pl.BlockSpec((seq, w), lambda b: (b, qcol + 1)),
                  pl.BlockSpec((seq, w), lambda b: (b, qcol + 2))],
        out_specs=pl.BlockSpec((seq, w), lambda b: (b, 0)),
        out_shape=jax.ShapeDtypeStruct((batch * seq, w), BF16),
        compiler_params=_cp(("arbitrary",), 32),
        name="attn_b_ctx",
    )(b_lambda, b_subln.reshape(1, HD), u_attn, u_attn, u_attn)


def _attn_b_lat_body(lam_ref, g_ref, q_ref, k_ref, v_ref, kc_ref, vc_ref, cq_ref, sq_ref, ck_ref, sk_ref,
                     o_ref, k_scr, kc_scr, v_scr, *, lam_init):
    hd = HD // 2
    scale = hd ** -0.5
    nq = pl.program_id(2)

    @pl.when(nq == 0)
    def _():
        k_scr[...] = _rope(k_ref[...], ck_ref[...], sk_ref[...]).T.astype(BF16)
        kc_scr[...] = kc_ref[...].T.astype(BF16)
        v_scr[...] = v_ref[...].astype(BF16)

    lam = _lambda(lam_ref, lam_init)
    k = k_scr[...]
    v = v_scr[...]
    kc = kc_scr[...]
    vc = vc_ref[...].astype(BF16)
    q = _rope(q_ref[...], cq_ref[...], sq_ref[...]) * (scale * LOG2E)
    lane = lax.broadcasted_iota(jnp.int32, q.shape, 1)
    outs = []
    for c in range(2):
        qc = jnp.where((lane >= hd) == (c == 1), q, 0.0).astype(BF16)
        s_ctx = _dot(qc, kc)
        s_lat = _dot(qc, k)
        m = jnp.maximum(jnp.max(s_ctx, axis=-1, keepdims=True), jnp.max(s_lat, axis=-1, keepdims=True))
        p_ctx = jnp.exp2(s_ctx - m)
        p_lat = jnp.exp2(s_lat - m)
        den = jnp.sum(p_ctx, axis=-1, keepdims=True) + jnp.sum(p_lat, axis=-1, keepdims=True)
        outs.append((_dot(p_ctx.astype(BF16), vc) + _dot(p_lat.astype(BF16), v)) / den)
    o = outs[0] - lam * outs[1]
    o_ref[...] = (_rms_lanes(o, g_ref[...]) * (1.0 - lam_init)).astype(BF16)


def attn_b_lat(u_attn, cache_k, cache_v, b_lambda, b_subln, rope_c, rope_s, lam_init, layer,
               batch, seq, row0, heads, col0, tq=256):
    qcol = col0 // HD
    kcol = qcol + heads
    vcol = kcol + heads
    nq = seq // tq
    past = cache_k.shape[2]
    rq0 = row0 // tq
    rs0 = row0 // seq
    return pl.pallas_call(
        functools.partial(_attn_b_lat_body, lam_init=lam_init),
        grid=(batch, heads, nq),
        in_specs=[pl.BlockSpec(b_lambda.shape, lambda b, h, n: (0, 0)),
                  pl.BlockSpec((1, HD), lambda b, h, n: (0, 0)),
                  pl.BlockSpec((tq, HD), lambda b, h, n: (rq0 + b * nq + n, qcol + h)),
                  pl.BlockSpec((seq, HD), lambda b, h, n: (rs0 + b, kcol + h)),
                  pl.BlockSpec((seq, HD), lambda b, h, n: (rs0 + b, vcol + h)),
                  pl.BlockSpec((None, None, past, HD), lambda b, h, n: (b, layer, 0, h)),
                  pl.BlockSpec((None, None, past, HD), lambda b, h, n: (b, layer, 0, h)),
                  pl.BlockSpec((tq, HD), lambda b, h, n: (n, 0)),
                  pl.BlockSpec((tq, HD), lambda b, h, n: (n, 0)),
                  pl.BlockSpec((seq, HD), lambda b, h, n: (0, 0)),
                  pl.BlockSpec((seq, HD), lambda b, h, n: (0, 0))],
        out_specs=pl.BlockSpec((tq, HD), lambda b, h, n: (b * nq + n, h)),
        out_shape=jax.ShapeDtypeStruct((batch * seq, heads * HD), BF16),
        scratch_shapes=[pltpu.VMEM((HD, seq), BF16), pltpu.VMEM((HD, past), BF16), pltpu.VMEM((seq, HD), BF16)],
        compiler_params=_cp(("arbitrary", "arbitrary", "arbitrary"), 40),
        name="attn_b_lat",
    )(b_lambda, b_subln.reshape(1, HD), u_attn, u_attn, u_attn, cache_k, cache_v,
      rope_c, rope_s, rope_c, rope_s)


def dft_tables(l):
    f = jnp.arange(l, dtype=jnp.int32)[:, None]
    s = jnp.arange(l, dtype=jnp.int32)[None, :]
    m = ((2 * f + 1) * s) % (4 * l)
    ang = m.astype(F32) * (math.pi / (2 * l))
    fm = jnp.concatenate([jnp.cos(ang), -jnp.sin(ang)], axis=0)
    hi = fm.astype(BF16)
    lo = (fm - hi.astype(F32)).astype(BF16)
    return hi, lo, hi.T


def hyena_feats(l):
    t = jnp.linspace(0.0, 1.0, l, dtype=F32)[:, None]
    bands = (HY_EMB - 1) // 2
    w = 2.0 * math.pi * jnp.arange(l, dtype=F32)[:, None] / l
    fr = jnp.linspace(1e-4, bands - 1, bands, dtype=F32)[None, :]
    feats = jnp.concatenate([t, jnp.cos(fr * w), -jnp.sin(fr * w)], axis=-1)
    return jnp.pad(feats, ((0, 0), (0, HD - HY_EMB)))


def hyena_decay(l, width):
    t = jnp.linspace(0.0, 1.0, l, dtype=F32)[:, None]
    deltas = jnp.abs(jnp.linspace(math.log(HY_TARGET) / HY_SLOW, math.log(HY_TARGET) / HY_FAST, width, dtype=F32))
    return jnp.exp(-t * deltas)


def _hy_time_body(feat_ref, dec_ref, w1_ref, b1_ref, w2_ref, b2_ref, w3_ref, fr_ref, kh_ref, kl_ref, nrm_scr,
                  *, width, l):
    ps = pl.program_id(2)
    c = pl.program_id(3)
    fr = fr_ref[...]
    h = jnp.sin(fr[0:1] * (_dot3(feat_ref[...], w1_ref[...]) + b1_ref[...]))
    h = jnp.sin(fr[1:2] * (_dot3(h, w2_ref[...]) + b2_ref[...]))
    dec = dec_ref[...]
    row = lax.broadcasted_iota(jnp.int32, dec.shape, 0)
    fwd = _dot3(h, w3_ref[:, :width]) * dec
    bwd = jnp.where((row == 0) & (c == 0), 0.0, _dot3(h, w3_ref[:, width:]) * dec)

    @pl.when((ps == 0) & (c == 0))
    def _():
        nrm_scr[...] = jnp.zeros_like(nrm_scr)

    @pl.when(ps == 0)
    def _():
        nrm_scr[...] += (jnp.sum(jnp.abs(fwd), axis=0, keepdims=True)
                         + jnp.sum(jnp.abs(bwd), axis=0, keepdims=True))

    @pl.when(ps == 1)
    def _():
        inv = 1.0 / ((nrm_scr[...] + EPS) * l)
        for part, val in ((0, fwd * inv), (1, bwd * inv)):
            hi, lo = _split(val)
            kh_ref[:, part * width:(part + 1) * width] = hi
            kl_ref[:, part * width:(part + 1) * width] = lo


def _hy_spec_body(fh_ref, fl_ref, kh_ref, kl_ref, o_ref, *, width, n_re):
    r = pl.program_id(2)
    fh = fh_ref[...]
    kh = kh_ref[...]
    res = _dot(fh, kh) + _dot(fh, kl_ref[...]) + _dot(fl_ref[...], kh)
    sign = jnp.where(r < n_re, 1.0, -1.0)
    o_ref[...] = res[:, :width] + sign * res[:, width:]


def hyena_filters(l, hy_w1p, hy_b1, hy_w2, hy_b2, hy_w3, hy_freq, f_hi, f_lo, width, rc=512, tr=512):
    depth = hy_w3.shape[0]
    ffn_w = hy_w2.shape[1]
    rc = min(rc, l)
    tr = min(tr, l)
    n_re = l // tr
    feats = hyena_feats(l)
    dec = hyena_decay(l, width)
    wspec = lambda shape: pl.BlockSpec((None,) + shape, lambda d, o, p, c: (d, 0, 0))
    kspec = pl.BlockSpec((None, rc, 2 * width), lambda d, o, p, c: (d, c * p, o))
    k_hi, k_lo = pl.pallas_call(
        functools.partial(_hy_time_body, width=width, l=l),
        grid=(depth, 2, 2, l // rc),
        in_specs=[pl.BlockSpec((rc, HD), lambda d, o, p, c: (c, 0)),
                  pl.BlockSpec((rc, width), lambda d, o, p, c: (c, 0)),
                  wspec((HD, ffn_w)), wspec((1, ffn_w)), wspec((ffn_w, ffn_w)), wspec((1, ffn_w)),
                  pl.BlockSpec((None, ffn_w, 2 * width), lambda d, o, p, c: (d, 0, o)),
                  wspec((2, ffn_w))],
        out_specs=[kspec, kspec],
        out_shape=[jax.ShapeDtypeStruct((depth, l, 4 * width), BF16)] * 2,
        scratch_shapes=[pltpu.VMEM((1, width), F32)],
        compiler_params=_cp(("arbitrary",) * 4, 32),
        name="hyena_time_filters",
    )(feats, dec, hy_w1p, hy_b1.reshape(depth, 1, ffn_w), hy_w2, hy_b2.reshape(depth, 1, ffn_w), hy_w3, hy_freq)
    fspec = pl.BlockSpec((tr, l), lambda d, o, r: (r, 0))
    kfull = pl.BlockSpec((None, l, 2 * width), lambda d, o, r: (d, 0, o))
    return pl.pallas_call(
        functools.partial(_hy_spec_body, width=width, n_re=n_re),
        grid=(depth, 2, 2 * n_re),
        in_specs=[fspec, fspec, kfull, kfull],
        out_specs=pl.BlockSpec((None, None, tr, width), lambda d, o, r: (d, o, r, 0)),
        out_shape=jax.ShapeDtypeStruct((depth, 2, 2 * l, width), F32),
        compiler_params=_cp(("arbitrary",) * 3, 48),
        name="hyena_spectra",
    )(f_hi, f_lo, k_hi, k_lo)


def _short_conv(u, w, b):
    l = u.shape[0]
    row = lax.broadcasted_iota(jnp.int32, u.shape, 0)
    up = jnp.where(row == 0, 0.0, pltpu.roll(u, 1, 0))
    dn = jnp.where(row == l - 1, 0.0, pltpu.roll(u, l - 1, 0))
    return up * w[0:1] + u * w[1:2] + dn * w[2:3] + b


def _hyena_body(hv_ref, h1_ref, h2_ref, wv_ref, w1_ref, w2_ref, bv_ref, b1_ref, b2_ref, bias_ref,
                fre_ref, fim_ref, gre_ref, gim_ref, kre_ref, kim_ref, o_ref,
                z_scr, zb_scr, acc_scr, *, nf):
    ph = pl.program_id(2)
    j = pl.program_id(3)

    @pl.when((ph == 0) & (j == 0))
    def _():
        v = _short_conv(hv_ref[...].astype(F32), wv_ref[...], bv_ref[...])
        z_scr[...] = v
        zb_scr[...] = v.astype(BF16)

    @pl.when(j == 0)
    def _():
        acc_scr[...] = jnp.zeros_like(acc_scr)

    zb = zb_scr[...]
    zre = _dot(fre_ref[...], zb)
    zim = _dot(fim_ref[...], zb)
    kre, kim = kre_ref[...], kim_ref[...]
    yre = (zre * kre - zim * kim).astype(BF16)
    yim = (zre * kim + zim * kre).astype(BF16)
    acc_scr[...] += _dot(gre_ref[...], yre) + _dot(gim_ref[...], yim)

    @pl.when((j == nf - 1) & (ph == 0))
    def _():
        x1 = _short_conv(h1_ref[...].astype(F32), w1_ref[...], b1_ref[...])
        y = x1 * (acc_scr[...] + z_scr[...] * bias_ref[0:1])
        z_scr[...] = y
        zb_scr[...] = y.astype(BF16)

    @pl.when((j == nf - 1) & (ph == 1))
    def _():
        x2 = _short_conv(h2_ref[...].astype(F32), w2_ref[...], b2_ref[...])
        o_ref[...] = (x2 * (acc_scr[...] + z_scr[...] * bias_ref[1:2])).astype(BF16)


def hyena(u_hy, conv_w, conv_b, bias, kf, f_hi, g_hi, layer, batch, seq, row0, width, cw=512, tfq=256):
    nch = width // cw
    tfq = min(tfq, seq)
    nf = seq // tfq
    rb = row0 // seq
    hspec = lambda part: pl.BlockSpec((seq, cw), lambda b, c, p, j: (rb + b, part * nch + c))
    wspec = lambda part: pl.BlockSpec((3, cw), lambda b, c, p, j: (0, part * nch + c))
    bspec = lambda part: pl.BlockSpec((1, cw), lambda b, c, p, j: (0, part * nch + c))
    return pl.pallas_call(
        functools.partial(_hyena_body, nf=nf),
        grid=(batch, nch, 2, nf),
        in_specs=[hspec(0), hspec(1), hspec(2), wspec(0), wspec(1), wspec(2), bspec(0), bspec(1), bspec(2),
                  pl.BlockSpec((2, cw), lambda b, c, p, j: (0, c)),
                  pl.BlockSpec((tfq, seq), lambda b, c, p, j: (j, 0)),
                  pl.BlockSpec((tfq, seq), lambda b, c, p, j: (nf + j, 0)),
                  pl.BlockSpec((seq, tfq), lambda b, c, p, j: (0, j)),
                  pl.BlockSpec((seq, tfq), lambda b, c, p, j: (0, nf + j)),
                  pl.BlockSpec((None, None, tfq, cw), lambda b, c, p, j: (layer, p, j, c)),
                  pl.BlockSpec((None, None, tfq, cw), lambda b, c, p, j: (layer, p, nf + j, c))],
        out_specs=pl.BlockSpec((seq, cw), lambda b, c, p, j: (b, c)),
        out_shape=jax.ShapeDtypeStruct((batch * seq, width), BF16),
        scratch_shapes=[pltpu.VMEM((seq, cw), F32), pltpu.VMEM((seq, cw), BF16), pltpu.VMEM((seq, cw), F32)],
        compiler_params=_cp(("arbitrary",) * 4, 56),
        name="hyena",
    )(u_hy, u_hy, u_hy, conv_w, conv_w, conv_w, conv_b, conv_b, conv_b, bias,
      f_hi, f_hi, g_hi, g_hi, kf, kf)


def _hgrn_pair_level(t, s, fwd):
    x = t ^ s
    lvl = jnp.where(x == 0, 0, 32 - lax.clz(x))
    used = (s <= t) if fwd else (s >= t)
    return jnp.where(used, lvl, -1)


def _hgrn_midpoint(b3, blk, fwd):
    rows = b3.shape[1]
    r0 = blk // 2 - 1 if fwd else blk // 2
    ref = b3[:, r0:r0 + 1, :]
    if rows > blk:
        pos = lax.broadcasted_iota(jnp.int32, b3.shape, 1)
        for j in range(1, rows // blk):
            ref = jnp.where(pos >= j * blk, b3[:, j * blk + r0:j * blk + r0 + 1, :], ref)
    return ref


def _hgrn_chunk(q, k, g, v, st, tri, lvl, fwd):
    c = CHUNK
    gh, gl = _split(g)
    b = _dot(tri, gh) + _dot(tri, gl)
    tot = b[c - 1:c] if fwd else b[0:1]
    vb = v.astype(BF16)
    o = _dot_nt((q * jnp.exp2(b)).astype(BF16), st.astype(BF16))
    kd = (k * jnp.exp2(tot - b)).astype(BF16)
    st_new = st * jnp.exp2(tot) + _dot_tn(vb, kd)

    a = jnp.where(lvl == 0, jnp.sum(q * k, axis=-1, keepdims=True), 0.0)
    a = jnp.where(lvl == 1, _dot_nt((q * jnp.exp2(g)).astype(BF16), k.astype(BF16)), a)
    blk, m = 4, 2
    while blk <= c:
        rows = max(blk, SUB)
        b3, q3, k3 = (x.reshape(c // rows, rows, HD) for x in (b, q, k))
        e = jnp.exp2(-jnp.abs(b3 - _hgrn_midpoint(b3, blk, fwd)))
        qt = (q3 * e).reshape(c, HD).astype(BF16)
        kt = (k3 * e).reshape(c, HD).astype(BF16)
        a = jnp.where(lvl == m, _dot_nt(qt, kt), a)
        blk, m = blk * 2, m + 1
    return o + _dot(a.astype(BF16), vb), st_new


def _hgrn_gate(z, lb):
    e = jnp.exp(-jnp.abs(z))
    r = 1.0 / (1.0 + e)
    pos = z >= 0
    sig_pos = jnp.where(pos, r, e * r)
    sig_neg = jnp.where(pos, e * r, r)
    f = sig_pos + jnp.maximum(lb, LB_FLOOR) * sig_neg
    return jnp.log(f) * LOG2E, (1.0 - lb) * sig_neg


def _hgrn_lower_bound(lb_ref, layer, depth, direction):
    x = lb_ref[...]
    rows = [x[2 * i + direction:2 * i + direction + 1] for i in range(depth)]
    m = functools.reduce(jnp.maximum, rows)
    es = [jnp.exp(r - m) for r in rows]
    tot = functools.reduce(lambda a, b: a + b, es)
    cum = jnp.zeros_like(tot)
    for i in range(1, layer + 1):
        cum = cum + es[i]
    return jnp.maximum(cum / tot, 0.0)


def _hgrn_body(*refs, layer, depth, has_state, emit_state):
    it = iter(refs)
    lb_ref, ng_ref, q_ref, ff_ref, fb_ref, i_ref, g_ref = (next(it) for _ in range(7))
    s0_ref = next(it) if has_state else None
    o_ref = next(it)
    so_ref = next(it) if emit_state else None
    of_scr, ob_scr, st_scr, tri_scr, lvl_scr = (next(it) for _ in range(5))
    nc = q_ref.shape[0] // CHUNK
    hp = q_ref.shape[1] // HD
    ti = lax.broadcasted_iota(jnp.int32, (CHUNK, CHUNK), 0)
    si = lax.broadcasted_iota(jnp.int32, (CHUNK, CHUNK), 1)
    chains = [(h, d) for h in range(hp) for d in range(2)]
    lbs = {}
    for d in range(2):
        tri_scr[d] = jnp.where((si <= ti) if d == 0 else (si >= ti), 1.0, 0.0).astype(BF16)
        lvl_scr[d] = _hgrn_pair_level(ti, si, d == 0)
        lb_all = _hgrn_lower_bound(lb_ref, layer, depth, d)
        for h in range(hp):
            lbs[h, d] = lb_all[:, h * HD:(h + 1) * HD]
            st_scr[2 * h + d] = s0_ref[d, h].T if has_state else jnp.zeros((HD, HD), F32)

    def step(i, carry):
        for h, d in chains:
            fwd = d == 0
            ci = i if fwd else nc - 1 - i
            sl = pl.ds(pl.multiple_of(ci * CHUNK, CHUNK), CHUNK)
            cs = slice(h * HD, (h + 1) * HD)
            xq = q_ref[sl, cs].astype(F32)
            log_f, kk = _hgrn_gate((ff_ref if fwd else fb_ref)[sl, cs].astype(F32), lbs[h, d])
            o, st = _hgrn_chunk(xq * _sigmoid(xq), kk, log_f, i_ref[sl, cs].astype(F32), st_scr[2 * h + d],
                                tri_scr[d], lvl_scr[d], fwd)
            st_scr[2 * h + d] = st
            (of_scr if fwd else ob_scr)[sl, cs] = o
        return carry

    lax.fori_loop(0, nc, step, 0)
    for h, d in chains:
        if emit_state:
            so_ref[d, h] = st_scr[2 * h + d].T
    for h in range(hp):
        cs = slice(h * HD, (h + 1) * HD)
        xg = g_ref[:, cs].astype(F32)
        o_ref[:, cs] = (_rms_lanes(of_scr[:, cs] + ob_scr[:, cs], ng_ref[...]) * (xg * _sigmoid(xg))).astype(BF16)


def hgrn(u_hg, hg_lb2, hg_norm, state, layer, depth, batch, seq, row0, heads, emit_state, col0=0, hp=4):
    rb = row0 // seq
    ng = heads // hp
    w = hp * HD
    cb = col0 // w
    part = lambda p: pl.BlockSpec((seq, w), lambda b, h: (rb + b, cb + p * ng + h))
    in_specs = [pl.BlockSpec((2 * depth, w), lambda b, h: (0, h)),
                pl.BlockSpec((1, HD), lambda b, h: (0, 0)),
                part(0), part(1), part(2), part(3), part(4)]
    args = [hg_lb2, hg_norm.reshape(1, HD), u_hg, u_hg, u_hg, u_hg, u_hg]
    if state is not None:
        in_specs.append(pl.BlockSpec((None, None, 2, hp, HD, HD), lambda b, h: (b, layer, 0, h, 0, 0)))
        args.append(state)
    out_specs = [pl.BlockSpec((seq, w), lambda b, h: (b, h))]
    out_shape = [jax.ShapeDtypeStruct((batch * seq, heads * HD), BF16)]
    if emit_state:
        out_specs.append(pl.BlockSpec((None, 2, hp, HD, HD), lambda b, h: (b, 0, h, 0, 0)))
        out_shape.append(jax.ShapeDtypeStruct((batch, 2, heads, HD, HD), F32))
    return pl.pallas_call(
        functools.partial(_hgrn_body, layer=layer, depth=depth, has_state=state is not None,
                          emit_state=emit_state),
        grid=(batch, ng),
        in_specs=in_specs,
        out_specs=out_specs,
        out_shape=out_shape,
        scratch_shapes=[pltpu.VMEM((seq, w), F32), pltpu.VMEM((seq, w), F32),
                        pltpu.VMEM((2 * hp, HD, HD), F32),
                        pltpu.VMEM((2, CHUNK, CHUNK), BF16), pltpu.VMEM((2, CHUNK, CHUNK), jnp.int32)],
        compiler_params=_cp(("arbitrary", "arbitrary"), 48),
        name="hgrn",
    )(*args)


def _merge_body(*refs, n_ctx_tiles):
    o_refs, (gt_ref, x_ref, mod_ref, wb_ref, wo_ref, out_ref) = refs[:2 * N_BRANCH], refs[2 * N_BRANCH:]
    d = x_ref.shape[1]
    is_ctx = pl.program_id(0) < n_ctx_tiles
    acc = None
    for n in range(N_BRANCH):
        o = jnp.where(is_ctx, o_refs[2 * n][...], o_refs[2 * n + 1][...])
        y = gt_ref[:, n * d:(n + 1) * d].astype(F32) * _dot(o, wb_ref[n])
        acc = y if acc is None else acc + y
    out = _dot(acc.astype(BF16), wo_ref[...])
    out_ref[...] = x_ref[...] + mod_ref[...][5:6] * out


def merge(branch_outs, gates, x, mod_l, w_branch, w_out, t_ctx, l_lat, tm=256):
    t, d = x.shape
    mw = w_branch.shape[1]
    nct = t_ctx // tm
    row = functools.partial(_mod_row, tm=tm, t_ctx=t_ctx, l_lat=l_lat)
    cspec = pl.BlockSpec((tm, mw), lambda i: (jnp.minimum(i, nct - 1), 0))
    lspec = pl.BlockSpec((tm, mw), lambda i: (jnp.maximum(i - nct, 0), 0))
    return pl.pallas_call(
        functools.partial(_merge_body, n_ctx_tiles=nct),
        grid=(t // tm,),
        in_specs=[cspec, lspec] * N_BRANCH + [
                  pl.BlockSpec((tm, N_BRANCH * d), lambda i: (i, 0)),
                  pl.BlockSpec((tm, d), lambda i: (i, 0)),
                  pl.BlockSpec((None, N_MOD, d), lambda i: (row(i), 0, 0)),
                  pl.BlockSpec((N_BRANCH, mw, d), lambda i: (0, 0, 0), pipeline_mode=pl.Buffered(1)),
                  pl.BlockSpec((d, d), lambda i: (0, 0), pipeline_mode=pl.Buffered(1))],
        out_specs=pl.BlockSpec((tm, d), lambda i: (i, 0)),
        out_shape=jax.ShapeDtypeStruct((t, d), F32),
        compiler_params=_cp(("arbitrary",), 56),
        name="merge",
    )(*[o for pair in branch_outs for o in pair], gates, x, mod_l, w_branch, w_out)


def _final_norm_body(x_ref, g_ref, oc_ref, ol_ref, *, n_ctx_tiles):
    i = pl.program_id(0)

    @pl.when(i < n_ctx_tiles)
    def _():
        oc_ref[...] = _rms_lanes(x_ref[...], g_ref[...])

    @pl.when(i >= n_ctx_tiles)
    def _():
        ol_ref[...] = _rms_lanes(x_ref[...], g_ref[...])


def final_norm(x, g, t_ctx, tm=512):
    t, d = x.shape
    nct = t_ctx // tm
    return pl.pallas_call(
        functools.partial(_final_norm_body, n_ctx_tiles=nct),
        grid=(t // tm,),
        in_specs=[pl.BlockSpec((tm, d), lambda i: (i, 0)), pl.BlockSpec((1, d), lambda i: (0, 0))],
        out_specs=[pl.BlockSpec((tm, d), lambda i: (jnp.minimum(i, nct - 1), 0)),
                   pl.BlockSpec((tm, d), lambda i: (jnp.maximum(i - nct, 0), 0))],
        out_shape=[jax.ShapeDtypeStruct((t_ctx, d), F32), jax.ShapeDtypeStruct((t - t_ctx, d), F32)],
        compiler_params=_cp(("arbitrary",), 32),
        name="final_norm",
    )(x, g.reshape(1, d))


def kernel(x_prompt, x_sample, c, cache_a_k, cache_a_v, cache_b_k, cache_b_v, state_hgrn, c_ctx, w_mod, b_mod, norm_g, w_ffn1_gu, w_ffn1_down, w_ffn2_gu, w_ffn2_down, w_in, w_branch, w_out, a_sink, b_lambda, b_subln, hy_conv_w, hy_conv_b, hy_w1, hy_b1, hy_w2, hy_b2, hy_w3, hy_freq, hy_bias, hg_lb, hg_norm, final_g):
    batch, seq, d = x_prompt.shape
    dec_batch, dec_seq, _ = x_sample.shape
    depth = w_mod.shape[0]
    mix_w = w_branch.shape[2]
    a_heads = a_sink.shape[1]
    a_kvh = cache_a_k.shape[3]
    a_group = a_heads // a_kvh
    b_heads = cache_b_k.shape[3]
    hg_heads = state_hgrn.shape[3]
    t_ctx, t_lat = batch * seq, dec_batch * dec_seq

    n_attn = (a_heads + 2 * a_kvh + 3 * b_heads) * HD
    b_col0 = (a_heads + 2 * a_kvh) * HD

    x = jnp.concatenate([x_prompt.reshape(t_ctx, d), x_sample.reshape(t_lat, d)], axis=0)
    cond = jnp.concatenate([c_ctx[None, :], c], axis=0)
    cond = jnp.pad(cond, ((0, MOD_ROWS - cond.shape[0]), (0, 0)))
    mod = modulation(cond, w_mod, b_mod).reshape(depth, MOD_ROWS, N_MOD, d)

    rope_a = rope_tables(dec_seq, HD, 1)
    rope_b = rope_tables(dec_seq, HD // 2, 2)
    fc_hi, fc_lo, gc_hi = dft_tables(seq)
    fl_hi, fl_lo, gl_hi = dft_tables(dec_seq)
    w1p = jnp.pad(hy_w1, ((0, 0), (0, HD - hy_w1.shape[1]), (0, 0)))
    kf_ctx = hyena_filters(seq, w1p, hy_b1, hy_w2, hy_b2, hy_w3, hy_freq, fc_hi, fc_lo, mix_w)
    kf_lat = hyena_filters(dec_seq, w1p, hy_b1, hy_w2, hy_b2, hy_w3, hy_freq, fl_hi, fl_lo, mix_w)

    cak = cache_a_k.reshape(dec_batch, depth, -1, a_kvh * HD)
    cav = cache_a_v.reshape(dec_batch, depth, -1, a_kvh * HD)
    cbk = cache_b_k.reshape(dec_batch, depth, -1, b_heads * HD)
    cbv = cache_b_v.reshape(dec_batch, depth, -1, b_heads * HD)
    hg_lb2 = hg_lb.reshape(depth * 2, mix_w)

    ak_l, av_l, bk_l, bv_l, st_l = [], [], [], [], []
    for l in range(depth):
        lam_init = 0.8 - 0.6 * math.exp(-0.3 * l)
        mod_l = mod[l]
        x, h_mix = ffn(x, mod_l, norm_g[l, 0:1], w_ffn1_gu[l].astype(BF16), w_ffn1_down[l].astype(BF16), 0,
                       t_ctx, dec_seq, next_g_row=norm_g[l, 1:2])
        n_rec = n_attn + 8 * mix_w
        u_attn = proj(h_mix, w_in[l, :, :n_attn].astype(BF16), F32, False)
        u_rec = proj(h_mix, w_in[l, :, n_attn:n_rec].astype(BF16), BF16, False)
        gates = proj(h_mix, w_in[l, :, n_rec:].astype(BF16), BF16, True)

        uc = u_attn[:t_ctx]
        k0 = a_heads * HD
        ak_l.append(uc[:, k0:k0 + a_kvh * HD].reshape(batch, seq, a_kvh, HD))
        av_l.append(uc[:, k0 + a_kvh * HD:b_col0].reshape(batch, seq, a_kvh, HD))
        k1 = b_col0 + b_heads * HD
        bk_l.append(uc[:, k1:k1 + b_heads * HD].reshape(batch, seq, b_heads, 2, HD // 2))
        bv_l.append(uc[:, k1 + b_heads * HD:].reshape(batch, seq, b_heads, HD))

        oa_c = attn_a_ctx(u_attn, a_sink[l], batch, seq, a_kvh, a_group)
        oa_s = attn_a_lat(u_attn, cak, cav, a_sink[l], rope_a[0], rope_a[1], l, dec_batch, dec_seq, t_ctx,
                          a_kvh, a_group)
        ob_c = attn_b_ctx(u_attn, b_lambda[l], b_subln[l], lam_init, batch, seq, b_heads, b_col0)
        ob_s = attn_b_lat(u_attn, cbk, cbv, b_lambda[l], b_subln[l], rope_b[0], rope_b[1], lam_init, l,
                          dec_batch, dec_seq, t_ctx, b_heads, b_col0)
        cb = hy_conv_b[l].reshape(1, -1)
        oc_c = hyena(u_rec, hy_conv_w[l], cb, hy_bias[l], kf_ctx, fc_hi, gc_hi, l, batch, seq, 0, mix_w)
        oc_s = hyena(u_rec, hy_conv_w[l], cb, hy_bias[l], kf_lat, fl_hi, gl_hi, l, dec_batch, dec_seq, t_ctx, mix_w)
        od_c, st = hgrn(u_rec, hg_lb2, hg_norm[l], None, l, depth, batch, seq, 0, hg_heads, True, col0=3 * mix_w)
        od_s, = hgrn(u_rec, hg_lb2, hg_norm[l], state_hgrn, l, depth, dec_batch, dec_seq, t_ctx, hg_heads, False,
                     col0=3 * mix_w)
        st_l.append(st)

        x = merge(((oa_c, oa_s), (ob_c, ob_s), (oc_c, oc_s), (od_c, od_s)), gates, x, mod_l,
                  w_branch[l].astype(BF16), w_out[l].astype(BF16), t_ctx, dec_seq)
        x = ffn(x, mod_l, norm_g[l, 2:3], w_ffn2_gu[l].astype(BF16), w_ffn2_down[l].astype(BF16), 2,
                t_ctx, dec_seq)

    y_ctx, y_lat = final_norm(x, final_g, t_ctx)
    return (y_ctx.reshape(batch, seq, d), y_lat.reshape(dec_batch, dec_seq, d),
            jnp.stack(ak_l, axis=1), jnp.stack(av_l, axis=1), jnp.stack(bk_l, axis=1), jnp.stack(bv_l, axis=1),
            jnp.stack(st_l, axis=1))
```

```python
import functools
import math

import jax
import jax.numpy as jnp
import numpy as np
from jax import lax
from jax.experimental import pallas as pl
from jax.experimental.pallas import tpu as pltpu

F32 = jnp.float32
BF16 = jnp.bfloat16

EPS = 1e-6
NEG = -1e30
LB_FLOOR = 1e-30
ROPE_BASE = 10000.0
GRID_W = 64
N_MOD = 9
N_BRANCH = 4
HD = 128
A_WINDOW = 128
HY_EMB = 33
HY_TARGET, HY_FAST, HY_SLOW = 1e-2, 0.3, 1.5
CHUNK = 128
SUB = 8
LOG2E = 1.4426950408889634
MOD_ROWS = 16
FFN_TILE = 512
PROJ_TILE = 512
VMEM_MB = 2 ** 20


def _cp(sem, vmem_mb):
    return pltpu.CompilerParams(dimension_semantics=sem, vmem_limit_bytes=vmem_mb * VMEM_MB)


def _dot(a, b):
    return jnp.dot(a, b, preferred_element_type=F32)


def _dot_nt(a, b):
    return lax.dot_general(a, b, (((1,), (1,)), ((), ())), preferred_element_type=F32)


def _dot_tn(a, b):
    return lax.dot_general(a, b, (((0,), (0,)), ((), ())), preferred_element_type=F32)


def _split(a):
    hi = a.astype(BF16)
    lo = (a - hi.astype(F32)).astype(BF16)
    return hi, lo


def _dot3(a, b):
    ah, al = _split(a)
    bh, bl = _split(b)
    return _dot(ah, bh) + _dot(ah, bl) + _dot(al, bh)


def _sigmoid(x):
    return 0.5 * jnp.tanh(0.5 * x) + 0.5


def _norm_mod(x, g, sc, sh):
    y = x * lax.rsqrt(jnp.mean(x * x, axis=-1, keepdims=True) + EPS)
    return (y * g) * (1.0 + sc) + sh


def _norm_mod_cols(load, keep, reload, store, shape, g, sc, sh):
    n, d = shape
    tiles = [slice(t, t + HD) for t in range(0, d, HD)]
    ssq = jnp.zeros((n, HD), F32)
    for cols in tiles:
        x = load(cols)
        if keep is not None:
            keep(cols, x)
        ssq = ssq + x * x
    r = lax.rsqrt(jnp.sum(ssq, axis=-1, keepdims=True) * (1.0 / d) + EPS)
    gs = g * (1.0 + sc)
    for cols in tiles:
        store(cols, (reload(cols) * r) * gs[:, cols] + sh[:, cols])


def _rms_lanes(x, g):
    return x * lax.rsqrt(jnp.mean(x * x, axis=-1, keepdims=True) + EPS) * g


def _mod_body(c_ref, w_ref, b_ref, o_ref):
    c = c_ref[...]
    a = (c * _sigmoid(c)).astype(BF16)
    o_ref[...] = _dot(a, w_ref[...].astype(BF16)) + b_ref[...]


def modulation(cond, w_mod, b_mod, tn=1024):
    depth, d, n = w_mod.shape
    return pl.pallas_call(
        _mod_body,
        grid=(depth, n // tn),
        in_specs=[pl.BlockSpec((MOD_ROWS, d), lambda l, j: (0, 0)),
                  pl.BlockSpec((None, d, tn), lambda l, j: (l, 0, j)),
                  pl.BlockSpec((None, 1, tn), lambda l, j: (l, 0, j))],
        out_specs=pl.BlockSpec((None, MOD_ROWS, tn), lambda l, j: (l, 0, j)),
        out_shape=jax.ShapeDtypeStruct((depth, MOD_ROWS, n), F32),
        compiler_params=_cp(("arbitrary", "arbitrary"), 40),
        name="modulation",
    )(cond, w_mod, b_mod.reshape(depth, 1, n))


def _mod_row(i, tm, t_ctx, l_lat):
    start = i * tm
    return jnp.where(start < t_ctx, 0, 1 + (start - t_ctx) // l_lat)


def _ffn_body(*refs, sub, nj, next_sub):
    if next_sub is None:
        x_ref, mod_ref, g_ref, wg_ref, wu_ref, wd_ref, o_ref, h_scr = refs
    else:
        x_ref, mod_ref, g_ref, gn_ref, wg_ref, wu_ref, wd_ref, o_ref, hn_ref, h_scr = refs
    j = pl.program_id(1)

    @pl.when(j == 0)
    def _():
        m = mod_ref[...]
        load = lambda cols: x_ref[:, cols]

        def store(cols, h):
            h_scr[:, cols] = h.astype(BF16)

        _norm_mod_cols(load, None, load, store, x_ref.shape, g_ref[...],
                       m[3 * sub + 1:3 * sub + 2], m[3 * sub:3 * sub + 1])
        o_ref[...] = jnp.zeros_like(o_ref)

    h = h_scr[...]
    tf = wg_ref.shape[1]
    hc = tf // 2
    part = None
    for c0 in range(0, tf, hc):
        a = _dot(h, wg_ref[:, c0:c0 + hc])
        b = _dot(h, wu_ref[:, c0:c0 + hc])
        act = (a * _sigmoid(a) * b).astype(BF16)
        y = _dot(act, wd_ref[c0:c0 + hc, :])
        part = y if part is None else part + y
    o_ref[...] += part

    @pl.when(j == nj - 1)
    def _():
        m = mod_ref[...]
        ga = 0.5 * m[3 * sub + 2:3 * sub + 3]
        if next_sub is None:
            o_ref[...] = x_ref[...] + ga * o_ref[...]
        else:
            def keep(cols, y):
                o_ref[:, cols] = y

            def store(cols, h):
                hn_ref[:, cols] = h.astype(BF16)

            _norm_mod_cols(lambda cols: x_ref[:, cols] + ga[:, cols] * o_ref[:, cols], keep,
                           lambda cols: o_ref[:, cols], store, x_ref.shape, gn_ref[...],
                           m[3 * next_sub + 1:3 * next_sub + 2], m[3 * next_sub:3 * next_sub + 1])


def col_tiles(w, tn):
    k, n = w.shape
    return w.reshape(k, n // tn, tn).transpose(1, 0, 2).astype(BF16)


def ffn(x, mod_l, g_row, w_gu, w_dn, sub, t_ctx, l_lat, next_g_row=None, tm=512):
    t, d = x.shape
    dff = w_dn.shape[0]
    tf = w_gu.shape[2]
    nj = dff // tf
    emit = next_g_row is not None
    row = functools.partial(_mod_row, tm=tm, t_ctx=t_ctx, l_lat=l_lat)
    gspec = pl.BlockSpec((1, d), lambda i, j: (0, 0))
    xspec = pl.BlockSpec((tm, d), lambda i, j: (i, 0))
    outs = pl.pallas_call(
        functools.partial(_ffn_body, sub=sub, nj=nj, next_sub=sub + 1 if emit else None),
        grid=(t // tm, nj),
        in_specs=[xspec,
                  pl.BlockSpec((None, N_MOD, d), lambda i, j: (row(i), 0, 0)),
                  gspec] + ([gspec] if emit else []) + [
                  pl.BlockSpec((None, d, tf), lambda i, j: (j, 0, 0)),
                  pl.BlockSpec((None, d, tf), lambda i, j: (j + nj, 0, 0)),
                  pl.BlockSpec((tf, d), lambda i, j: (j, 0))],
        out_specs=[xspec] + ([xspec] if emit else []),
        out_shape=[jax.ShapeDtypeStruct((t, d), F32)] + ([jax.ShapeDtypeStruct((t, d), BF16)] if emit else []),
        scratch_shapes=[pltpu.VMEM((tm, d), BF16)],
        compiler_params=_cp(("arbitrary", "arbitrary"), 56),
        name="ffn",
    )(x, mod_l, g_row, *([next_g_row] if emit else []), w_gu, w_gu, w_dn)
    return outs if emit else outs[0]


def _proj_body(h_ref, w_ref, o_ref, *, gate):
    r = _dot(h_ref[...], w_ref[...])
    o_ref[...] = (_sigmoid(r) if gate else r).astype(o_ref.dtype)


def proj(h, w, out_dtype, gate, tm=2048):
    t, d = h.shape
    nj, _, tn = w.shape
    n = nj * tn
    while t % tm:
        tm //= 2
    return pl.pallas_call(
        functools.partial(_proj_body, gate=gate),
        grid=(t // tm, nj),
        in_specs=[pl.BlockSpec((tm, d), lambda i, j: (i, 0)),
                  pl.BlockSpec((None, d, tn), lambda i, j: (j, 0, 0))],
        out_specs=pl.BlockSpec((tm, tn), lambda i, j: (i, j)),
        out_shape=jax.ShapeDtypeStruct((t, n), out_dtype),
        compiler_params=_cp(("arbitrary", "arbitrary"), 56),
        name="proj",
    )(h, w)


def rope_tables(l, rot_dim, reps):
    n_rows = l // GRID_W
    rows = jnp.broadcast_to(jnp.arange(n_rows, dtype=F32)[:, None], (n_rows, GRID_W)).reshape(-1)
    cols = jnp.broadcast_to(jnp.arange(GRID_W, dtype=F32)[None, :], (n_rows, GRID_W)).reshape(-1)
    axis_dim = rot_dim // 2
    inv = ROPE_BASE ** (-jnp.arange(0, axis_dim, 2, dtype=F32) / axis_dim)
    ang = jnp.concatenate([rows[:, None] * inv, cols[:, None] * inv], axis=-1)
    cos = jnp.repeat(jnp.cos(ang), 2, axis=-1)
    sin = jnp.repeat(jnp.sin(ang), 2, axis=-1)
    sign = jnp.tile(jnp.array([-1.0, 1.0], F32), rot_dim // 2)
    return jnp.tile(cos, (1, reps)), jnp.tile(sin * sign, (1, reps))


def _rope(x, c, s):
    lane = lax.broadcasted_iota(jnp.int32, x.shape, 1)
    nxt = pltpu.roll(x, x.shape[1] - 1, 1)
    prv = pltpu.roll(x, 1, 1)
    return x * c + jnp.where((lane & 1) == 0, nxt, prv) * s


def _attn_a_ctx_body(sink_ref, q_ref, k_ref, v_ref, o_ref, *, group, kvh):
    scale = HD ** -0.5
    for kh in range(kvh):
        k = k_ref[:, kh * HD:(kh + 1) * HD].astype(BF16)
        v = v_ref[:, kh * HD:(kh + 1) * HD].astype(BF16)
        for g in range(group):
            hs = slice((kh * group + g) * HD, (kh * group + g + 1) * HD)
            q = q_ref[:, hs].astype(BF16)
            s = _dot_nt(q, k) * scale
            sink = sink_ref[kh * group + g]
            m = jnp.maximum(jnp.max(s, axis=-1, keepdims=True), sink)
            p = jnp.exp(s - m)
            den = jnp.sum(p, axis=-1, keepdims=True) + jnp.exp(sink - m)
            o_ref[:, hs] = (_dot(p.astype(BF16), v) / den).astype(BF16)


def attn_a_ctx(u_attn, sink, batch, seq, kvh, group):
    qw = kvh * group * HD
    kw = kvh * HD
    kcol = qw // kw
    return pl.pallas_call(
        functools.partial(_attn_a_ctx_body, group=group, kvh=kvh),
        grid=(batch,),
        in_specs=[pl.BlockSpec(memory_space=pltpu.SMEM),
                  pl.BlockSpec((seq, qw), lambda b: (b, 0)),
                  pl.BlockSpec((seq, kw), lambda b: (b, kcol)),
                  pl.BlockSpec((seq, kw), lambda b: (b, kcol + 1))],
        out_specs=pl.BlockSpec((seq, qw), lambda b: (b, 0)),
        out_shape=jax.ShapeDtypeStruct((batch * seq, qw), BF16),
        compiler_params=_cp(("arbitrary",), 32),
        name="attn_a_ctx",
    )(sink, u_attn, u_attn, u_attn)


def _attn_a_lat_body(sink_ref, q_ref, k0_ref, k1_ref, k2_ref, v0_ref, v1_ref, v2_ref, kc_ref, vc_ref,
                     cq_ref, sq_ref, c0_ref, s0_ref, c2_ref, s2_ref, o_ref, *, group, kvh, seq):
    n = pl.program_id(1)
    blk = q_ref.shape[0]
    scale = HD ** -0.5
    cq, sq = cq_ref[...], sq_ref[...]
    qi = lax.broadcasted_iota(jnp.int32, (blk, 3 * blk), 0)
    kj = lax.broadcasted_iota(jnp.int32, (blk, 3 * blk), 1)
    kpos = (n - 1) * blk + kj
    qpos = n * blk + qi
    valid = (kpos >= 0) & (kpos < seq) & (jnp.abs(qpos - kpos) <= A_WINDOW)
    for kh in range(kvh):
        ks = slice(kh * HD, (kh + 1) * HD)
        kw = jnp.concatenate([_rope(k0_ref[:, ks], c0_ref[...], s0_ref[...]),
                              _rope(k1_ref[:, ks], cq, sq),
                              _rope(k2_ref[:, ks], c2_ref[...], s2_ref[...])], axis=0).astype(BF16)
        vw = jnp.concatenate([v0_ref[:, ks], v1_ref[:, ks], v2_ref[:, ks]], axis=0).astype(BF16)
        kc = kc_ref[:, ks].astype(BF16)
        vc = vc_ref[:, ks].astype(BF16)
        for g in range(group):
            hs = slice((kh * group + g) * HD, (kh * group + g + 1) * HD)
            q = _rope(q_ref[:, hs], cq, sq).astype(BF16)
            s_loc = jnp.where(valid, _dot_nt(q, kw) * scale, NEG)
            s_ctx = _dot_nt(q, kc) * scale
            sink = sink_ref[kh * group + g]
            m = jnp.maximum(jnp.maximum(jnp.max(s_loc, axis=-1, keepdims=True),
                                        jnp.max(s_ctx, axis=-1, keepdims=True)), sink)
            p_loc = jnp.exp(s_loc - m)
            p_ctx = jnp.exp(s_ctx - m)
            den = (jnp.sum(p_loc, axis=-1, keepdims=True) + jnp.sum(p_ctx, axis=-1, keepdims=True)
                   + jnp.exp(sink - m))
            o = _dot(p_ctx.astype(BF16), vc) + _dot(p_loc.astype(BF16), vw)
            o_ref[:, hs] = (o / den).astype(BF16)


def attn_a_lat(u_attn, cache_k, cache_v, sink, rope_c, rope_s, layer, batch, seq, row0, kvh, group, blk=256):
    nb = seq // blk
    rb0 = row0 // blk
    qw = kvh * group * HD
    kw = kvh * HD
    kcol = qw // kw
    vcol = kcol + 1
    past = cache_k.shape[2]

    def rows(b, n):
        return rb0 + b * nb + n

    prev = lambda n: jnp.maximum(n - 1, 0)
    nxt = lambda n: jnp.minimum(n + 1, nb - 1)
    tab = lambda f: pl.BlockSpec((blk, HD), lambda b, n: (f(n), 0))
    kv = lambda f, col: pl.BlockSpec((blk, kw), lambda b, n: (rows(b, f(n)), col))
    same = lambda n: n
    return pl.pallas_call(
        functools.partial(_attn_a_lat_body, group=group, kvh=kvh, seq=seq),
        grid=(batch, nb),
        in_specs=[pl.BlockSpec(memory_space=pltpu.SMEM),
                  pl.BlockSpec((blk, qw), lambda b, n: (rows(b, n), 0)),
                  kv(prev, kcol), kv(same, kcol), kv(nxt, kcol),
                  kv(prev, vcol), kv(same, vcol), kv(nxt, vcol),
                  pl.BlockSpec((None, None, past, kw), lambda b, n: (b, layer, 0, 0)),
                  pl.BlockSpec((None, None, past, kw), lambda b, n: (b, layer, 0, 0)),
                  tab(same), tab(same), tab(prev), tab(prev), tab(nxt), tab(nxt)],
        out_specs=pl.BlockSpec((blk, qw), lambda b, n: (b * nb + n, 0)),
        out_shape=jax.ShapeDtypeStruct((batch * seq, qw), BF16),
        compiler_params=_cp(("arbitrary", "arbitrary"), 40),
        name="attn_a_lat",
    )(sink, u_attn, u_attn, u_attn, u_attn, u_attn, u_attn, u_attn, cache_k, cache_v,
      rope_c, rope_s, rope_c, rope_s, rope_c, rope_s)


def _lambda(lam_ref, lam_init):
    lw = lam_ref[...]
    return (jnp.exp(jnp.sum(lw[0:1] * lw[1:2], axis=-1, keepdims=True))
            - jnp.exp(jnp.sum(lw[2:3] * lw[3:4], axis=-1, keepdims=True)) + lam_init)


def _softmax_parts(parts):
    m = functools.reduce(jnp.maximum, [jnp.max(s, axis=-1, keepdims=True) for s in parts])
    ps = [jnp.exp(s - m) for s in parts]
    inv = 1.0 / functools.reduce(lambda a, b: a + b, [jnp.sum(p, axis=-1, keepdims=True) for p in ps])
    return [p * inv for p in ps]


def _attn_b_ctx_body(lam_ref, g_ref, q_ref, k_ref, v_ref, o_ref, *, lam_init):
    hd = HD // 2
    scale = hd ** -0.5
    lam = _lambda(lam_ref, lam_init)
    for h in range(q_ref.shape[1] // HD):
        hs = slice(h * HD, (h + 1) * HD)
        q = q_ref[:, hs].astype(BF16)
        k = k_ref[:, hs].astype(BF16)
        p0, = _softmax_parts([_dot_nt(q[:, :hd], k[:, :hd]) * scale])
        p1, = _softmax_parts([_dot_nt(q[:, hd:], k[:, hd:]) * scale])
        o = _dot((p0 - lam * p1).astype(BF16), v_ref[:, hs].astype(BF16))
        o_ref[:, hs] = (_rms_lanes(o, g_ref[...]) * (1.0 - lam_init)).astype(BF16)


def attn_b_ctx(u_attn, b_lambda, b_subln, lam_init, batch, seq, heads, col0):
    w = heads * HD
    qcol = col0 // w
    return pl.pallas_call(
        functools.partial(_attn_b_ctx_body, lam_init=lam_init),
        grid=(batch,),
        in_specs=[pl.BlockSpec(b_lambda.shape, lambda b: (0, 0)),
                  pl.BlockSpec((1, HD), lambda b: (0, 0)),
                  pl.BlockSpec((seq, w), lambda b: (b, qcol)),
                  pl.BlockSpec((seq, w), lambda b: (b, qcol + 1)),
                  pl.BlockSpec((seq, w), lambda b: (b, qcol + 2))],
        out_specs=pl.BlockSpec((seq, w), lambda b: (b, 0)),
        out_shape=jax.ShapeDtypeStruct((batch * seq, w), BF16),
        compiler_params=_cp(("arbitrary",), 32),
        name="attn_b_ctx",
    )(b_lambda, b_subln.reshape(1, HD), u_attn, u_attn, u_attn)


def _attn_b_lat_body(lam_ref, g_ref, q_ref, k_ref, v_ref, kc_ref, vc_ref, cq_ref, sq_ref, ck_ref, sk_ref,
                     o_ref, k_scr, kc_scr, v_scr, *, lam_init):
    hd = HD // 2
    scale = hd ** -0.5
    nq = pl.program_id(2)

    @pl.when(nq == 0)
    def _():
        k_scr[...] = _rope(k_ref[...], ck_ref[...], sk_ref[...]).T.astype(BF16)
        kc_scr[...] = kc_ref[...].T.astype(BF16)
        v_scr[...] = v_ref[...].astype(BF16)

    lam = _lambda(lam_ref, lam_init)
    k = k_scr[...]
    v = v_scr[...]
    kc = kc_scr[...]
    vc = vc_ref[...].astype(BF16)
    q = _rope(q_ref[...], cq_ref[...], sq_ref[...]) * (scale * LOG2E)
    lane = lax.broadcasted_iota(jnp.int32, q.shape, 1)
    outs = []
    for c in range(2):
        qc = jnp.where((lane >= hd) == (c == 1), q, 0.0).astype(BF16)
        s_ctx = _dot(qc, kc)
        s_lat = _dot(qc, k)
        m = jnp.maximum(jnp.max(s_ctx, axis=-1, keepdims=True), jnp.max(s_lat, axis=-1, keepdims=True))
        p_ctx = jnp.exp2(s_ctx - m)
        p_lat = jnp.exp2(s_lat - m)
        den = jnp.sum(p_ctx, axis=-1, keepdims=True) + jnp.sum(p_lat, axis=-1, keepdims=True)
        outs.append((_dot(p_ctx.astype(BF16), vc) + _dot(p_lat.astype(BF16), v)) / den)
    o = outs[0] - lam * outs[1]
    o_ref[...] = (_rms_lanes(o, g_ref[...]) * (1.0 - lam_init)).astype(BF16)


def attn_b_lat(u_attn, cache_k, cache_v, b_lambda, b_subln, rope_c, rope_s, lam_init, layer,
               batch, seq, row0, heads, col0, tq=256):
    qcol = col0 // HD
    kcol = qcol + heads
    vcol = kcol + heads
    nq = seq // tq
    past = cache_k.shape[2]
    rq0 = row0 // tq
    rs0 = row0 // seq
    return pl.pallas_call(
        functools.partial(_attn_b_lat_body, lam_init=lam_init),
        grid=(batch, heads, nq),
        in_specs=[pl.BlockSpec(b_lambda.shape, lambda b, h, n: (0, 0)),
                  pl.BlockSpec((1, HD), lambda b, h, n: (0, 0)),
                  pl.BlockSpec((tq, HD), lambda b, h, n: (rq0 + b * nq + n, qcol + h)),
                  pl.BlockSpec((seq, HD), lambda b, h, n: (rs0 + b, kcol + h)),
                  pl.BlockSpec((seq, HD), lambda b, h, n: (rs0 + b, vcol + h)),
                  pl.BlockSpec((None, None, past, HD), lambda b, h, n: (b, layer, 0, h)),
                  pl.BlockSpec((None, None, past, HD), lambda b, h, n: (b, layer, 0, h)),
                  pl.BlockSpec((tq, HD), lambda b, h, n: (n, 0)),
                  pl.BlockSpec((tq, HD), lambda b, h, n: (n, 0)),
                  pl.BlockSpec((seq, HD), lambda b, h, n: (0, 0)),
                  pl.BlockSpec((seq, HD), lambda b, h, n: (0, 0))],
        out_specs=pl.BlockSpec((tq, HD), lambda b, h, n: (b * nq + n, h)),
        out_shape=jax.ShapeDtypeStruct((batch * seq, heads * HD), BF16),
        scratch_shapes=[pltpu.VMEM((HD, seq), BF16), pltpu.VMEM((HD, past), BF16), pltpu.VMEM((seq, HD), BF16)],
        compiler_params=_cp(("arbitrary", "arbitrary", "arbitrary"), 40),
        name="attn_b_lat",
    )(b_lambda, b_subln.reshape(1, HD), u_attn, u_attn, u_attn, cache_k, cache_v,
      rope_c, rope_s, rope_c, rope_s)


def dft_half_tables(l):
    h = l // 2
    f = jnp.arange(h, dtype=jnp.int32)[:, None]
    j = jnp.arange(h, dtype=jnp.int32)[None, :]

    def table(pos):
        m = ((2 * f + 1) * pos) % (4 * l)
        ang = m.astype(F32) * (math.pi / (2 * l))
        t = jnp.concatenate([jnp.cos(ang), -jnp.sin(ang)], axis=0)
        hi = t.astype(BF16)
        return hi, (t - hi.astype(F32)).astype(BF16)

    (te, te_lo), (to, to_lo) = table(2 * j), table(2 * j + 1)
    return te, te_lo, to, to_lo, te.T, to.T


def parity_order(l):
    return jnp.concatenate([jnp.arange(0, l, 2), jnp.arange(1, l, 2)])


def hyena_feats(l):
    t = jnp.linspace(0.0, 1.0, l, dtype=F32)[:, None]
    bands = (HY_EMB - 1) // 2
    w = 2.0 * math.pi * jnp.arange(l, dtype=F32)[:, None] / l
    fr = jnp.linspace(1e-4, bands - 1, bands, dtype=F32)[None, :]
    feats = jnp.concatenate([t, jnp.cos(fr * w), -jnp.sin(fr * w)], axis=-1)
    return jnp.pad(feats, ((0, 0), (0, HD - HY_EMB)))


def hyena_decay(l, width):
    t = jnp.linspace(0.0, 1.0, l, dtype=F32)[:, None]
    deltas = jnp.abs(jnp.linspace(math.log(HY_TARGET) / HY_SLOW, math.log(HY_TARGET) / HY_FAST, width, dtype=F32))
    return jnp.exp(-t * deltas)


def _hy_time_body(feat_ref, dec_ref, w1_ref, b1_ref, w2_ref, b2_ref, w3_ref, fr_ref, kh_ref, kl_ref, nrm_scr,
                  *, width, l):
    ps = pl.program_id(2)
    c = pl.program_id(3)
    fr = fr_ref[...]
    h = jnp.sin(fr[0:1] * (_dot3(feat_ref[...], w1_ref[...]) + b1_ref[...]))
    h = jnp.sin(fr[1:2] * (_dot3(h, w2_ref[...]) + b2_ref[...]))
    dec = dec_ref[...]
    row = lax.broadcasted_iota(jnp.int32, dec.shape, 0)
    fwd = _dot3(h, w3_ref[:, :width]) * dec
    bwd = jnp.where((row == 0) & (c == 0), 0.0, _dot3(h, w3_ref[:, width:]) * dec)

    @pl.when((ps == 0) & (c == 0))
    def _():
        nrm_scr[...] = jnp.zeros_like(nrm_scr)

    @pl.when(ps == 0)
    def _():
        nrm_scr[...] += (jnp.sum(jnp.abs(fwd), axis=0, keepdims=True)
                         + jnp.sum(jnp.abs(bwd), axis=0, keepdims=True))

    @pl.when(ps == 1)
    def _():
        inv = 1.0 / ((nrm_scr[...] + EPS) * l)
        for part, val in ((0, fwd * inv), (1, bwd * inv)):
            hi, lo = _split(val)
            kh_ref[:, part * width:(part + 1) * width] = hi
            kl_ref[:, part * width:(part + 1) * width] = lo


def _hy_spec_body(ter_h, tei_h, tor_h, toi_h, ter_l, tei_l, tor_l, toi_l, keh_ref, kel_ref, koh_ref, kol_ref,
                   o_ref, *, width):
    keh, kel, koh, kol = keh_ref[...], kel_ref[...], koh_ref[...], kol_ref[...]

    def dot3(th, tl, kh, kl):
        t = th[...]
        return _dot(t, kh) + _dot(t, kl) + _dot(tl[...], kh)

    p_re, p_im = dot3(ter_h, ter_l, keh, kel), dot3(tei_h, tei_l, keh, kel)
    q_re, q_im = dot3(tor_h, tor_l, koh, kol), dot3(toi_h, toi_l, koh, kol)
    o_ref[0] = (p_re + q_re)[:, :width] + (p_re + q_re)[:, width:]
    o_ref[1] = (p_im + q_im)[:, :width] - (p_im + q_im)[:, width:]
    o_ref[2] = (p_re - q_re)[:, :width] + (p_re - q_re)[:, width:]
    o_ref[3] = (q_im - p_im)[:, :width] - (q_im - p_im)[:, width:]


def hyena_filters(l, hy_w1p, hy_b1, hy_w2, hy_b2, hy_w3, hy_freq, tabs, width, rc=512, tr=256):
    te, te_lo, to, to_lo = tabs[:4]
    depth = hy_w3.shape[0]
    ffn_w = hy_w2.shape[1]
    half = l // 2
    rc = min(rc, l)
    tr = min(tr, half)
    nr = half // tr
    order = parity_order(l)
    feats = hyena_feats(l)[order]
    dec = hyena_decay(l, width)[order]
    wspec = lambda shape: pl.BlockSpec((None,) + shape, lambda d, o, p, c: (d, 0, 0))
    kspec = pl.BlockSpec((None, rc, 2 * width), lambda d, o, p, c: (d, c * p, o))
    k_hi, k_lo = pl.pallas_call(
        functools.partial(_hy_time_body, width=width, l=l),
        grid=(depth, 2, 2, l // rc),
        in_specs=[pl.BlockSpec((rc, HD), lambda d, o, p, c: (c, 0)),
                  pl.BlockSpec((rc, width), lambda d, o, p, c: (c, 0)),
                  wspec((HD, ffn_w)), wspec((1, ffn_w)), wspec((ffn_w, ffn_w)), wspec((1, ffn_w)),
                  pl.BlockSpec((None, ffn_w, 2 * width), lambda d, o, p, c: (d, 0, o)),
                  wspec((2, ffn_w))],
        out_specs=[kspec, kspec],
        out_shape=[jax.ShapeDtypeStruct((depth, l, 4 * width), BF16)] * 2,
        scratch_shapes=[pltpu.VMEM((1, width), F32)],
        compiler_params=_cp(("arbitrary",) * 4, 32),
        name="hyena_time_filters",
    )(feats, dec, hy_w1p, hy_b1.reshape(depth, 1, ffn_w), hy_w2, hy_b2.reshape(depth, 1, ffn_w), hy_w3, hy_freq)
    re_spec = pl.BlockSpec((tr, half), lambda d, o, r: (r, 0))
    im_spec = pl.BlockSpec((tr, half), lambda d, o, r: (nr + r, 0))
    even = pl.BlockSpec((None, half, 2 * width), lambda d, o, r: (d, 0, o))
    odd = pl.BlockSpec((None, half, 2 * width), lambda d, o, r: (d, 1, o))
    return pl.pallas_call(
        functools.partial(_hy_spec_body, width=width),
        grid=(depth, 2, nr),
        in_specs=[re_spec, im_spec, re_spec, im_spec, re_spec, im_spec, re_spec, im_spec, even, even, odd, odd],
        out_specs=pl.BlockSpec((None, None, 4, tr, width), lambda d, o, r: (d, o, 0, r, 0)),
        out_shape=jax.ShapeDtypeStruct((depth, 2, 4, half, width), F32),
        compiler_params=_cp(("arbitrary",) * 3, 48),
        name="hyena_spectra",
    )(te, te, to, to, te_lo, te_lo, to_lo, to_lo, k_hi, k_lo, k_hi, k_lo)


def _short_conv(u, w, b):
    l = u.shape[0]
    row = lax.broadcasted_iota(jnp.int32, u.shape, 0)
    up = jnp.where(row == 0, 0.0, pltpu.roll(u, 1, 0))
    dn = jnp.where(row == l - 1, 0.0, pltpu.roll(u, l - 1, 0))
    return up * w[0:1] + u * w[1:2] + dn * w[2:3] + b


def _hyena_body(hv_ref, h1_ref, h2_ref, wv_ref, w1_ref, w2_ref, bv_ref, b1_ref, b2_ref, bias_ref,
                ter_ref, tei_ref, tor_ref, toi_ref, ger_ref, gei_ref, gor_ref, goi_ref, k_ref, o_ref,
                z_scr, ze_scr, zo_scr, acc_scr, *, nf):
    ph = pl.program_id(1)
    j = pl.program_id(2)
    half = ze_scr.shape[0]
    even = pl.ds(0, half, stride=2)
    odd = pl.ds(1, half, stride=2)
    tiles = [(t, slice(t * HD, (t + 1) * HD)) for t in range(z_scr.shape[0])]
    gather = lambda scr: jnp.concatenate([scr[t] for t, _ in tiles], axis=1)

    def set_input(z):
        for t, cols in tiles:
            z_scr[t] = z[:, cols]
            ze_scr[:, cols] = z_scr[t, even, :].astype(BF16)
            zo_scr[:, cols] = z_scr[t, odd, :].astype(BF16)

    @pl.when((ph == 0) & (j == 0))
    def _():
        set_input(_short_conv(hv_ref[...].astype(F32), wv_ref[...], bv_ref[...]))

    @pl.when(j == 0)
    def _():
        acc_scr[...] = jnp.zeros_like(acc_scr)

    ze, zo = ze_scr[...], zo_scr[...]
    p_re, p_im = _dot(ter_ref[...], ze), _dot(tei_ref[...], ze)
    q_re, q_im = _dot(tor_ref[...], zo), _dot(toi_ref[...], zo)
    z1_re, z1_im, z2_re, z2_im = p_re + q_re, p_im + q_im, p_re - q_re, q_im - p_im
    k1_re, k1_im, k2_re, k2_im = k_ref[0], k_ref[1], k_ref[2], k_ref[3]
    y1_re = z1_re * k1_re - z1_im * k1_im
    y1_im = z1_re * k1_im + z1_im * k1_re
    y2_re = z2_re * k2_re - z2_im * k2_im
    y2_im = z2_re * k2_im + z2_im * k2_re
    y_even = (_dot(ger_ref[...], (y1_re + y2_re).astype(BF16))
              + _dot(gei_ref[...], (y1_im - y2_im).astype(BF16)))
    y_odd = (_dot(gor_ref[...], (y1_re - y2_re).astype(BF16))
             + _dot(goi_ref[...], (y1_im + y2_im).astype(BF16)))
    for t, cols in tiles:
        acc_scr[t, even, :] += y_even[:, cols]
        acc_scr[t, odd, :] += y_odd[:, cols]

    @pl.when((j == nf - 1) & (ph == 0))
    def _():
        x1 = _short_conv(h1_ref[...].astype(F32), w1_ref[...], b1_ref[...])
        set_input(x1 * (gather(acc_scr) + gather(z_scr) * bias_ref[0:1]))

    @pl.when((j == nf - 1) & (ph == 1))
    def _():
        x2 = _short_conv(h2_ref[...].astype(F32), w2_ref[...], b2_ref[...])
        o_ref[...] = (x2 * (gather(acc_scr) + gather(z_scr) * bias_ref[1:2])).astype(BF16)


def hyena(u_hy, conv_w, conv_b, bias, kf, tabs, layer, batch, seq, row0, width, tfq=256):
    te, _, to, _, ge, go = tabs
    half = seq // 2
    tfq = min(tfq, half)
    nf = half // tfq
    rb = row0 // seq
    hspec = lambda part: pl.BlockSpec((seq, width), lambda b, p, j: (rb + b, part))
    wspec = lambda part: pl.BlockSpec((3, width), lambda b, p, j: (0, part))
    bspec = lambda part: pl.BlockSpec((1, width), lambda b, p, j: (0, part))
    t_re = pl.BlockSpec((tfq, half), lambda b, p, j: (j, 0))
    t_im = pl.BlockSpec((tfq, half), lambda b, p, j: (nf + j, 0))
    g_re = pl.BlockSpec((half, tfq), lambda b, p, j: (0, j))
    g_im = pl.BlockSpec((half, tfq), lambda b, p, j: (0, nf + j))
    return pl.pallas_call(
        functools.partial(_hyena_body, nf=nf),
        grid=(batch, 2, nf),
        in_specs=[hspec(0), hspec(1), hspec(2), wspec(0), wspec(1), wspec(2), bspec(0), bspec(1), bspec(2),
                  pl.BlockSpec((2, width), lambda b, p, j: (0, 0)),
                  t_re, t_im, t_re, t_im, g_re, g_im, g_re, g_im,
                  pl.BlockSpec((None, None, 4, tfq, width), lambda b, p, j: (layer, p, 0, j, 0))],
        out_specs=pl.BlockSpec((seq, width), lambda b, p, j: (b, 0)),
        out_shape=jax.ShapeDtypeStruct((batch * seq, width), BF16),
        scratch_shapes=[pltpu.VMEM((width // HD, seq, HD), F32), pltpu.VMEM((half, width), BF16),
                        pltpu.VMEM((half, width), BF16), pltpu.VMEM((width // HD, seq, HD), F32)],
        compiler_params=_cp(("arbitrary",) * 3, 56),
        name="hyena",
    )(u_hy, u_hy, u_hy, conv_w, conv_w, conv_w, conv_b, conv_b, conv_b, bias,
      te, te, to, to, ge, ge, go, go, kf)


def _hgrn_pair_level(t, s, fwd):
    x = t ^ s
    lvl = jnp.where(x == 0, 0, 32 - lax.clz(x))
    used = (s <= t) if fwd else (s >= t)
    return jnp.where(used, lvl, -1)


def _hgrn_midpoint(b3, blk, fwd):
    rows = b3.shape[1]
    r0 = blk // 2 - 1 if fwd else blk // 2
    ref = b3[:, r0:r0 + 1, :]
    if rows > blk:
        pos = lax.broadcasted_iota(jnp.int32, b3.shape, 1)
        for j in range(1, rows // blk):
            ref = jnp.where(pos >= j * blk, b3[:, j * blk + r0:j * blk + r0 + 1, :], ref)
    return ref


def _hgrn_chunk(q, k, g, v, st, tri, lvl, fwd):
    c = CHUNK
    gh, gl = _split(g)
    b = _dot(tri, gh) + _dot(tri, gl)
    tot = b[c - 1:c] if fwd else b[0:1]
    vb = v.astype(BF16)
    o = _dot_nt((q * jnp.exp2(b)).astype(BF16), st.astype(BF16))
    kd = (k * jnp.exp2(tot - b)).astype(BF16)
    st_new = st * jnp.exp2(tot) + _dot_tn(vb, kd)

    a = jnp.where(lvl == 0, jnp.sum(q * k, axis=-1, keepdims=True), 0.0)
    a = jnp.where(lvl == 1, _dot_nt((q * jnp.exp2(g)).astype(BF16), k.astype(BF16)), a)
    blk, m = 4, 2
    while blk <= c:
        rows = max(blk, SUB)
        b3, q3, k3 = (x.reshape(c // rows, rows, HD) for x in (b, q, k))
        e = jnp.exp2(-jnp.abs(b3 - _hgrn_midpoint(b3, blk, fwd)))
        qt = (q3 * e).reshape(c, HD).astype(BF16)
        kt = (k3 * e).reshape(c, HD).astype(BF16)
        a = jnp.where(lvl == m, _dot_nt(qt, kt), a)
        blk, m = blk * 2, m + 1
    return o + _dot(a.astype(BF16), vb), st_new


def _hgrn_gate(z, lb):
    e = jnp.exp(-jnp.abs(z))
    r = 1.0 / (1.0 + e)
    pos = z >= 0
    sig_pos = jnp.where(pos, r, e * r)
    sig_neg = jnp.where(pos, e * r, r)
    f = sig_pos + jnp.maximum(lb, LB_FLOOR) * sig_neg
    return jnp.log(f) * LOG2E, (1.0 - lb) * sig_neg


def _hgrn_lower_bound(lb_ref, layer, depth, direction):
    x = lb_ref[...]
    rows = [x[2 * i + direction:2 * i + direction + 1] for i in range(depth)]
    m = functools.reduce(jnp.maximum, rows)
    es = [jnp.exp(r - m) for r in rows]
    tot = functools.reduce(lambda a, b: a + b, es)
    cum = jnp.zeros_like(tot)
    for i in range(1, layer + 1):
        cum = cum + es[i]
    return jnp.maximum(cum / tot, 0.0)


def _hgrn_body(*refs, layer, depth, has_state, emit_state):
    it = iter(refs)
    lb_ref, ng_ref, q_ref, ff_ref, fb_ref, i_ref, g_ref = (next(it) for _ in range(7))
    s0_ref = next(it) if has_state else None
    o_ref = next(it)
    so_ref = next(it) if emit_state else None
    of_scr, ob_scr, st_scr, tri_scr, lvl_scr = (next(it) for _ in range(5))
    nc = q_ref.shape[0] // CHUNK
    hp = q_ref.shape[1] // HD
    ti = lax.broadcasted_iota(jnp.int32, (CHUNK, CHUNK), 0)
    si = lax.broadcasted_iota(jnp.int32, (CHUNK, CHUNK), 1)
    chains = [(h, d) for h in range(hp) for d in range(2)]
    lbs = {}
    for d in range(2):
        tri_scr[d] = jnp.where((si <= ti) if d == 0 else (si >= ti), 1.0, 0.0).astype(BF16)
        lvl_scr[d] = _hgrn_pair_level(ti, si, d == 0)
        lb_all = _hgrn_lower_bound(lb_ref, layer, depth, d)
        for h in range(hp):
            lbs[h, d] = lb_all[:, h * HD:(h + 1) * HD]
            st_scr[2 * h + d] = s0_ref[d, h].T if has_state else jnp.zeros((HD, HD), F32)

    def step(i, carry):
        for h, d in chains:
            fwd = d == 0
            ci = i if fwd else nc - 1 - i
            sl = pl.ds(pl.multiple_of(ci * CHUNK, CHUNK), CHUNK)
            cs = slice(h * HD, (h + 1) * HD)
            xq = q_ref[sl, cs].astype(F32)
            log_f, kk = _hgrn_gate((ff_ref if fwd else fb_ref)[sl, cs].astype(F32), lbs[h, d])
            o, st = _hgrn_chunk(xq * _sigmoid(xq), kk, log_f, i_ref[sl, cs].astype(F32), st_scr[2 * h + d],
                                tri_scr[d], lvl_scr[d], fwd)
            st_scr[2 * h + d] = st
            (of_scr if fwd else ob_scr)[sl, cs] = o
        return carry

    lax.fori_loop(0, nc, step, 0)
    for h, d in chains:
        if emit_state:
            so_ref[d, h] = st_scr[2 * h + d].T
    for h in range(hp):
        cs = slice(h * HD, (h + 1) * HD)
        xg = g_ref[:, cs].astype(F32)
        o_ref[:, cs] = (_rms_lanes(of_scr[:, cs] + ob_scr[:, cs], ng_ref[...]) * (xg * _sigmoid(xg))).astype(BF16)


def hgrn(u_hg, hg_lb2, hg_norm, state, layer, depth, batch, seq, row0, heads, emit_state, col0=0, hp=4):
    rb = row0 // seq
    ng = heads // hp
    w = hp * HD
    cb = col0 // w
    part = lambda p: pl.BlockSpec((seq, w), lambda b, h: (rb + b, cb + p * ng + h))
    in_specs = [pl.BlockSpec((2 * depth, w), lambda b, h: (0, h)),
                pl.BlockSpec((1, HD), lambda b, h: (0, 0)),
                part(0), part(1), part(2), part(3), part(4)]
    args = [hg_lb2, hg_norm.reshape(1, HD), u_hg, u_hg, u_hg, u_hg, u_hg]
    if state is not None:
        in_specs.append(pl.BlockSpec((None, None, 2, hp, HD, HD), lambda b, h: (b, layer, 0, h, 0, 0)))
        args.append(state)
    out_specs = [pl.BlockSpec((seq, w), lambda b, h: (b, h))]
    out_shape = [jax.ShapeDtypeStruct((batch * seq, heads * HD), BF16)]
    if emit_state:
        out_specs.append(pl.BlockSpec((None, 2, hp, HD, HD), lambda b, h: (b, 0, h, 0, 0)))
        out_shape.append(jax.ShapeDtypeStruct((batch, 2, heads, HD, HD), F32))
    return pl.pallas_call(
        functools.partial(_hgrn_body, layer=layer, depth=depth, has_state=state is not None,
                          emit_state=emit_state),
        grid=(batch, ng),
        in_specs=in_specs,
        out_specs=out_specs,
        out_shape=out_shape,
        scratch_shapes=[pltpu.VMEM((seq, w), F32), pltpu.VMEM((seq, w), F32),
                        pltpu.VMEM((2 * hp, HD, HD), F32),
                        pltpu.VMEM((2, CHUNK, CHUNK), BF16), pltpu.VMEM((2, CHUNK, CHUNK), jnp.int32)],
        compiler_params=_cp(("arbitrary", "arbitrary"), 48),
        name="hgrn",
    )(*args)


def _merge_body(*refs, n_ctx_tiles):
    o_refs, (gt_ref, x_ref, mod_ref, wb_ref, wo_ref, out_ref) = refs[:2 * N_BRANCH], refs[2 * N_BRANCH:]
    d = x_ref.shape[1]
    is_ctx = pl.program_id(0) < n_ctx_tiles
    acc = None
    for n in range(N_BRANCH):
        o = jnp.where(is_ctx, o_refs[2 * n][...], o_refs[2 * n + 1][...])
        y = gt_ref[:, n * d:(n + 1) * d].astype(F32) * _dot(o, wb_ref[n])
        acc = y if acc is None else acc + y
    out = _dot(acc.astype(BF16), wo_ref[...])
    out_ref[...] = x_ref[...] + mod_ref[...][5:6] * out


def merge(branch_outs, gates, x, mod_l, w_branch, w_out, t_ctx, l_lat, tm=256):
    t, d = x.shape
    mw = w_branch.shape[1]
    nct = t_ctx // tm
    row = functools.partial(_mod_row, tm=tm, t_ctx=t_ctx, l_lat=l_lat)
    cspec = pl.BlockSpec((tm, mw), lambda i: (jnp.minimum(i, nct - 1), 0))
    lspec = pl.BlockSpec((tm, mw), lambda i: (jnp.maximum(i - nct, 0), 0))
    return pl.pallas_call(
        functools.partial(_merge_body, n_ctx_tiles=nct),
        grid=(t // tm,),
        in_specs=[cspec, lspec] * N_BRANCH + [
                  pl.BlockSpec((tm, N_BRANCH * d), lambda i: (i, 0)),
                  pl.BlockSpec((tm, d), lambda i: (i, 0)),
                  pl.BlockSpec((None, N_MOD, d), lambda i: (row(i), 0, 0)),
                  pl.BlockSpec((N_BRANCH, mw, d), lambda i: (0, 0, 0), pipeline_mode=pl.Buffered(1)),
                  pl.BlockSpec((d, d), lambda i: (0, 0), pipeline_mode=pl.Buffered(1))],
        out_specs=pl.BlockSpec((tm, d), lambda i: (i, 0)),
        out_shape=jax.ShapeDtypeStruct((t, d), F32),
        compiler_params=_cp(("arbitrary",), 56),
        name="merge",
    )(*[o for pair in branch_outs for o in pair], gates, x, mod_l, w_branch, w_out)


def _final_norm_body(x_ref, g_ref, oc_ref, ol_ref, *, n_ctx_tiles):
    i = pl.program_id(0)

    @pl.when(i < n_ctx_tiles)
    def _():
        oc_ref[...] = _rms_lanes(x_ref[...], g_ref[...])

    @pl.when(i >= n_ctx_tiles)
    def _():
        ol_ref[...] = _rms_lanes(x_ref[...], g_ref[...])


def final_norm(x, g, t_ctx, tm=512):
    t, d = x.shape
    nct = t_ctx // tm
    return pl.pallas_call(
        functools.partial(_final_norm_body, n_ctx_tiles=nct),
        grid=(t // tm,),
        in_specs=[pl.BlockSpec((tm, d), lambda i: (i, 0)), pl.BlockSpec((1, d), lambda i: (0, 0))],
        out_specs=[pl.BlockSpec((tm, d), lambda i: (jnp.minimum(i, nct - 1), 0)),
                   pl.BlockSpec((tm, d), lambda i: (jnp.maximum(i - nct, 0), 0))],
        out_shape=[jax.ShapeDtypeStruct((t_ctx, d), F32), jax.ShapeDtypeStruct((t - t_ctx, d), F32)],
        compiler_params=_cp(("arbitrary",), 32),
        name="final_norm",
    )(x, g.reshape(1, d))


def kernel(x_prompt, x_sample, c, cache_a_k, cache_a_v, cache_b_k, cache_b_v, state_hgrn, c_ctx, w_mod, b_mod, norm_g, w_ffn1_gu, w_ffn1_down, w_ffn2_gu, w_ffn2_down, w_in, w_branch, w_out, a_sink, b_lambda, b_subln, hy_conv_w, hy_conv_b, hy_w1, hy_b1, hy_w2, hy_b2, hy_w3, hy_freq, hy_bias, hg_lb, hg_norm, final_g):
    batch, seq, d = x_prompt.shape
    dec_batch, dec_seq, _ = x_sample.shape
    depth = w_mod.shape[0]
    mix_w = w_branch.shape[2]
    a_heads = a_sink.shape[1]
    a_kvh = cache_a_k.shape[3]
    a_group = a_heads // a_kvh
    b_heads = cache_b_k.shape[3]
    hg_heads = state_hgrn.shape[3]
    t_ctx, t_lat = batch * seq, dec_batch * dec_seq

    n_attn = (a_heads + 2 * a_kvh + 3 * b_heads) * HD
    b_col0 = (a_heads + 2 * a_kvh) * HD

    x = jnp.concatenate([x_prompt.reshape(t_ctx, d), x_sample.reshape(t_lat, d)], axis=0)
    cond = jnp.concatenate([c_ctx[None, :], c], axis=0)
    cond = jnp.pad(cond, ((0, MOD_ROWS - cond.shape[0]), (0, 0)))
    mod = modulation(cond, w_mod, b_mod).reshape(depth, MOD_ROWS, N_MOD, d)

    rope_a = rope_tables(dec_seq, HD, 1)
    rope_b = rope_tables(dec_seq, HD // 2, 2)
    tabs_ctx = dft_half_tables(seq)
    tabs_lat = dft_half_tables(dec_seq)
    w1p = jnp.pad(hy_w1, ((0, 0), (0, HD - hy_w1.shape[1]), (0, 0)))
    kf_ctx = hyena_filters(seq, w1p, hy_b1, hy_w2, hy_b2, hy_w3, hy_freq, tabs_ctx, mix_w)
    kf_lat = hyena_filters(dec_seq, w1p, hy_b1, hy_w2, hy_b2, hy_w3, hy_freq, tabs_lat, mix_w)

    cak = cache_a_k.reshape(dec_batch, depth, -1, a_kvh * HD)
    cav = cache_a_v.reshape(dec_batch, depth, -1, a_kvh * HD)
    cbk = cache_b_k.reshape(dec_batch, depth, -1, b_heads * HD)
    cbv = cache_b_v.reshape(dec_batch, depth, -1, b_heads * HD)
    hg_lb2 = hg_lb.reshape(depth * 2, mix_w)

    ak_l, av_l, bk_l, bv_l, st_l = [], [], [], [], []
    for l in range(depth):
        lam_init = 0.8 - 0.6 * math.exp(-0.3 * l)
        mod_l = mod[l]
        x, h_mix = ffn(x, mod_l, norm_g[l, 0:1], col_tiles(w_ffn1_gu[l], FFN_TILE), w_ffn1_down[l].astype(BF16), 0,
                       t_ctx, dec_seq, next_g_row=norm_g[l, 1:2])
        n_rec = n_attn + 8 * mix_w
        w_in_t = col_tiles(w_in[l], PROJ_TILE)
        ja, jr = n_attn // PROJ_TILE, n_rec // PROJ_TILE
        u_attn = proj(h_mix, w_in_t[:ja], F32, False)
        u_rec = proj(h_mix, w_in_t[ja:jr], BF16, False)
        gates = proj(h_mix, w_in_t[jr:], BF16, True)

        uc = u_attn[:t_ctx]
        k0 = a_heads * HD
        ak_l.append(uc[:, k0:k0 + a_kvh * HD].reshape(batch, seq, a_kvh, HD))
        av_l.append(uc[:, k0 + a_kvh * HD:b_col0].reshape(batch, seq, a_kvh, HD))
        k1 = b_col0 + b_heads * HD
        bk_l.append(uc[:, k1:k1 + b_heads * HD].reshape(batch, seq, b_heads, 2, HD // 2))
        bv_l.append(uc[:, k1 + b_heads * HD:].reshape(batch, seq, b_heads, HD))

        oa_c = attn_a_ctx(u_attn, a_sink[l], batch, seq, a_kvh, a_group)
        oa_s = attn_a_lat(u_attn, cak, cav, a_sink[l], rope_a[0], rope_a[1], l, dec_batch, dec_seq, t_ctx,
                          a_kvh, a_group)
        ob_c = attn_b_ctx(u_attn, b_lambda[l], b_subln[l], lam_init, batch, seq, b_heads, b_col0)
        ob_s = attn_b_lat(u_attn, cbk, cbv, b_lambda[l], b_subln[l], rope_b[0], rope_b[1], lam_init, l,
                          dec_batch, dec_seq, t_ctx, b_heads, b_col0)
        cb = hy_conv_b[l].reshape(1, -1)
        oc_c = hyena(u_rec, hy_conv_w[l], cb, hy_bias[l], kf_ctx, tabs_ctx, l, batch, seq, 0, mix_w)
        oc_s = hyena(u_rec, hy_conv_w[l], cb, hy_bias[l], kf_lat, tabs_lat, l, dec_batch, dec_seq, t_ctx, mix_w)
        od_c, st = hgrn(u_rec, hg_lb2, hg_norm[l], None, l, depth, batch, seq, 0, hg_heads, True, col0=3 * mix_w)
        od_s, = hgrn(u_rec, hg_lb2, hg_norm[l], state_hgrn, l, depth, dec_batch, dec_seq, t_ctx, hg_heads, False,
                     col0=3 * mix_w)
        st_l.append(st)

        x = merge(((oa_c, oa_s), (ob_c, ob_s), (oc_c, oc_s), (od_c, od_s)), gates, x, mod_l,
                  w_branch[l].astype(BF16), w_out[l].astype(BF16), t_ctx, dec_seq)
        x = ffn(x, mod_l, norm_g[l, 2:3], col_tiles(w_ffn2_gu[l], FFN_TILE), w_ffn2_down[l].astype(BF16), 2,
                t_ctx, dec_seq)

    y_ctx, y_lat = final_norm(x, final_g, t_ctx)
    return (y_ctx.reshape(batch, seq, d), y_lat.reshape(dec_batch, dec_seq, d),
            jnp.stack(ak_l, axis=1), jnp.stack(av_l, axis=1), jnp.stack(bk_l, axis=1), jnp.stack(bv_l, axis=1),
            jnp.stack(st_l, axis=1))
```

```python
import functools
import math

import jax
import jax.numpy as jnp
import numpy as np
from jax import lax
from jax.experimental import pallas as pl
from jax.experimental.pallas import tpu as pltpu

F32 = jnp.float32
BF16 = jnp.bfloat16

EPS = 1e-6
NEG = -1e30
LB_FLOOR = 1e-30
ROPE_BASE = 10000.0
GRID_W = 64
N_MOD = 9
N_BRANCH = 4
HD = 128
A_WINDOW = 128
HY_EMB = 33
HY_TARGET, HY_FAST, HY_SLOW = 1e-2, 0.3, 1.5
CHUNK = 128
SUB = 8
LOG2E = 1.4426950408889634
MOD_ROWS = 16
FFN_TILE = 512
PROJ_TILE = 512
VMEM_MB = 2 ** 20


def _cp(sem, vmem_mb):
    return pltpu.CompilerParams(dimension_semantics=sem, vmem_limit_bytes=vmem_mb * VMEM_MB)


def _dot(a, b):
    return jnp.dot(a, b, preferred_element_type=F32)


def _dot_nt(a, b):
    return lax.dot_general(a, b, (((1,), (1,)), ((), ())), preferred_element_type=F32)


def _dot_tn(a, b):
    return lax.dot_general(a, b, (((0,), (0,)), ((), ())), preferred_element_type=F32)


def _split(a):
    hi = a.astype(BF16)
    lo = (a - hi.astype(F32)).astype(BF16)
    return hi, lo


def _dot3(a, b):
    ah, al = _split(a)
    bh, bl = _split(b)
    return _dot(ah, bh) + _dot(ah, bl) + _dot(al, bh)


def _sigmoid(x):
    return 0.5 * jnp.tanh(0.5 * x) + 0.5


def _norm_mod(x, g, sc, sh):
    y = x * lax.rsqrt(jnp.mean(x * x, axis=-1, keepdims=True) + EPS)
    return (y * g) * (1.0 + sc) + sh


def _norm_mod_cols(load, keep, reload, store, shape, g, sc, sh):
    n, d = shape
    tiles = [slice(t, t + HD) for t in range(0, d, HD)]
    ssq = jnp.zeros((n, HD), F32)
    for cols in tiles:
        x = load(cols)
        if keep is not None:
            keep(cols, x)
        ssq = ssq + x * x
    r = lax.rsqrt(jnp.sum(ssq, axis=-1, keepdims=True) * (1.0 / d) + EPS)
    gs = g * (1.0 + sc)
    for cols in tiles:
        store(cols, (reload(cols) * r) * gs[:, cols] + sh[:, cols])


def _rms_lanes(x, g):
    return x * lax.rsqrt(jnp.mean(x * x, axis=-1, keepdims=True) + EPS) * g


def _mod_body(c_ref, w_ref, b_ref, o_ref):
    c = c_ref[...]
    a = (c * _sigmoid(c)).astype(BF16)
    o_ref[...] = _dot(a, w_ref[...].astype(BF16)) + b_ref[...]


def modulation(cond, w_mod, b_mod, tn=1024):
    depth, d, n = w_mod.shape
    return pl.pallas_call(
        _mod_body,
        grid=(depth, n // tn),
        in_specs=[pl.BlockSpec((MOD_ROWS, d), lambda l, j: (0, 0)),
                  pl.BlockSpec((None, d, tn), lambda l, j: (l, 0, j)),
                  pl.BlockSpec((None, 1, tn), lambda l, j: (l, 0, j))],
        out_specs=pl.BlockSpec((None, MOD_ROWS, tn), lambda l, j: (l, 0, j)),
        out_shape=jax.ShapeDtypeStruct((depth, MOD_ROWS, n), F32),
        compiler_params=_cp(("arbitrary", "arbitrary"), 40),
        name="modulation",
    )(cond, w_mod, b_mod.reshape(depth, 1, n))


def _mod_row(i, tm, t_ctx, l_lat):
    start = i * tm
    return jnp.where(start < t_ctx, 0, 1 + (start - t_ctx) // l_lat)


def _ffn_body(*refs, sub, nj, next_sub):
    if next_sub is None:
        x_ref, mod_ref, g_ref, wg_ref, wu_ref, wd_ref, o_ref, h_scr = refs
    else:
        x_ref, mod_ref, g_ref, gn_ref, wg_ref, wu_ref, wd_ref, o_ref, hn_ref, h_scr = refs
    j = pl.program_id(1)

    @pl.when(j == 0)
    def _():
        m = mod_ref[...]
        load = lambda cols: x_ref[:, cols]

        def store(cols, h):
            h_scr[:, cols] = h.astype(BF16)

        _norm_mod_cols(load, None, load, store, x_ref.shape, g_ref[...],
                       m[3 * sub + 1:3 * sub + 2], m[3 * sub:3 * sub + 1])
        o_ref[...] = jnp.zeros_like(o_ref)

    h = h_scr[...]
    tf = wg_ref.shape[1]
    hc = tf // 2
    halves = [slice(c0, c0 + hc) for c0 in range(0, tf, hc)]
    ab = [(_dot(h, wg_ref[:, cs]), _dot(h, wu_ref[:, cs])) for cs in halves]
    acts = [(a * _sigmoid(a) * b).astype(BF16) for a, b in ab]
    o_ref[...] += functools.reduce(lambda p, q: p + q, [_dot(act, wd_ref[cs, :]) for act, cs in zip(acts, halves)])

    @pl.when(j == nj - 1)
    def _():
        m = mod_ref[...]
        ga = 0.5 * m[3 * sub + 2:3 * sub + 3]
        if next_sub is None:
            o_ref[...] = x_ref[...] + ga * o_ref[...]
        else:
            def keep(cols, y):
                o_ref[:, cols] = y

            def store(cols, h):
                hn_ref[:, cols] = h.astype(BF16)

            _norm_mod_cols(lambda cols: x_ref[:, cols] + ga[:, cols] * o_ref[:, cols], keep,
                           lambda cols: o_ref[:, cols], store, x_ref.shape, gn_ref[...],
                           m[3 * next_sub + 1:3 * next_sub + 2], m[3 * next_sub:3 * next_sub + 1])


def ffn(x, mod_l, g_row, w_gu, w_dn, sub, t_ctx, l_lat, next_g_row=None, tm=512, tf=FFN_TILE):
    t, d = x.shape
    dff = w_dn.shape[0]
    nj = dff // tf
    emit = next_g_row is not None
    row = functools.partial(_mod_row, tm=tm, t_ctx=t_ctx, l_lat=l_lat)
    gspec = pl.BlockSpec((1, d), lambda i, j: (0, 0))
    xspec = pl.BlockSpec((tm, d), lambda i, j: (i, 0))
    outs = pl.pallas_call(
        functools.partial(_ffn_body, sub=sub, nj=nj, next_sub=sub + 1 if emit else None),
        grid=(t // tm, nj),
        in_specs=[xspec,
                  pl.BlockSpec((None, N_MOD, d), lambda i, j: (row(i), 0, 0)),
                  gspec] + ([gspec] if emit else []) + [
                  pl.BlockSpec((d, tf), lambda i, j: (0, j)),
                  pl.BlockSpec((d, tf), lambda i, j: (0, j + nj)),
                  pl.BlockSpec((tf, d), lambda i, j: (j, 0))],
        out_specs=[xspec] + ([xspec] if emit else []),
        out_shape=[jax.ShapeDtypeStruct((t, d), F32)] + ([jax.ShapeDtypeStruct((t, d), BF16)] if emit else []),
        scratch_shapes=[pltpu.VMEM((tm, d), BF16)],
        compiler_params=_cp(("arbitrary", "arbitrary"), 56),
        name="ffn",
    )(x, mod_l, g_row, *([next_g_row] if emit else []), w_gu, w_gu, w_dn)
    return outs if emit else outs[0]


def _proj_body(h_ref, w_ref, o_ref, *, gate):
    r = _dot(h_ref[...], w_ref[...])
    o_ref[...] = (_sigmoid(r) if gate else r).astype(o_ref.dtype)


def proj(h, w, out_dtype, gate, tm=2048, tn=PROJ_TILE):
    t, d = h.shape
    n = w.shape[1]
    while t % tm:
        tm //= 2
    return pl.pallas_call(
        functools.partial(_proj_body, gate=gate),
        grid=(t // tm, n // tn),
        in_specs=[pl.BlockSpec((tm, d), lambda i, j: (i, 0)),
                  pl.BlockSpec((d, tn), lambda i, j: (0, j))],
        out_specs=pl.BlockSpec((tm, tn), lambda i, j: (i, j)),
        out_shape=jax.ShapeDtypeStruct((t, n), out_dtype),
        compiler_params=_cp(("arbitrary", "arbitrary"), 56),
        name="proj",
    )(h, w)


def rope_tables(l, rot_dim, reps):
    n_rows = l // GRID_W
    rows = jnp.broadcast_to(jnp.arange(n_rows, dtype=F32)[:, None], (n_rows, GRID_W)).reshape(-1)
    cols = jnp.broadcast_to(jnp.arange(GRID_W, dtype=F32)[None, :], (n_rows, GRID_W)).reshape(-1)
    axis_dim = rot_dim // 2
    inv = ROPE_BASE ** (-jnp.arange(0, axis_dim, 2, dtype=F32) / axis_dim)
    ang = jnp.concatenate([rows[:, None] * inv, cols[:, None] * inv], axis=-1)
    cos = jnp.repeat(jnp.cos(ang), 2, axis=-1)
    sin = jnp.repeat(jnp.sin(ang), 2, axis=-1)
    sign = jnp.tile(jnp.array([-1.0, 1.0], F32), rot_dim // 2)
    return jnp.tile(cos, (1, reps)), jnp.tile(sin * sign, (1, reps))


def _rope(x, c, s):
    lane = lax.broadcasted_iota(jnp.int32, x.shape, 1)
    nxt = pltpu.roll(x, x.shape[1] - 1, 1)
    prv = pltpu.roll(x, 1, 1)
    return x * c + jnp.where((lane & 1) == 0, nxt, prv) * s


def _attn_a_ctx_body(sink_ref, q_ref, k_ref, v_ref, o_ref, *, group, kvh):
    scale = HD ** -0.5
    for kh in range(kvh):
        k = k_ref[:, kh * HD:(kh + 1) * HD].astype(BF16)
        v = v_ref[:, kh * HD:(kh + 1) * HD].astype(BF16)
        for g in range(group):
            hs = slice((kh * group + g) * HD, (kh * group + g + 1) * HD)
            q = q_ref[:, hs].astype(BF16)
            s = _dot_nt(q, k) * scale
            sink = sink_ref[kh * group + g]
            m = jnp.maximum(jnp.max(s, axis=-1, keepdims=True), sink)
            p = jnp.exp(s - m)
            den = jnp.sum(p, axis=-1, keepdims=True) + jnp.exp(sink - m)
            o_ref[:, hs] = (_dot(p.astype(BF16), v) / den).astype(BF16)


def attn_a_ctx(u_attn, sink, batch, seq, kvh, group):
    qw = kvh * group * HD
    kw = kvh * HD
    kcol = qw // kw
    return pl.pallas_call(
        functools.partial(_attn_a_ctx_body, group=group, kvh=kvh),
        grid=(batch,),
        in_specs=[pl.BlockSpec(memory_space=pltpu.SMEM),
                  pl.BlockSpec((seq, qw), lambda b: (b, 0)),
                  pl.BlockSpec((seq, kw), lambda b: (b, kcol)),
                  pl.BlockSpec((seq, kw), lambda b: (b, kcol + 1))],
        out_specs=pl.BlockSpec((seq, qw), lambda b: (b, 0)),
        out_shape=jax.ShapeDtypeStruct((batch * seq, qw), BF16),
        compiler_params=_cp(("arbitrary",), 32),
        name="attn_a_ctx",
    )(sink, u_attn, u_attn, u_attn)


def _attn_a_lat_body(sink_ref, q_ref, k0_ref, k1_ref, k2_ref, v0_ref, v1_ref, v2_ref, kc_ref, vc_ref,
                     cq_ref, sq_ref, c0_ref, s0_ref, c2_ref, s2_ref, o_ref, *, group, kvh, seq):
    n = pl.program_id(1)
    blk = q_ref.shape[0]
    scale = HD ** -0.5
    cq, sq = cq_ref[...], sq_ref[...]
    qi = lax.broadcasted_iota(jnp.int32, (blk, 3 * blk), 0)
    kj = lax.broadcasted_iota(jnp.int32, (blk, 3 * blk), 1)
    kpos = (n - 1) * blk + kj
    qpos = n * blk + qi
    valid = (kpos >= 0) & (kpos < seq) & (jnp.abs(qpos - kpos) <= A_WINDOW)
    for kh in range(kvh):
        ks = slice(kh * HD, (kh + 1) * HD)
        kw = jnp.concatenate([_rope(k0_ref[:, ks], c0_ref[...], s0_ref[...]),
                              _rope(k1_ref[:, ks], cq, sq),
                              _rope(k2_ref[:, ks], c2_ref[...], s2_ref[...])], axis=0).astype(BF16)
        vw = jnp.concatenate([v0_ref[:, ks], v1_ref[:, ks], v2_ref[:, ks]], axis=0).astype(BF16)
        kc = kc_ref[:, ks].astype(BF16)
        vc = vc_ref[:, ks].astype(BF16)
        for g in range(group):
            hs = slice((kh * group + g) * HD, (kh * group + g + 1) * HD)
            q = _rope(q_ref[:, hs], cq, sq).astype(BF16)
            s_loc = jnp.where(valid, _dot_nt(q, kw) * scale, NEG)
            s_ctx = _dot_nt(q, kc) * scale
            sink = sink_ref[kh * group + g]
            m = jnp.maximum(jnp.maximum(jnp.max(s_loc, axis=-1, keepdims=True),
                                        jnp.max(s_ctx, axis=-1, keepdims=True)), sink)
            p_loc = jnp.exp(s_loc - m)
            p_ctx = jnp.exp(s_ctx - m)
            den = (jnp.sum(p_loc, axis=-1, keepdims=True) + jnp.sum(p_ctx, axis=-1, keepdims=True)
                   + jnp.exp(sink - m))
            o = _dot(p_ctx.astype(BF16), vc) + _dot(p_loc.astype(BF16), vw)
            o_ref[:, hs] = (o / den).astype(BF16)


def attn_a_lat(u_attn, cache_k, cache_v, sink, rope_c, rope_s, layer, batch, seq, row0, kvh, group, blk=256):
    nb = seq // blk
    rb0 = row0 // blk
    qw = kvh * group * HD
    kw = kvh * HD
    kcol = qw // kw
    vcol = kcol + 1
    past = cache_k.shape[2]

    def rows(b, n):
        return rb0 + b * nb + n

    prev = lambda n: jnp.maximum(n - 1, 0)
    nxt = lambda n: jnp.minimum(n + 1, nb - 1)
    tab = lambda f: pl.BlockSpec((blk, HD), lambda b, n: (f(n), 0))
    kv = lambda f, col: pl.BlockSpec((blk, kw), lambda b, n: (rows(b, f(n)), col))
    same = lambda n: n
    return pl.pallas_call(
        functools.partial(_attn_a_lat_body, group=group, kvh=kvh, seq=seq),
        grid=(batch, nb),
        in_specs=[pl.BlockSpec(memory_space=pltpu.SMEM),
                  pl.BlockSpec((blk, qw), lambda b, n: (rows(b, n), 0)),
                  kv(prev, kcol), kv(same, kcol), kv(nxt, kcol),
                  kv(prev, vcol), kv(same, vcol), kv(nxt, vcol),
                  pl.BlockSpec((None, None, past, kw), lambda b, n: (b, layer, 0, 0)),
                  pl.BlockSpec((None, None, past, kw), lambda b, n: (b, layer, 0, 0)),
                  tab(same), tab(same), tab(prev), tab(prev), tab(nxt), tab(nxt)],
        out_specs=pl.BlockSpec((blk, qw), lambda b, n: (b * nb + n, 0)),
        out_shape=jax.ShapeDtypeStruct((batch * seq, qw), BF16),
        compiler_params=_cp(("arbitrary", "arbitrary"), 40),
        name="attn_a_lat",
    )(sink, u_attn, u_attn, u_attn, u_attn, u_attn, u_attn, u_attn, cache_k, cache_v,
      rope_c, rope_s, rope_c, rope_s, rope_c, rope_s)


def _lambda(lam_ref, lam_init):
    lw = lam_ref[...]
    return (jnp.exp(jnp.sum(lw[0:1] * lw[1:2], axis=-1, keepdims=True))
            - jnp.exp(jnp.sum(lw[2:3] * lw[3:4], axis=-1, keepdims=True)) + lam_init)


def _softmax_parts(parts):
    m = functools.reduce(jnp.maximum, [jnp.max(s, axis=-1, keepdims=True) for s in parts])
    ps = [jnp.exp(s - m) for s in parts]
    inv = 1.0 / functools.reduce(lambda a, b: a + b, [jnp.sum(p, axis=-1, keepdims=True) for p in ps])
    return [p * inv for p in ps]


def _attn_b_ctx_body(lam_ref, g_ref, q_ref, k_ref, v_ref, o_ref, *, lam_init):
    hd = HD // 2
    scale = hd ** -0.5
    lam = _lambda(lam_ref, lam_init)
    for h in range(q_ref.shape[1] // HD):
        hs = slice(h * HD, (h + 1) * HD)
        q = q_ref[:, hs].astype(BF16)
        k = k_ref[:, hs].astype(BF16)
        p0, = _softmax_parts([_dot_nt(q[:, :hd], k[:, :hd]) * scale])
        p1, = _softmax_parts([_dot_nt(q[:, hd:], k[:, hd:]) * scale])
        o = _dot((p0 - lam * p1).astype(BF16), v_ref[:, hs].astype(BF16))
        o_ref[:, hs] = (_rms_lanes(o, g_ref[...]) * (1.0 - lam_init)).astype(BF16)


def attn_b_ctx(u_attn, b_lambda, b_subln, lam_init, batch, seq, heads, col0):
    w = heads * HD
    qcol = col0 // w
    return pl.pallas_call(
        functools.partial(_attn_b_ctx_body, lam_init=lam_init),
        grid=(batch,),
        in_specs=[pl.BlockSpec(b_lambda.shape, lambda b: (0, 0)),
                  pl.BlockSpec((1, HD), lambda b: (0, 0)),
                  pl.BlockSpec((seq, w), lambda b: (b, qcol)),
                  pl.BlockSpec((seq, w), lambda b: (b, qcol + 1)),
                  pl.BlockSpec((seq, w), lambda b: (b, qcol + 2))],
        out_specs=pl.BlockSpec((seq, w), lambda b: (b, 0)),
        out_shape=jax.ShapeDtypeStruct((batch * seq, w), BF16),
        compiler_params=_cp(("arbitrary",), 32),
        name="attn_b_ctx",
    )(b_lambda, b_subln.reshape(1, HD), u_attn, u_attn, u_attn)


def _attn_b_lat_body(lam_ref, g_ref, q_ref, k_ref, v_ref, kc_ref, vc_ref, cq_ref, sq_ref, ck_ref, sk_ref,
                     o_ref, k_scr, kc_scr, v_scr, *, lam_init):
    hd = HD // 2
    scale = hd ** -0.5
    nq = pl.program_id(2)

    @pl.when(nq == 0)
    def _():
        k_scr[...] = _rope(k_ref[...], ck_ref[...], sk_ref[...]).T.astype(BF16)
        kc_scr[...] = kc_ref[...].T.astype(BF16)
        v_scr[...] = v_ref[...].astype(BF16)

    lam = _lambda(lam_ref, lam_init)
    k = k_scr[...]
    v = v_scr[...]
    kc = kc_scr[...]
    vc = vc_ref[...].astype(BF16)
    q = _rope(q_ref[...], cq_ref[...], sq_ref[...]) * (scale * LOG2E)
    lane = lax.broadcasted_iota(jnp.int32, q.shape, 1)
    outs = []
    for c in range(2):
        qc = jnp.where((lane >= hd) == (c == 1), q, 0.0).astype(BF16)
        s_ctx = _dot(qc, kc)
        s_lat = _dot(qc, k)
        m = jnp.maximum(jnp.max(s_ctx, axis=-1, keepdims=True), jnp.max(s_lat, axis=-1, keepdims=True))
        p_ctx = jnp.exp2(s_ctx - m)
        p_lat = jnp.exp2(s_lat - m)
        den = jnp.sum(p_ctx, axis=-1, keepdims=True) + jnp.sum(p_lat, axis=-1, keepdims=True)
        outs.append((_dot(p_ctx.astype(BF16), vc) + _dot(p_lat.astype(BF16), v)) / den)
    o = outs[0] - lam * outs[1]
    o_ref[...] = (_rms_lanes(o, g_ref[...]) * (1.0 - lam_init)).astype(BF16)


def attn_b_lat(u_attn, cache_k, cache_v, b_lambda, b_subln, rope_c, rope_s, lam_init, layer,
               batch, seq, row0, heads, col0, tq=256):
    qcol = col0 // HD
    kcol = qcol + heads
    vcol = kcol + heads
    nq = seq // tq
    past = cache_k.shape[2]
    rq0 = row0 // tq
    rs0 = row0 // seq
    return pl.pallas_call(
        functools.partial(_attn_b_lat_body, lam_init=lam_init),
        grid=(batch, heads, nq),
        in_specs=[pl.BlockSpec(b_lambda.shape, lambda b, h, n: (0, 0)),
                  pl.BlockSpec((1, HD), lambda b, h, n: (0, 0)),
                  pl.BlockSpec((tq, HD), lambda b, h, n: (rq0 + b * nq + n, qcol + h)),
                  pl.BlockSpec((seq, HD), lambda b, h, n: (rs0 + b, kcol + h)),
                  pl.BlockSpec((seq, HD), lambda b, h, n: (rs0 + b, vcol + h)),
                  pl.BlockSpec((None, None, past, HD), lambda b, h, n: (b, layer, 0, h)),
                  pl.BlockSpec((None, None, past, HD), lambda b, h, n: (b, layer, 0, h)),
                  pl.BlockSpec((tq, HD), lambda b, h, n: (n, 0)),
                  pl.BlockSpec((tq, HD), lambda b, h, n: (n, 0)),
                  pl.BlockSpec((seq, HD), lambda b, h, n: (0, 0)),
                  pl.BlockSpec((seq, HD), lambda b, h, n: (0, 0))],
        out_specs=pl.BlockSpec((tq, HD), lambda b, h, n: (b * nq + n, h)),
        out_shape=jax.ShapeDtypeStruct((batch * seq, heads * HD), BF16),
        scratch_shapes=[pltpu.VMEM((HD, seq), BF16), pltpu.VMEM((HD, past), BF16), pltpu.VMEM((seq, HD), BF16)],
        compiler_params=_cp(("arbitrary", "arbitrary", "arbitrary"), 40),
        name="attn_b_lat",
    )(b_lambda, b_subln.reshape(1, HD), u_attn, u_attn, u_attn, cache_k, cache_v,
      rope_c, rope_s, rope_c, rope_s)


def dft_half_tables(l):
    h = l // 2
    r = jnp.arange(l, dtype=jnp.int32)[:, None]
    f = r % h
    quarter = jnp.where(r >= h, l, 0)
    j = jnp.arange(h, dtype=jnp.int32)[None, :]

    def table(pos):
        m = ((2 * f + 1) * pos + quarter) % (4 * l)
        t = jnp.cos(m.astype(F32) * (math.pi / (2 * l)))
        hi = t.astype(BF16)
        return hi, (t - hi.astype(F32)).astype(BF16)

    (te, te_lo), (to, to_lo) = table(2 * j), table(2 * j + 1)
    return te, te_lo, to, to_lo, te.T, to.T


def parity_order(l):
    return jnp.concatenate([jnp.arange(0, l, 2), jnp.arange(1, l, 2)])


def hyena_feats(l):
    t = jnp.linspace(0.0, 1.0, l, dtype=F32)[:, None]
    bands = (HY_EMB - 1) // 2
    w = 2.0 * math.pi * jnp.arange(l, dtype=F32)[:, None] / l
    fr = jnp.linspace(1e-4, bands - 1, bands, dtype=F32)[None, :]
    feats = jnp.concatenate([t, jnp.cos(fr * w), -jnp.sin(fr * w)], axis=-1)
    return jnp.pad(feats, ((0, 0), (0, HD - HY_EMB)))


def hyena_decay(l, width):
    t = jnp.linspace(0.0, 1.0, l, dtype=F32)[:, None]
    deltas = jnp.abs(jnp.linspace(math.log(HY_TARGET) / HY_SLOW, math.log(HY_TARGET) / HY_FAST, width, dtype=F32))
    return jnp.exp(-t * deltas)


def _hy_time_body(feat_ref, dec_ref, w1_ref, b1_ref, w2_ref, b2_ref, w3_ref, fr_ref, kh_ref, kl_ref, nrm_scr,
                  *, width, l):
    ps = pl.program_id(2)
    c = pl.program_id(3)
    fr = fr_ref[...]
    h = jnp.sin(fr[0:1] * (_dot3(feat_ref[...], w1_ref[...]) + b1_ref[...]))
    h = jnp.sin(fr[1:2] * (_dot3(h, w2_ref[...]) + b2_ref[...]))
    dec = dec_ref[...]
    row = lax.broadcasted_iota(jnp.int32, dec.shape, 0)
    fwd = _dot3(h, w3_ref[:, :width]) * dec
    bwd = jnp.where((row == 0) & (c == 0), 0.0, _dot3(h, w3_ref[:, width:]) * dec)

    @pl.when((ps == 0) & (c == 0))
    def _():
        nrm_scr[...] = jnp.zeros_like(nrm_scr)

    @pl.when(ps == 0)
    def _():
        nrm_scr[...] += (jnp.sum(jnp.abs(fwd), axis=0, keepdims=True)
                         + jnp.sum(jnp.abs(bwd), axis=0, keepdims=True))

    @pl.when(ps == 1)
    def _():
        inv = 1.0 / ((nrm_scr[...] + EPS) * l)
        for part, val in ((0, fwd * inv), (1, bwd * inv)):
            hi, lo = _split(val)
            kh_ref[:, part * width:(part + 1) * width] = hi
            kl_ref[:, part * width:(part + 1) * width] = lo


def _hy_spec_body(ter_h, tei_h, tor_h, toi_h, ter_l, tei_l, tor_l, toi_l, keh_ref, kel_ref, koh_ref, kol_ref,
                   o_ref, *, width):
    keh, kel, koh, kol = keh_ref[...], kel_ref[...], koh_ref[...], kol_ref[...]

    def dot3(th, tl, kh, kl):
        t = th[...]
        return _dot(t, kh) + _dot(t, kl) + _dot(tl[...], kh)

    p_re, p_im = dot3(ter_h, ter_l, keh, kel), dot3(tei_h, tei_l, keh, kel)
    q_re, q_im = dot3(tor_h, tor_l, koh, kol), dot3(toi_h, toi_l, koh, kol)
    o_ref[0] = (p_re + q_re)[:, :width] + (p_re + q_re)[:, width:]
    o_ref[1] = (p_im + q_im)[:, :width] - (p_im + q_im)[:, width:]
    o_ref[2] = (p_re - q_re)[:, :width] + (p_re - q_re)[:, width:]
    o_ref[3] = (q_im - p_im)[:, :width] - (q_im - p_im)[:, width:]


def hyena_filters(l, hy_w1p, hy_b1, hy_w2, hy_b2, hy_w3, hy_freq, tabs, width, rc=512, tr=256):
    te, te_lo, to, to_lo = tabs[:4]
    depth = hy_w3.shape[0]
    ffn_w = hy_w2.shape[1]
    half = l // 2
    rc = min(rc, l)
    tr = min(tr, half)
    nr = half // tr
    order = parity_order(l)
    feats = hyena_feats(l)[order]
    dec = hyena_decay(l, width)[order]
    wspec = lambda shape: pl.BlockSpec((None,) + shape, lambda d, o, p, c: (d, 0, 0))
    kspec = pl.BlockSpec((None, rc, 2 * width), lambda d, o, p, c: (d, c * p, o))
    k_hi, k_lo = pl.pallas_call(
        functools.partial(_hy_time_body, width=width, l=l),
        grid=(depth, 2, 2, l // rc),
        in_specs=[pl.BlockSpec((rc, HD), lambda d, o, p, c: (c, 0)),
                  pl.BlockSpec((rc, width), lambda d, o, p, c: (c, 0)),
                  wspec((HD, ffn_w)), wspec((1, ffn_w)), wspec((ffn_w, ffn_w)), wspec((1, ffn_w)),
                  pl.BlockSpec((None, ffn_w, 2 * width), lambda d, o, p, c: (d, 0, o)),
                  wspec((2, ffn_w))],
        out_specs=[kspec, kspec],
        out_shape=[jax.ShapeDtypeStruct((depth, l, 4 * width), BF16)] * 2,
        scratch_shapes=[pltpu.VMEM((1, width), F32)],
        compiler_params=_cp(("arbitrary",) * 4, 32),
        name="hyena_time_filters",
    )(feats, dec, hy_w1p, hy_b1.reshape(depth, 1, ffn_w), hy_w2, hy_b2.reshape(depth, 1, ffn_w), hy_w3, hy_freq)
    re_spec = pl.BlockSpec((tr, half), lambda d, o, r: (r, 0))
    im_spec = pl.BlockSpec((tr, half), lambda d, o, r: (nr + r, 0))
    even = pl.BlockSpec((None, half, 2 * width), lambda d, o, r: (d, 0, o))
    odd = pl.BlockSpec((None, half, 2 * width), lambda d, o, r: (d, 1, o))
    return pl.pallas_call(
        functools.partial(_hy_spec_body, width=width),
        grid=(depth, 2, nr),
        in_specs=[re_spec, im_spec, re_spec, im_spec, re_spec, im_spec, re_spec, im_spec, even, even, odd, odd],
        out_specs=pl.BlockSpec((None, None, 4, tr, width), lambda d, o, r: (d, o, 0, r, 0)),
        out_shape=jax.ShapeDtypeStruct((depth, 2, 4, half, width), F32),
        compiler_params=_cp(("arbitrary",) * 3, 48),
        name="hyena_spectra",
    )(te, te, to, to, te_lo, te_lo, to_lo, to_lo, k_hi, k_lo, k_hi, k_lo)


def _short_conv(u, w, b):
    l = u.shape[0]
    row = lax.broadcasted_iota(jnp.int32, u.shape, 0)
    up = jnp.where(row == 0, 0.0, pltpu.roll(u, 1, 0))
    dn = jnp.where(row == l - 1, 0.0, pltpu.roll(u, l - 1, 0))
    return up * w[0:1] + u * w[1:2] + dn * w[2:3] + b


def _hyena_body(hv_ref, h1_ref, h2_ref, wv_ref, w1_ref, w2_ref, bv_ref, b1_ref, b2_ref, bias_ref,
                ter_ref, tei_ref, tor_ref, toi_ref, ger_ref, gei_ref, gor_ref, goi_ref, k_ref, o_ref,
                z_scr, ze_scr, zo_scr, acc_scr, *, nf):
    ph = pl.program_id(1)
    j = pl.program_id(2)
    half = ze_scr.shape[0]
    even = pl.ds(0, half, stride=2)
    odd = pl.ds(1, half, stride=2)
    tiles = [(t, slice(t * HD, (t + 1) * HD)) for t in range(z_scr.shape[0])]
    gather = lambda scr: jnp.concatenate([scr[t] for t, _ in tiles], axis=1)

    def set_input(z):
        for t, cols in tiles:
            z_scr[t] = z[:, cols]
            ze_scr[:, cols] = z_scr[t, even, :].astype(BF16)
            zo_scr[:, cols] = z_scr[t, odd, :].astype(BF16)

    @pl.when((ph == 0) & (j == 0))
    def _():
        set_input(_short_conv(hv_ref[...].astype(F32), wv_ref[...], bv_ref[...]))

    @pl.when(j == 0)
    def _():
        acc_scr[...] = jnp.zeros_like(acc_scr)

    ze, zo = ze_scr[...], zo_scr[...]
    p_re, p_im = _dot(ter_ref[...], ze), _dot(tei_ref[...], ze)
    q_re, q_im = _dot(tor_ref[...], zo), _dot(toi_ref[...], zo)
    z1_re, z1_im, z2_re, z2_im = p_re + q_re, p_im + q_im, p_re - q_re, q_im - p_im
    k1_re, k1_im, k2_re, k2_im = k_ref[0], k_ref[1], k_ref[2], k_ref[3]
    y1_re = z1_re * k1_re - z1_im * k1_im
    y1_im = z1_re * k1_im + z1_im * k1_re
    y2_re = z2_re * k2_re - z2_im * k2_im
    y2_im = z2_re * k2_im + z2_im * k2_re
    y_even = (_dot(ger_ref[...], (y1_re + y2_re).astype(BF16))
              + _dot(gei_ref[...], (y1_im - y2_im).astype(BF16)))
    y_odd = (_dot(gor_ref[...], (y1_re - y2_re).astype(BF16))
             + _dot(goi_ref[...], (y1_im + y2_im).astype(BF16)))
    for t, cols in tiles:
        acc_scr[t, even, :] += y_even[:, cols]
        acc_scr[t, odd, :] += y_odd[:, cols]

    @pl.when((j == nf - 1) & (ph == 0))
    def _():
        x1 = _short_conv(h1_ref[...].astype(F32), w1_ref[...], b1_ref[...])
        set_input(x1 * (gather(acc_scr) + gather(z_scr) * bias_ref[0:1]))

    @pl.when((j == nf - 1) & (ph == 1))
    def _():
        x2 = _short_conv(h2_ref[...].astype(F32), w2_ref[...], b2_ref[...])
        o_ref[...] = (x2 * (gather(acc_scr) + gather(z_scr) * bias_ref[1:2])).astype(BF16)


def hyena(u_hy, conv_w, conv_b, bias, kf, tabs, layer, batch, seq, row0, width, tfq=256):
    te, _, to, _, ge, go = tabs
    half = seq // 2
    tfq = min(tfq, half)
    nf = half // tfq
    rb = row0 // seq
    hspec = lambda part: pl.BlockSpec((seq, width), lambda b, p, j: (rb + b, part))
    wspec = lambda part: pl.BlockSpec((3, width), lambda b, p, j: (0, part))
    bspec = lambda part: pl.BlockSpec((1, width), lambda b, p, j: (0, part))
    t_re = pl.BlockSpec((tfq, half), lambda b, p, j: (j, 0))
    t_im = pl.BlockSpec((tfq, half), lambda b, p, j: (nf + j, 0))
    g_re = pl.BlockSpec((half, tfq), lambda b, p, j: (0, j))
    g_im = pl.BlockSpec((half, tfq), lambda b, p, j: (0, nf + j))
    return pl.pallas_call(
        functools.partial(_hyena_body, nf=nf),
        grid=(batch, 2, nf),
        in_specs=[hspec(0), hspec(1), hspec(2), wspec(0), wspec(1), wspec(2), bspec(0), bspec(1), bspec(2),
                  pl.BlockSpec((2, width), lambda b, p, j: (0, 0)),
                  t_re, t_im, t_re, t_im, g_re, g_im, g_re, g_im,
                  pl.BlockSpec((None, None, 4, tfq, width), lambda b, p, j: (layer, p, 0, j, 0))],
        out_specs=pl.BlockSpec((seq, width), lambda b, p, j: (b, 0)),
        out_shape=jax.ShapeDtypeStruct((batch * seq, width), BF16),
        scratch_shapes=[pltpu.VMEM((width // HD, seq, HD), F32), pltpu.VMEM((half, width), BF16),
                        pltpu.VMEM((half, width), BF16), pltpu.VMEM((width // HD, seq, HD), F32)],
        compiler_params=_cp(("arbitrary",) * 3, 56),
        name="hyena",
    )(u_hy, u_hy, u_hy, conv_w, conv_w, conv_w, conv_b, conv_b, conv_b, bias,
      te, te, to, to, ge, ge, go, go, kf)


def _hgrn_pair_level(t, s, fwd):
    x = t ^ s
    lvl = jnp.where(x == 0, 0, 32 - lax.clz(x))
    used = (s <= t) if fwd else (s >= t)
    return jnp.where(used, lvl, -1)


def _hgrn_midpoint(b3, blk, fwd):
    rows = b3.shape[1]
    r0 = blk // 2 - 1 if fwd else blk // 2
    ref = b3[:, r0:r0 + 1, :]
    if rows > blk:
        pos = lax.broadcasted_iota(jnp.int32, b3.shape, 1)
        for j in range(1, rows // blk):
            ref = jnp.where(pos >= j * blk, b3[:, j * blk + r0:j * blk + r0 + 1, :], ref)
    return ref


def _hgrn_chunk(q, k, g, v, st, tri, lvl, fwd):
    c = CHUNK
    gh, gl = _split(g)
    b = _dot(tri, gh) + _dot(tri, gl)
    tot = b[c - 1:c] if fwd else b[0:1]
    vb = v.astype(BF16)
    o = _dot_nt((q * jnp.exp2(b)).astype(BF16), st.astype(BF16))
    kd = (k * jnp.exp2(tot - b)).astype(BF16)
    st_new = st * jnp.exp2(tot) + _dot_tn(vb, kd)

    a = jnp.where(lvl == 0, jnp.sum(q * k, axis=-1, keepdims=True), 0.0)
    a = jnp.where(lvl == 1, _dot_nt((q * jnp.exp2(g)).astype(BF16), k.astype(BF16)), a)
    blk, m = 4, 2
    while blk <= c:
        rows = max(blk, SUB)
        b3, q3, k3 = (x.reshape(c // rows, rows, HD) for x in (b, q, k))
        e = jnp.exp2(-jnp.abs(b3 - _hgrn_midpoint(b3, blk, fwd)))
        qt = (q3 * e).reshape(c, HD).astype(BF16)
        kt = (k3 * e).reshape(c, HD).astype(BF16)
        a = jnp.where(lvl == m, _dot_nt(qt, kt), a)
        blk, m = blk * 2, m + 1
    return o + _dot(a.astype(BF16), vb), st_new


def _hgrn_gate(z, lb):
    e = jnp.exp(-jnp.abs(z))
    r = 1.0 / (1.0 + e)
    pos = z >= 0
    sig_pos = jnp.where(pos, r, e * r)
    sig_neg = jnp.where(pos, e * r, r)
    f = sig_pos + jnp.maximum(lb, LB_FLOOR) * sig_neg
    return jnp.log(f) * LOG2E, (1.0 - lb) * sig_neg


def _hgrn_lower_bound(lb_ref, layer, depth, direction):
    x = lb_ref[...]
    rows = [x[2 * i + direction:2 * i + direction + 1] for i in range(depth)]
    m = functools.reduce(jnp.maximum, rows)
    es = [jnp.exp(r - m) for r in rows]
    tot = functools.reduce(lambda a, b: a + b, es)
    cum = jnp.zeros_like(tot)
    for i in range(1, layer + 1):
        cum = cum + es[i]
    return jnp.maximum(cum / tot, 0.0)


def _hgrn_body(*refs, layer, depth, has_state, emit_state):
    it = iter(refs)
    lb_ref, ng_ref, q_ref, ff_ref, fb_ref, i_ref, g_ref = (next(it) for _ in range(7))
    s0_ref = next(it) if has_state else None
    o_ref = next(it)
    so_ref = next(it) if emit_state else None
    of_scr, ob_scr, st_scr, tri_scr, lvl_scr = (next(it) for _ in range(5))
    nc = q_ref.shape[0] // CHUNK
    hp = q_ref.shape[1] // HD
    ti = lax.broadcasted_iota(jnp.int32, (CHUNK, CHUNK), 0)
    si = lax.broadcasted_iota(jnp.int32, (CHUNK, CHUNK), 1)
    chains = [(h, d) for h in range(hp) for d in range(2)]
    lbs = {}
    for d in range(2):
        tri_scr[d] = jnp.where((si <= ti) if d == 0 else (si >= ti), 1.0, 0.0).astype(BF16)
        lvl_scr[d] = _hgrn_pair_level(ti, si, d == 0)
        lb_all = _hgrn_lower_bound(lb_ref, layer, depth, d)
        for h in range(hp):
            lbs[h, d] = lb_all[:, h * HD:(h + 1) * HD]
            st_scr[2 * h + d] = s0_ref[d, h].T if has_state else jnp.zeros((HD, HD), F32)

    def step(i, carry):
        for h, d in chains:
            fwd = d == 0
            ci = i if fwd else nc - 1 - i
            sl = pl.ds(pl.multiple_of(ci * CHUNK, CHUNK), CHUNK)
            cs = slice(h * HD, (h + 1) * HD)
            xq = q_ref[sl, cs].astype(F32)
            log_f, kk = _hgrn_gate((ff_ref if fwd else fb_ref)[sl, cs].astype(F32), lbs[h, d])
            o, st = _hgrn_chunk(xq * _sigmoid(xq), kk, log_f, i_ref[sl, cs].astype(F32), st_scr[2 * h + d],
                                tri_scr[d], lvl_scr[d], fwd)
            st_scr[2 * h + d] = st
            (of_scr if fwd else ob_scr)[sl, cs] = o
        return carry

    lax.fori_loop(0, nc, step, 0)
    for h, d in chains:
        if emit_state:
            so_ref[d, h] = st_scr[2 * h + d].T
    for h in range(hp):
        cs = slice(h * HD, (h + 1) * HD)
        xg = g_ref[:, cs].astype(F32)
        o_ref[:, cs] = (_rms_lanes(of_scr[:, cs] + ob_scr[:, cs], ng_ref[...]) * (xg * _sigmoid(xg))).astype(BF16)


def hgrn(u_hg, hg_lb2, hg_norm, state, layer, depth, batch, seq, row0, heads, emit_state, col0=0, hp=4):
    rb = row0 // seq
    ng = heads // hp
    w = hp * HD
    cb = col0 // w
    part = lambda p: pl.BlockSpec((seq, w), lambda b, h: (rb + b, cb + p * ng + h))
    in_specs = [pl.BlockSpec((2 * depth, w), lambda b, h: (0, h)),
                pl.BlockSpec((1, HD), lambda b, h: (0, 0)),
                part(0), part(1), part(2), part(3), part(4)]
    args = [hg_lb2, hg_norm.reshape(1, HD), u_hg, u_hg, u_hg, u_hg, u_hg]
    if state is not None:
        in_specs.append(pl.BlockSpec((None, None, 2, hp, HD, HD), lambda b, h: (b, layer, 0, h, 0, 0)))
        args.append(state)
    out_specs = [pl.BlockSpec((seq, w), lambda b, h: (b, h))]
    out_shape = [jax.ShapeDtypeStruct((batch * seq, heads * HD), BF16)]
    if emit_state:
        out_specs.append(pl.BlockSpec((None, 2, hp, HD, HD), lambda b, h: (b, 0, h, 0, 0)))
        out_shape.append(jax.ShapeDtypeStruct((batch, 2, heads, HD, HD), F32))
    return pl.pallas_call(
        functools.partial(_hgrn_body, layer=layer, depth=depth, has_state=state is not None,
                          emit_state=emit_state),
        grid=(batch, ng),
        in_specs=in_specs,
        out_specs=out_specs,
        out_shape=out_shape,
        scratch_shapes=[pltpu.VMEM((seq, w), F32), pltpu.VMEM((seq, w), F32),
                        pltpu.VMEM((2 * hp, HD, HD), F32),
                        pltpu.VMEM((2, CHUNK, CHUNK), BF16), pltpu.VMEM((2, CHUNK, CHUNK), jnp.int32)],
        compiler_params=_cp(("arbitrary", "arbitrary"), 48),
        name="hgrn",
    )(*args)


def _merge_body(*refs, n_ctx_tiles):
    o_refs, (gt_ref, x_ref, mod_ref, wb_ref, wo_ref, out_ref) = refs[:2 * N_BRANCH], refs[2 * N_BRANCH:]
    d = x_ref.shape[1]
    is_ctx = pl.program_id(0) < n_ctx_tiles
    acc = None
    for n in range(N_BRANCH):
        o = jnp.where(is_ctx, o_refs[2 * n][...], o_refs[2 * n + 1][...])
        y = gt_ref[:, n * d:(n + 1) * d].astype(F32) * _dot(o, wb_ref[n])
        acc = y if acc is None else acc + y
    out = _dot(acc.astype(BF16), wo_ref[...])
    out_ref[...] = x_ref[...] + mod_ref[...][5:6] * out


def merge(branch_outs, gates, x, mod_l, w_branch, w_out, t_ctx, l_lat, tm=256):
    t, d = x.shape
    mw = w_branch.shape[1]
    nct = t_ctx // tm
    row = functools.partial(_mod_row, tm=tm, t_ctx=t_ctx, l_lat=l_lat)
    cspec = pl.BlockSpec((tm, mw), lambda i: (jnp.minimum(i, nct - 1), 0))
    lspec = pl.BlockSpec((tm, mw), lambda i: (jnp.maximum(i - nct, 0), 0))
    return pl.pallas_call(
        functools.partial(_merge_body, n_ctx_tiles=nct),
        grid=(t // tm,),
        in_specs=[cspec, lspec] * N_BRANCH + [
                  pl.BlockSpec((tm, N_BRANCH * d), lambda i: (i, 0)),
                  pl.BlockSpec((tm, d), lambda i: (i, 0)),
                  pl.BlockSpec((None, N_MOD, d), lambda i: (row(i), 0, 0)),
                  pl.BlockSpec((N_BRANCH, mw, d), lambda i: (0, 0, 0), pipeline_mode=pl.Buffered(1)),
                  pl.BlockSpec((d, d), lambda i: (0, 0), pipeline_mode=pl.Buffered(1))],
        out_specs=pl.BlockSpec((tm, d), lambda i: (i, 0)),
        out_shape=jax.ShapeDtypeStruct((t, d), F32),
        compiler_params=_cp(("arbitrary",), 56),
        name="merge",
    )(*[o for pair in branch_outs for o in pair], gates, x, mod_l, w_branch, w_out)


def _final_norm_body(x_ref, g_ref, oc_ref, ol_ref, *, n_ctx_tiles):
    i = pl.program_id(0)

    @pl.when(i < n_ctx_tiles)
    def _():
        oc_ref[...] = _rms_lanes(x_ref[...], g_ref[...])

    @pl.when(i >= n_ctx_tiles)
    def _():
        ol_ref[...] = _rms_lanes(x_ref[...], g_ref[...])


def final_norm(x, g, t_ctx, tm=512):
    t, d = x.shape
    nct = t_ctx // tm
    return pl.pallas_call(
        functools.partial(_final_norm_body, n_ctx_tiles=nct),
        grid=(t // tm,),
        in_specs=[pl.BlockSpec((tm, d), lambda i: (i, 0)), pl.BlockSpec((1, d), lambda i: (0, 0))],
        out_specs=[pl.BlockSpec((tm, d), lambda i: (jnp.minimum(i, nct - 1), 0)),
                   pl.BlockSpec((tm, d), lambda i: (jnp.maximum(i - nct, 0), 0))],
        out_shape=[jax.ShapeDtypeStruct((t_ctx, d), F32), jax.ShapeDtypeStruct((t - t_ctx, d), F32)],
        compiler_params=_cp(("arbitrary",), 32),
        name="final_norm",
    )(x, g.reshape(1, d))


def kernel(x_prompt, x_sample, c, cache_a_k, cache_a_v, cache_b_k, cache_b_v, state_hgrn, c_ctx, w_mod, b_mod, norm_g, w_ffn1_gu, w_ffn1_down, w_ffn2_gu, w_ffn2_down, w_in, w_branch, w_out, a_sink, b_lambda, b_subln, hy_conv_w, hy_conv_b, hy_w1, hy_b1, hy_w2, hy_b2, hy_w3, hy_freq, hy_bias, hg_lb, hg_norm, final_g):
    batch, seq, d = x_prompt.shape
    dec_batch, dec_seq, _ = x_sample.shape
    depth = w_mod.shape[0]
    mix_w = w_branch.shape[2]
    a_heads = a_sink.shape[1]
    a_kvh = cache_a_k.shape[3]
    a_group = a_heads // a_kvh
    b_heads = cache_b_k.shape[3]
    hg_heads = state_hgrn.shape[3]
    t_ctx, t_lat = batch * seq, dec_batch * dec_seq

    n_attn = (a_heads + 2 * a_kvh + 3 * b_heads) * HD
    b_col0 = (a_heads + 2 * a_kvh) * HD

    x = jnp.concatenate([x_prompt.reshape(t_ctx, d), x_sample.reshape(t_lat, d)], axis=0)
    cond = jnp.concatenate([c_ctx[None, :], c], axis=0)
    cond = jnp.pad(cond, ((0, MOD_ROWS - cond.shape[0]), (0, 0)))
    mod = modulation(cond, w_mod, b_mod).reshape(depth, MOD_ROWS, N_MOD, d)

    rope_a = rope_tables(dec_seq, HD, 1)
    rope_b = rope_tables(dec_seq, HD // 2, 2)
    tabs_ctx = dft_half_tables(seq)
    tabs_lat = dft_half_tables(dec_seq)
    w1p = jnp.pad(hy_w1, ((0, 0), (0, HD - hy_w1.shape[1]), (0, 0)))
    kf_ctx = hyena_filters(seq, w1p, hy_b1, hy_w2, hy_b2, hy_w3, hy_freq, tabs_ctx, mix_w)
    kf_lat = hyena_filters(dec_seq, w1p, hy_b1, hy_w2, hy_b2, hy_w3, hy_freq, tabs_lat, mix_w)

    cak = cache_a_k.reshape(dec_batch, depth, -1, a_kvh * HD)
    cav = cache_a_v.reshape(dec_batch, depth, -1, a_kvh * HD)
    cbk = cache_b_k.reshape(dec_batch, depth, -1, b_heads * HD)
    cbv = cache_b_v.reshape(dec_batch, depth, -1, b_heads * HD)
    hg_lb2 = hg_lb.reshape(depth * 2, mix_w)

    ak_l, av_l, bk_l, bv_l, st_l = [], [], [], [], []
    for l in range(depth):
        lam_init = 0.8 - 0.6 * math.exp(-0.3 * l)
        mod_l = mod[l]
        x, h_mix = ffn(x, mod_l, norm_g[l, 0:1], w_ffn1_gu[l].astype(BF16), w_ffn1_down[l].astype(BF16), 0,
                       t_ctx, dec_seq, next_g_row=norm_g[l, 1:2])
        n_rec = n_attn + 8 * mix_w
        u_attn = proj(h_mix, w_in[l, :, :n_attn].astype(BF16), F32, False)
        u_rec = proj(h_mix, w_in[l, :, n_attn:n_rec].astype(BF16), BF16, False)
        gates = proj(h_mix, w_in[l, :, n_rec:].astype(BF16), BF16, True)

        uc = u_attn[:t_ctx]
        k0 = a_heads * HD
        ak_l.append(uc[:, k0:k0 + a_kvh * HD].reshape(batch, seq, a_kvh, HD))
        av_l.append(uc[:, k0 + a_kvh * HD:b_col0].reshape(batch, seq, a_kvh, HD))
        k1 = b_col0 + b_heads * HD
        bk_l.append(uc[:, k1:k1 + b_heads * HD].reshape(batch, seq, b_heads, 2, HD // 2))
        bv_l.append(uc[:, k1 + b_heads * HD:].reshape(batch, seq, b_heads, HD))

        oa_c = attn_a_ctx(u_attn, a_sink[l], batch, seq, a_kvh, a_group)
        oa_s = attn_a_lat(u_attn, cak, cav, a_sink[l], rope_a[0], rope_a[1], l, dec_batch, dec_seq, t_ctx,
                          a_kvh, a_group)
        ob_c = attn_b_ctx(u_attn, b_lambda[l], b_subln[l], lam_init, batch, seq, b_heads, b_col0)
        ob_s = attn_b_lat(u_attn, cbk, cbv, b_lambda[l], b_subln[l], rope_b[0], rope_b[1], lam_init, l,
                          dec_batch, dec_seq, t_ctx, b_heads, b_col0)
        cb = hy_conv_b[l].reshape(1, -1)
        oc_c = hyena(u_rec, hy_conv_w[l], cb, hy_bias[l], kf_ctx, tabs_ctx, l, batch, seq, 0, mix_w)
        oc_s = hyena(u_rec, hy_conv_w[l], cb, hy_bias[l], kf_lat, tabs_lat, l, dec_batch, dec_seq, t_ctx, mix_w)
        od_c, st = hgrn(u_rec, hg_lb2, hg_norm[l], None, l, depth, batch, seq, 0, hg_heads, True, col0=3 * mix_w)
        od_s, = hgrn(u_rec, hg_lb2, hg_norm[l], state_hgrn, l, depth, dec_batch, dec_seq, t_ctx, hg_heads, False,
                     col0=3 * mix_w)
        st_l.append(st)

        x = merge(((oa_c, oa_s), (ob_c, ob_s), (oc_c, oc_s), (od_c, od_s)), gates, x, mod_l,
                  w_branch[l].astype(BF16), w_out[l].astype(BF16), t_ctx, dec_seq)
        x = ffn(x, mod_l, norm_g[l, 2:3], w_ffn2_gu[l].astype(BF16), w_ffn2_down[l].astype(BF16), 2,
                t_ctx, dec_seq)

    y_ctx, y_lat = final_norm(x, final_g, t_ctx)
    return (y_ctx.reshape(batch, seq, d), y_lat.reshape(dec_batch, dec_seq, d),
            jnp.stack(ak_l, axis=1), jnp.stack(av_l, axis=1), jnp.stack(bk_l, axis=1), jnp.stack(bv_l, axis=1),
            jnp.stack(st_l, axis=1))
```

```python
import functools
import math

import jax
import jax.numpy as jnp
import numpy as np
from jax import lax
from jax.experimental import pallas as pl
from jax.experimental.pallas import tpu as pltpu

F32 = jnp.float32
BF16 = jnp.bfloat16

EPS = 1e-6
NEG = -1e30
LB_FLOOR = 1e-30
ROPE_BASE = 10000.0
GRID_W = 64
N_MOD = 9
N_BRANCH = 4
HD = 128
A_WINDOW = 128
HY_EMB = 33
HY_TARGET, HY_FAST, HY_SLOW = 1e-2, 0.3, 1.5
CHUNK = 128
SUB = 8
LOG2E = 1.4426950408889634
MOD_ROWS = 16
FFN_TILE = 512
PROJ_TILE = 512
VMEM_MB = 2 ** 20


def _cp(sem, vmem_mb):
    return pltpu.CompilerParams(dimension_semantics=sem, vmem_limit_bytes=vmem_mb * VMEM_MB)


def _dot(a, b):
    return jnp.dot(a, b, preferred_element_type=F32)


def _dot_nt(a, b):
    return lax.dot_general(a, b, (((1,), (1,)), ((), ())), preferred_element_type=F32)


def _dot_tn(a, b):
    return lax.dot_general(a, b, (((0,), (0,)), ((), ())), preferred_element_type=F32)


def _split(a):
    hi = a.astype(BF16)
    lo = (a - hi.astype(F32)).astype(BF16)
    return hi, lo


def _dot3(a, b):
    ah, al = _split(a)
    bh, bl = _split(b)
    return _dot(ah, bh) + _dot(ah, bl) + _dot(al, bh)


def _sigmoid(x):
    return 0.5 * jnp.tanh(0.5 * x) + 0.5


def _norm_mod(x, g, sc, sh):
    y = x * lax.rsqrt(jnp.mean(x * x, axis=-1, keepdims=True) + EPS)
    return (y * g) * (1.0 + sc) + sh


def _norm_mod_cols(load, keep, reload, store, shape, g, sc, sh):
    n, d = shape
    tiles = [slice(t, t + HD) for t in range(0, d, HD)]
    ssq = jnp.zeros((n, HD), F32)
    for cols in tiles:
        x = load(cols)
        if keep is not None:
            keep(cols, x)
        ssq = ssq + x * x
    r = lax.rsqrt(jnp.sum(ssq, axis=-1, keepdims=True) * (1.0 / d) + EPS)
    gs = g * (1.0 + sc)
    for cols in tiles:
        store(cols, (reload(cols) * r) * gs[:, cols] + sh[:, cols])


def _rms_lanes(x, g):
    return x * lax.rsqrt(jnp.mean(x * x, axis=-1, keepdims=True) + EPS) * g


def _mod_body(c_ref, w_ref, b_ref, o_ref):
    c = c_ref[...]
    a = (c * _sigmoid(c)).astype(BF16)
    o_ref[...] = _dot(a, w_ref[...].astype(BF16)) + b_ref[...]


def modulation(cond, w_mod, b_mod, tn=1024):
    depth, d, n = w_mod.shape
    return pl.pallas_call(
        _mod_body,
        grid=(depth, n // tn),
        in_specs=[pl.BlockSpec((MOD_ROWS, d), lambda l, j: (0, 0)),
                  pl.BlockSpec((None, d, tn), lambda l, j: (l, 0, j)),
                  pl.BlockSpec((None, 1, tn), lambda l, j: (l, 0, j))],
        out_specs=pl.BlockSpec((None, MOD_ROWS, tn), lambda l, j: (l, 0, j)),
        out_shape=jax.ShapeDtypeStruct((depth, MOD_ROWS, n), F32),
        compiler_params=_cp(("arbitrary", "arbitrary"), 40),
        name="modulation",
    )(cond, w_mod, b_mod.reshape(depth, 1, n))


def _mod_row(i, tm, t_ctx, l_lat):
    start = i * tm
    return jnp.where(start < t_ctx, 0, 1 + (start - t_ctx) // l_lat)


def _ffn_body(*refs, sub, nj, next_sub):
    if next_sub is None:
        x_ref, mod_ref, g_ref, wg_ref, wu_ref, wd_ref, o_ref, h_scr = refs
    else:
        x_ref, mod_ref, g_ref, gn_ref, wg_ref, wu_ref, wd_ref, o_ref, hn_ref, h_scr = refs
    j = pl.program_id(1)

    @pl.when(j == 0)
    def _():
        m = mod_ref[...]
        load = lambda cols: x_ref[:, cols]

        def store(cols, h):
            h_scr[:, cols] = h.astype(BF16)

        _norm_mod_cols(load, None, load, store, x_ref.shape, g_ref[...],
                       m[3 * sub + 1:3 * sub + 2], m[3 * sub:3 * sub + 1])
        o_ref[...] = jnp.zeros_like(o_ref)

    h = h_scr[...]
    tf = wg_ref.shape[1]
    hc = tf // 2
    halves = [slice(c0, c0 + hc) for c0 in range(0, tf, hc)]
    ab = [(_dot(h, wg_ref[:, cs]), _dot(h, wu_ref[:, cs])) for cs in halves]
    acts = [(a * _sigmoid(a) * b).astype(BF16) for a, b in ab]
    o_ref[...] += functools.reduce(lambda p, q: p + q, [_dot(act, wd_ref[cs, :]) for act, cs in zip(acts, halves)])

    @pl.when(j == nj - 1)
    def _():
        m = mod_ref[...]
        ga = 0.5 * m[3 * sub + 2:3 * sub + 3]
        if next_sub is None:
            o_ref[...] = x_ref[...] + ga * o_ref[...]
        else:
            def keep(cols, y):
                o_ref[:, cols] = y

            def store(cols, h):
                hn_ref[:, cols] = h.astype(BF16)

            _norm_mod_cols(lambda cols: x_ref[:, cols] + ga[:, cols] * o_ref[:, cols], keep,
                           lambda cols: o_ref[:, cols], store, x_ref.shape, gn_ref[...],
                           m[3 * next_sub + 1:3 * next_sub + 2], m[3 * next_sub:3 * next_sub + 1])


def ffn(x, mod_l, g_row, w_gu, w_dn, sub, t_ctx, l_lat, next_g_row=None, tm=512, tf=FFN_TILE):
    t, d = x.shape
    dff = w_dn.shape[0]
    nj = dff // tf
    emit = next_g_row is not None
    row = functools.partial(_mod_row, tm=tm, t_ctx=t_ctx, l_lat=l_lat)
    gspec = pl.BlockSpec((1, d), lambda i, j: (0, 0))
    xspec = pl.BlockSpec((tm, d), lambda i, j: (i, 0))
    outs = pl.pallas_call(
        functools.partial(_ffn_body, sub=sub, nj=nj, next_sub=sub + 1 if emit else None),
        grid=(t // tm, nj),
        in_specs=[xspec,
                  pl.BlockSpec((None, N_MOD, d), lambda i, j: (row(i), 0, 0)),
                  gspec] + ([gspec] if emit else []) + [
                  pl.BlockSpec((d, tf), lambda i, j: (0, j)),
                  pl.BlockSpec((d, tf), lambda i, j: (0, j + nj)),
                  pl.BlockSpec((tf, d), lambda i, j: (j, 0))],
        out_specs=[xspec] + ([xspec] if emit else []),
        out_shape=[jax.ShapeDtypeStruct((t, d), F32)] + ([jax.ShapeDtypeStruct((t, d), BF16)] if emit else []),
        scratch_shapes=[pltpu.VMEM((tm, d), BF16)],
        compiler_params=_cp(("arbitrary", "arbitrary"), 56),
        name="ffn",
    )(x, mod_l, g_row, *([next_g_row] if emit else []), w_gu, w_gu, w_dn)
    return outs if emit else outs[0]


def _proj_body(h_ref, w_ref, o_ref, *, gate):
    r = _dot(h_ref[...], w_ref[...])
    o_ref[...] = (_sigmoid(r) if gate else r).astype(o_ref.dtype)


def proj(h, w, out_dtype, gate, tm=2048, tn=PROJ_TILE):
    t, d = h.shape
    n = w.shape[1]
    while t % tm:
        tm //= 2
    return pl.pallas_call(
        functools.partial(_proj_body, gate=gate),
        grid=(t // tm, n // tn),
        in_specs=[pl.BlockSpec((tm, d), lambda i, j: (i, 0)),
                  pl.BlockSpec((d, tn), lambda i, j: (0, j))],
        out_specs=pl.BlockSpec((tm, tn), lambda i, j: (i, j)),
        out_shape=jax.ShapeDtypeStruct((t, n), out_dtype),
        compiler_params=_cp(("arbitrary", "arbitrary"), 56),
        name="proj",
    )(h, w)


def rope_tables(l, rot_dim, reps):
    n_rows = l // GRID_W
    rows = jnp.broadcast_to(jnp.arange(n_rows, dtype=F32)[:, None], (n_rows, GRID_W)).reshape(-1)
    cols = jnp.broadcast_to(jnp.arange(GRID_W, dtype=F32)[None, :], (n_rows, GRID_W)).reshape(-1)
    axis_dim = rot_dim // 2
    inv = ROPE_BASE ** (-jnp.arange(0, axis_dim, 2, dtype=F32) / axis_dim)
    ang = jnp.concatenate([rows[:, None] * inv, cols[:, None] * inv], axis=-1)
    cos = jnp.repeat(jnp.cos(ang), 2, axis=-1)
    sin = jnp.repeat(jnp.sin(ang), 2, axis=-1)
    sign = jnp.tile(jnp.array([-1.0, 1.0], F32), rot_dim // 2)
    return jnp.tile(cos, (1, reps)), jnp.tile(sin * sign, (1, reps))


def _rope(x, c, s):
    lane = lax.broadcasted_iota(jnp.int32, x.shape, 1)
    nxt = pltpu.roll(x, x.shape[1] - 1, 1)
    prv = pltpu.roll(x, 1, 1)
    return x * c + jnp.where((lane & 1) == 0, nxt, prv) * s


def _attn_a_ctx_body(sink_ref, q_ref, k_ref, v_ref, o_ref, *, group, kvh):
    scale = HD ** -0.5
    for kh in range(kvh):
        k = k_ref[:, kh * HD:(kh + 1) * HD].astype(BF16)
        v = v_ref[:, kh * HD:(kh + 1) * HD].astype(BF16)
        for g in range(group):
            hs = slice((kh * group + g) * HD, (kh * group + g + 1) * HD)
            q = q_ref[:, hs].astype(BF16)
            s = _dot_nt(q, k) * scale
            sink = sink_ref[kh * group + g]
            m = jnp.maximum(jnp.max(s, axis=-1, keepdims=True), sink)
            p = jnp.exp(s - m)
            den = jnp.sum(p, axis=-1, keepdims=True) + jnp.exp(sink - m)
            o_ref[:, hs] = (_dot(p.astype(BF16), v) / den).astype(BF16)


def attn_a_ctx(u_attn, sink, batch, seq, kvh, group):
    qw = kvh * group * HD
    kw = kvh * HD
    kcol = qw // kw
    return pl.pallas_call(
        functools.partial(_attn_a_ctx_body, group=group, kvh=kvh),
        grid=(batch,),
        in_specs=[pl.BlockSpec(memory_space=pltpu.SMEM),
                  pl.BlockSpec((seq, qw), lambda b: (b, 0)),
                  pl.BlockSpec((seq, kw), lambda b: (b, kcol)),
                  pl.BlockSpec((seq, kw), lambda b: (b, kcol + 1))],
        out_specs=pl.BlockSpec((seq, qw), lambda b: (b, 0)),
        out_shape=jax.ShapeDtypeStruct((batch * seq, qw), BF16),
        compiler_params=_cp(("arbitrary",), 32),
        name="attn_a_ctx",
    )(sink, u_attn, u_attn, u_attn)


def _attn_a_lat_body(sink_ref, q_ref, k0_ref, k1_ref, k2_ref, v0_ref, v1_ref, v2_ref, kc_ref, vc_ref,
                     cq_ref, sq_ref, c0_ref, s0_ref, c2_ref, s2_ref, o_ref, *, group, kvh, seq):
    n = pl.program_id(1)
    blk = q_ref.shape[0]
    scale = HD ** -0.5
    cq, sq = cq_ref[...], sq_ref[...]
    qi = lax.broadcasted_iota(jnp.int32, (blk, 3 * blk), 0)
    kj = lax.broadcasted_iota(jnp.int32, (blk, 3 * blk), 1)
    kpos = (n - 1) * blk + kj
    qpos = n * blk + qi
    valid = (kpos >= 0) & (kpos < seq) & (jnp.abs(qpos - kpos) <= A_WINDOW)
    for kh in range(kvh):
        ks = slice(kh * HD, (kh + 1) * HD)
        kw = jnp.concatenate([_rope(k0_ref[:, ks], c0_ref[...], s0_ref[...]),
                              _rope(k1_ref[:, ks], cq, sq),
                              _rope(k2_ref[:, ks], c2_ref[...], s2_ref[...])], axis=0).astype(BF16)
        vw = jnp.concatenate([v0_ref[:, ks], v1_ref[:, ks], v2_ref[:, ks]], axis=0).astype(BF16)
        kc = kc_ref[:, ks].astype(BF16)
        vc = vc_ref[:, ks].astype(BF16)
        for g in range(group):
            hs = slice((kh * group + g) * HD, (kh * group + g + 1) * HD)
            q = (_rope(q_ref[:, hs], cq, sq) * (scale * LOG2E)).astype(BF16)
            s_loc = jnp.where(valid, _dot_nt(q, kw), NEG)
            s_ctx = _dot_nt(q, kc)
            sink = sink_ref[kh * group + g] * LOG2E
            m = jnp.maximum(jnp.maximum(jnp.max(s_loc, axis=-1, keepdims=True),
                                        jnp.max(s_ctx, axis=-1, keepdims=True)), sink)
            p_loc = jnp.exp2(s_loc - m)
            p_ctx = jnp.exp2(s_ctx - m)
            den = (jnp.sum(p_loc, axis=-1, keepdims=True) + jnp.sum(p_ctx, axis=-1, keepdims=True)
                   + jnp.exp2(sink - m))
            o = _dot(p_ctx.astype(BF16), vc) + _dot(p_loc.astype(BF16), vw)
            o_ref[:, hs] = (o / den).astype(BF16)


def attn_a_lat(u_attn, cache_k, cache_v, sink, rope_c, rope_s, layer, batch, seq, row0, kvh, group, blk=256):
    nb = seq // blk
    rb0 = row0 // blk
    qw = kvh * group * HD
    kw = kvh * HD
    kcol = qw // kw
    vcol = kcol + 1
    past = cache_k.shape[2]

    def rows(b, n):
        return rb0 + b * nb + n

    prev = lambda n: jnp.maximum(n - 1, 0)
    nxt = lambda n: jnp.minimum(n + 1, nb - 1)
    tab = lambda f: pl.BlockSpec((blk, HD), lambda b, n: (f(n), 0))
    kv = lambda f, col: pl.BlockSpec((blk, kw), lambda b, n: (rows(b, f(n)), col))
    same = lambda n: n
    return pl.pallas_call(
        functools.partial(_attn_a_lat_body, group=group, kvh=kvh, seq=seq),
        grid=(batch, nb),
        in_specs=[pl.BlockSpec(memory_space=pltpu.SMEM),
                  pl.BlockSpec((blk, qw), lambda b, n: (rows(b, n), 0)),
                  kv(prev, kcol), kv(same, kcol), kv(nxt, kcol),
                  kv(prev, vcol), kv(same, vcol), kv(nxt, vcol),
                  pl.BlockSpec((None, None, past, kw), lambda b, n: (b, layer, 0, 0)),
                  pl.BlockSpec((None, None, past, kw), lambda b, n: (b, layer, 0, 0)),
                  tab(same), tab(same), tab(prev), tab(prev), tab(nxt), tab(nxt)],
        out_specs=pl.BlockSpec((blk, qw), lambda b, n: (b * nb + n, 0)),
        out_shape=jax.ShapeDtypeStruct((batch * seq, qw), BF16),
        compiler_params=_cp(("arbitrary", "arbitrary"), 40),
        name="attn_a_lat",
    )(sink, u_attn, u_attn, u_attn, u_attn, u_attn, u_attn, u_attn, cache_k, cache_v,
      rope_c, rope_s, rope_c, rope_s, rope_c, rope_s)


def _lambda(lam_ref, lam_init):
    lw = lam_ref[...]
    return (jnp.exp(jnp.sum(lw[0:1] * lw[1:2], axis=-1, keepdims=True))
            - jnp.exp(jnp.sum(lw[2:3] * lw[3:4], axis=-1, keepdims=True)) + lam_init)


def _softmax_parts(parts):
    m = functools.reduce(jnp.maximum, [jnp.max(s, axis=-1, keepdims=True) for s in parts])
    ps = [jnp.exp(s - m) for s in parts]
    inv = 1.0 / functools.reduce(lambda a, b: a + b, [jnp.sum(p, axis=-1, keepdims=True) for p in ps])
    return [p * inv for p in ps]


def _attn_b_ctx_body(lam_ref, g_ref, q_ref, k_ref, v_ref, o_ref, *, lam_init):
    hd = HD // 2
    scale = hd ** -0.5
    lam = _lambda(lam_ref, lam_init)
    for h in range(q_ref.shape[1] // HD):
        hs = slice(h * HD, (h + 1) * HD)
        q = q_ref[:, hs].astype(BF16)
        k = k_ref[:, hs].astype(BF16)
        p0, = _softmax_parts([_dot_nt(q[:, :hd], k[:, :hd]) * scale])
        p1, = _softmax_parts([_dot_nt(q[:, hd:], k[:, hd:]) * scale])
        o = _dot((p0 - lam * p1).astype(BF16), v_ref[:, hs].astype(BF16))
        o_ref[:, hs] = (_rms_lanes(o, g_ref[...]) * (1.0 - lam_init)).astype(BF16)


def attn_b_ctx(u_attn, b_lambda, b_subln, lam_init, batch, seq, heads, col0):
    w = heads * HD
    qcol = col0 // w
    return pl.pallas_call(
        functools.partial(_attn_b_ctx_body, lam_init=lam_init),
        grid=(batch,),
        in_specs=[pl.BlockSpec(b_lambda.shape, lambda b: (0, 0)),
                  pl.BlockSpec((1, HD), lambda b: (0, 0)),
                  pl.BlockSpec((seq, w), lambda b: (b, qcol)),
                  pl.BlockSpec((seq, w), lambda b: (b, qcol + 1)),
                  pl.BlockSpec((seq, w), lambda b: (b, qcol + 2))],
        out_specs=pl.BlockSpec((seq, w), lambda b: (b, 0)),
        out_shape=jax.ShapeDtypeStruct((batch * seq, w), BF16),
        compiler_params=_cp(("arbitrary",), 32),
        name="attn_b_ctx",
    )(b_lambda, b_subln.reshape(1, HD), u_attn, u_attn, u_attn)


def _attn_b_lat_body(lam_ref, g_ref, q_ref, k_ref, v_ref, kc_ref, vc_ref, cq_ref, sq_ref, ck_ref, sk_ref,
                     o_ref, k_scr, kc_scr, v_scr, *, lam_init):
    hd = HD // 2
    scale = hd ** -0.5
    nq = pl.program_id(2)

    @pl.when(nq == 0)
    def _():
        k_scr[...] = _rope(k_ref[...], ck_ref[...], sk_ref[...]).T.astype(BF16)
        kc_scr[...] = kc_ref[...].T.astype(BF16)
        v_scr[...] = v_ref[...].astype(BF16)

    lam = _lambda(lam_ref, lam_init)
    k = k_scr[...]
    v = v_scr[...]
    kc = kc_scr[...]
    vc = vc_ref[...].astype(BF16)
    q = _rope(q_ref[...], cq_ref[...], sq_ref[...]) * (scale * LOG2E)
    lane = lax.broadcasted_iota(jnp.int32, q.shape, 1)
    outs = []
    for c in range(2):
        qc = jnp.where((lane >= hd) == (c == 1), q, 0.0).astype(BF16)
        s_ctx = _dot(qc, kc)
        s_lat = _dot(qc, k)
        m = jnp.maximum(jnp.max(s_ctx, axis=-1, keepdims=True), jnp.max(s_lat, axis=-1, keepdims=True))
        p_ctx = jnp.exp2(s_ctx - m)
        p_lat = jnp.exp2(s_lat - m)
        den = jnp.sum(p_ctx, axis=-1, keepdims=True) + jnp.sum(p_lat, axis=-1, keepdims=True)
        outs.append((_dot(p_ctx.astype(BF16), vc) + _dot(p_lat.astype(BF16), v)) / den)
    o = outs[0] - lam * outs[1]
    o_ref[...] = (_rms_lanes(o, g_ref[...]) * (1.0 - lam_init)).astype(BF16)


def attn_b_lat(u_attn, cache_k, cache_v, b_lambda, b_subln, rope_c, rope_s, lam_init, layer,
               batch, seq, row0, heads, col0, tq=256):
    qcol = col0 // HD
    kcol = qcol + heads
    vcol = kcol + heads
    nq = seq // tq
    past = cache_k.shape[2]
    rq0 = row0 // tq
    rs0 = row0 // seq
    return pl.pallas_call(
        functools.partial(_attn_b_lat_body, lam_init=lam_init),
        grid=(batch, heads, nq),
        in_specs=[pl.BlockSpec(b_lambda.shape, lambda b, h, n: (0, 0)),
                  pl.BlockSpec((1, HD), lambda b, h, n: (0, 0)),
                  pl.BlockSpec((tq, HD), lambda b, h, n: (rq0 + b * nq + n, qcol + h)),
                  pl.BlockSpec((seq, HD), lambda b, h, n: (rs0 + b, kcol + h)),
                  pl.BlockSpec((seq, HD), lambda b, h, n: (rs0 + b, vcol + h)),
                  pl.BlockSpec((None, None, past, HD), lambda b, h, n: (b, layer, 0, h)),
                  pl.BlockSpec((None, None, past, HD), lambda b, h, n: (b, layer, 0, h)),
                  pl.BlockSpec((tq, HD), lambda b, h, n: (n, 0)),
                  pl.BlockSpec((tq, HD), lambda b, h, n: (n, 0)),
                  pl.BlockSpec((seq, HD), lambda b, h, n: (0, 0)),
                  pl.BlockSpec((seq, HD), lambda b, h, n: (0, 0))],
        out_specs=pl.BlockSpec((tq, HD), lambda b, h, n: (b * nq + n, h)),
        out_shape=jax.ShapeDtypeStruct((batch * seq, heads * HD), BF16),
        scratch_shapes=[pltpu.VMEM((HD, seq), BF16), pltpu.VMEM((HD, past), BF16), pltpu.VMEM((seq, HD), BF16)],
        compiler_params=_cp(("arbitrary", "arbitrary", "arbitrary"), 40),
        name="attn_b_lat",
    )(b_lambda, b_subln.reshape(1, HD), u_attn, u_attn, u_attn, cache_k, cache_v,
      rope_c, rope_s, rope_c, rope_s)


def dft_half_tables(l):
    h = l // 2
    r = jnp.arange(l, dtype=jnp.int32)[:, None]
    f = r % h
    quarter = jnp.where(r >= h, l, 0)
    j = jnp.arange(h, dtype=jnp.int32)[None, :]

    def table(pos):
        m = ((2 * f + 1) * pos + quarter) % (4 * l)
        t = jnp.cos(m.astype(F32) * (math.pi / (2 * l)))
        hi = t.astype(BF16)
        return hi, (t - hi.astype(F32)).astype(BF16)

    (te, te_lo), (to, to_lo) = table(2 * j), table(2 * j + 1)
    return te, te_lo, to, to_lo, te.T, to.T


def parity_order(l):
    return jnp.concatenate([jnp.arange(0, l, 2), jnp.arange(1, l, 2)])


def hyena_feats(l):
    t = jnp.linspace(0.0, 1.0, l, dtype=F32)[:, None]
    bands = (HY_EMB - 1) // 2
    w = 2.0 * math.pi * jnp.arange(l, dtype=F32)[:, None] / l
    fr = jnp.linspace(1e-4, bands - 1, bands, dtype=F32)[None, :]
    feats = jnp.concatenate([t, jnp.cos(fr * w), -jnp.sin(fr * w)], axis=-1)
    return jnp.pad(feats, ((0, 0), (0, HD - HY_EMB)))


def hyena_decay(l, width):
    t = jnp.linspace(0.0, 1.0, l, dtype=F32)[:, None]
    deltas = jnp.abs(jnp.linspace(math.log(HY_TARGET) / HY_SLOW, math.log(HY_TARGET) / HY_FAST, width, dtype=F32))
    return jnp.exp(-t * deltas)


def _hy_time_body(feat_ref, dec_ref, w1_ref, b1_ref, w2_ref, b2_ref, w3_ref, fr_ref, kh_ref, kl_ref, nrm_scr,
                  *, width, l):
    ps = pl.program_id(2)
    c = pl.program_id(3)
    fr = fr_ref[...]
    h = jnp.sin(fr[0:1] * (_dot3(feat_ref[...], w1_ref[...]) + b1_ref[...]))
    h = jnp.sin(fr[1:2] * (_dot3(h, w2_ref[...]) + b2_ref[...]))
    dec = dec_ref[...]
    row = lax.broadcasted_iota(jnp.int32, dec.shape, 0)
    fwd = _dot3(h, w3_ref[:, :width]) * dec
    bwd = jnp.where((row == 0) & (c == 0), 0.0, _dot3(h, w3_ref[:, width:]) * dec)

    @pl.when((ps == 0) & (c == 0))
    def _():
        nrm_scr[...] = jnp.zeros_like(nrm_scr)

    @pl.when(ps == 0)
    def _():
        nrm_scr[...] += (jnp.sum(jnp.abs(fwd), axis=0, keepdims=True)
                         + jnp.sum(jnp.abs(bwd), axis=0, keepdims=True))

    @pl.when(ps == 1)
    def _():
        inv = 1.0 / ((nrm_scr[...] + EPS) * l)
        for part, val in ((0, fwd * inv), (1, bwd * inv)):
            hi, lo = _split(val)
            kh_ref[:, part * width:(part + 1) * width] = hi
            kl_ref[:, part * width:(part + 1) * width] = lo


def _hy_spec_body(ter_h, tei_h, tor_h, toi_h, ter_l, tei_l, tor_l, toi_l, keh_ref, kel_ref, koh_ref, kol_ref,
                   o_ref, *, width):
    keh, kel, koh, kol = keh_ref[...], kel_ref[...], koh_ref[...], kol_ref[...]

    def dot3(th, tl, kh, kl):
        t = th[...]
        return _dot(t, kh) + _dot(t, kl) + _dot(tl[...], kh)

    p_re, p_im = dot3(ter_h, ter_l, keh, kel), dot3(tei_h, tei_l, keh, kel)
    q_re, q_im = dot3(tor_h, tor_l, koh, kol), dot3(toi_h, toi_l, koh, kol)
    o_ref[0] = (p_re + q_re)[:, :width] + (p_re + q_re)[:, width:]
    o_ref[1] = (p_im + q_im)[:, :width] - (p_im + q_im)[:, width:]
    o_ref[2] = (p_re - q_re)[:, :width] + (p_re - q_re)[:, width:]
    o_ref[3] = (q_im - p_im)[:, :width] - (q_im - p_im)[:, width:]


def hyena_filters(l, hy_w1p, hy_b1, hy_w2, hy_b2, hy_w3, hy_freq, tabs, width, rc=512, tr=256):
    te, te_lo, to, to_lo = tabs[:4]
    depth = hy_w3.shape[0]
    ffn_w = hy_w2.shape[1]
    half = l // 2
    rc = min(rc, l)
    tr = min(tr, half)
    nr = half // tr
    order = parity_order(l)
    feats = hyena_feats(l)[order]
    dec = hyena_decay(l, width)[order]
    wspec = lambda shape: pl.BlockSpec((None,) + shape, lambda d, o, p, c: (d, 0, 0))
    kspec = pl.BlockSpec((None, rc, 2 * width), lambda d, o, p, c: (d, c * p, o))
    k_hi, k_lo = pl.pallas_call(
        functools.partial(_hy_time_body, width=width, l=l),
        grid=(depth, 2, 2, l // rc),
        in_specs=[pl.BlockSpec((rc, HD), lambda d, o, p, c: (c, 0)),
                  pl.BlockSpec((rc, width), lambda d, o, p, c: (c, 0)),
                  wspec((HD, ffn_w)), wspec((1, ffn_w)), wspec((ffn_w, ffn_w)), wspec((1, ffn_w)),
                  pl.BlockSpec((None, ffn_w, 2 * width), lambda d, o, p, c: (d, 0, o)),
                  wspec((2, ffn_w))],
        out_specs=[kspec, kspec],
        out_shape=[jax.ShapeDtypeStruct((depth, l, 4 * width), BF16)] * 2,
        scratch_shapes=[pltpu.VMEM((1, width), F32)],
        compiler_params=_cp(("arbitrary",) * 4, 32),
        name="hyena_time_filters",
    )(feats, dec, hy_w1p, hy_b1.reshape(depth, 1, ffn_w), hy_w2, hy_b2.reshape(depth, 1, ffn_w), hy_w3, hy_freq)
    re_spec = pl.BlockSpec((tr, half), lambda d, o, r: (r, 0))
    im_spec = pl.BlockSpec((tr, half), lambda d, o, r: (nr + r, 0))
    even = pl.BlockSpec((None, half, 2 * width), lambda d, o, r: (d, 0, o))
    odd = pl.BlockSpec((None, half, 2 * width), lambda d, o, r: (d, 1, o))
    return pl.pallas_call(
        functools.partial(_hy_spec_body, width=width),
        grid=(depth, 2, nr),
        in_specs=[re_spec, im_spec, re_spec, im_spec, re_spec, im_spec, re_spec, im_spec, even, even, odd, odd],
        out_specs=pl.BlockSpec((None, None, 4, tr, width), lambda d, o, r: (d, o, 0, r, 0)),
        out_shape=jax.ShapeDtypeStruct((depth, 2, 4, half, width), F32),
        compiler_params=_cp(("arbitrary",) * 3, 48),
        name="hyena_spectra",
    )(te, te, to, to, te_lo, te_lo, to_lo, to_lo, k_hi, k_lo, k_hi, k_lo)


def _short_conv(u, w, b):
    l = u.shape[0]
    row = lax.broadcasted_iota(jnp.int32, u.shape, 0)
    up = jnp.where(row == 0, 0.0, pltpu.roll(u, 1, 0))
    dn = jnp.where(row == l - 1, 0.0, pltpu.roll(u, l - 1, 0))
    return up * w[0:1] + u * w[1:2] + dn * w[2:3] + b


def _hyena_body(hv_ref, h1_ref, h2_ref, wv_ref, w1_ref, w2_ref, bv_ref, b1_ref, b2_ref, bias_ref,
                ter_ref, tei_ref, tor_ref, toi_ref, ger_ref, gei_ref, gor_ref, goi_ref, k_ref, o_ref,
                z_scr, ze_scr, zo_scr, acc_scr, *, nf):
    ph = pl.program_id(1)
    j = pl.program_id(2)
    half = ze_scr.shape[0]
    even = pl.ds(0, half, stride=2)
    odd = pl.ds(1, half, stride=2)
    tiles = [(t, slice(t * HD, (t + 1) * HD)) for t in range(z_scr.shape[0])]
    gather = lambda scr: jnp.concatenate([scr[t] for t, _ in tiles], axis=1)

    def set_input(z):
        for t, cols in tiles:
            z_scr[t] = z[:, cols]
            ze_scr[:, cols] = z_scr[t, even, :].astype(BF16)
            zo_scr[:, cols] = z_scr[t, odd, :].astype(BF16)

    @pl.when((ph == 0) & (j == 0))
    def _():
        set_input(_short_conv(hv_ref[...].astype(F32), wv_ref[...], bv_ref[...]))

    @pl.when(j == 0)
    def _():
        acc_scr[...] = jnp.zeros_like(acc_scr)

    ze, zo = ze_scr[...], zo_scr[...]
    p_re, p_im = _dot(ter_ref[...], ze), _dot(tei_ref[...], ze)
    q_re, q_im = _dot(tor_ref[...], zo), _dot(toi_ref[...], zo)
    z1_re, z1_im, z2_re, z2_im = p_re + q_re, p_im + q_im, p_re - q_re, q_im - p_im
    k1_re, k1_im, k2_re, k2_im = k_ref[0], k_ref[1], k_ref[2], k_ref[3]
    y1_re = z1_re * k1_re - z1_im * k1_im
    y1_im = z1_re * k1_im + z1_im * k1_re
    y2_re = z2_re * k2_re - z2_im * k2_im
    y2_im = z2_re * k2_im + z2_im * k2_re
    y_even = (_dot(ger_ref[...], (y1_re + y2_re).astype(BF16))
              + _dot(gei_ref[...], (y1_im - y2_im).astype(BF16)))
    y_odd = (_dot(gor_ref[...], (y1_re - y2_re).astype(BF16))
             + _dot(goi_ref[...], (y1_im + y2_im).astype(BF16)))
    for t, cols in tiles:
        acc_scr[t, even, :] += y_even[:, cols]
        acc_scr[t, odd, :] += y_odd[:, cols]

    @pl.when((j == nf - 1) & (ph == 0))
    def _():
        x1 = _short_conv(h1_ref[...].astype(F32), w1_ref[...], b1_ref[...])
        set_input(x1 * (gather(acc_scr) + gather(z_scr) * bias_ref[0:1]))

    @pl.when((j == nf - 1) & (ph == 1))
    def _():
        x2 = _short_conv(h2_ref[...].astype(F32), w2_ref[...], b2_ref[...])
        o_ref[...] = (x2 * (gather(acc_scr) + gather(z_scr) * bias_ref[1:2])).astype(BF16)


def hyena(u_hy, conv_w, conv_b, bias, kf, tabs, layer, batch, seq, row0, width, tfq=256):
    te, _, to, _, ge, go = tabs
    half = seq // 2
    tfq = min(tfq, half)
    nf = half // tfq
    rb = row0 // seq
    hspec = lambda part: pl.BlockSpec((seq, width), lambda b, p, j: (rb + b, part))
    wspec = lambda part: pl.BlockSpec((3, width), lambda b, p, j: (0, part))
    bspec = lambda part: pl.BlockSpec((1, width), lambda b, p, j: (0, part))
    t_re = pl.BlockSpec((tfq, half), lambda b, p, j: (j, 0))
    t_im = pl.BlockSpec((tfq, half), lambda b, p, j: (nf + j, 0))
    g_re = pl.BlockSpec((half, tfq), lambda b, p, j: (0, j))
    g_im = pl.BlockSpec((half, tfq), lambda b, p, j: (0, nf + j))
    return pl.pallas_call(
        functools.partial(_hyena_body, nf=nf),
        grid=(batch, 2, nf),
        in_specs=[hspec(0), hspec(1), hspec(2), wspec(0), wspec(1), wspec(2), bspec(0), bspec(1), bspec(2),
                  pl.BlockSpec((2, width), lambda b, p, j: (0, 0)),
                  t_re, t_im, t_re, t_im, g_re, g_im, g_re, g_im,
                  pl.BlockSpec((None, None, 4, tfq, width), lambda b, p, j: (layer, p, 0, j, 0))],
        out_specs=pl.BlockSpec((seq, width), lambda b, p, j: (b, 0)),
        out_shape=jax.ShapeDtypeStruct((batch * seq, width), BF16),
        scratch_shapes=[pltpu.VMEM((width // HD, seq, HD), F32), pltpu.VMEM((half, width), BF16),
                        pltpu.VMEM((half, width), BF16), pltpu.VMEM((width // HD, seq, HD), F32)],
        compiler_params=_cp(("arbitrary",) * 3, 56),
        name="hyena",
    )(u_hy, u_hy, u_hy, conv_w, conv_w, conv_w, conv_b, conv_b, conv_b, bias,
      te, te, to, to, ge, ge, go, go, kf)


def _hgrn_pair_level(t, s, fwd):
    x = t ^ s
    lvl = jnp.where(x == 0, 0, 32 - lax.clz(x))
    used = (s <= t) if fwd else (s >= t)
    return jnp.where(used, lvl, -1)


def _hgrn_midpoint(b3, blk, fwd):
    rows = b3.shape[1]
    r0 = blk // 2 - 1 if fwd else blk // 2
    ref = b3[:, r0:r0 + 1, :]
    if rows > blk:
        pos = lax.broadcasted_iota(jnp.int32, b3.shape, 1)
        for j in range(1, rows // blk):
            ref = jnp.where(pos >= j * blk, b3[:, j * blk + r0:j * blk + r0 + 1, :], ref)
    return ref


def _hgrn_chunk(q, k, g, v, st, tri, lvl, fwd):
    c = CHUNK
    gh, gl = _split(g)
    b = _dot(tri, gh) + _dot(tri, gl)
    tot = b[c - 1:c] if fwd else b[0:1]
    vb = v.astype(BF16)
    o = _dot_nt((q * jnp.exp2(b)).astype(BF16), st.astype(BF16))
    kd = (k * jnp.exp2(tot - b)).astype(BF16)
    st_new = st * jnp.exp2(tot) + _dot_tn(vb, kd)

    a = jnp.where(lvl == 0, jnp.sum(q * k, axis=-1, keepdims=True), 0.0)
    a = jnp.where(lvl == 1, _dot_nt((q * jnp.exp2(g)).astype(BF16), k.astype(BF16)), a)
    blk, m = 4, 2
    while blk <= c:
        rows = max(blk, SUB)
        b3, q3, k3 = (x.reshape(c // rows, rows, HD) for x in (b, q, k))
        e = jnp.exp2(-jnp.abs(b3 - _hgrn_midpoint(b3, blk, fwd)))
        qt = (q3 * e).reshape(c, HD).astype(BF16)
        kt = (k3 * e).reshape(c, HD).astype(BF16)
        a = jnp.where(lvl == m, _dot_nt(qt, kt), a)
        blk, m = blk * 2, m + 1
    return o + _dot(a.astype(BF16), vb), st_new


def _hgrn_gate(z, lb):
    e = jnp.exp(-jnp.abs(z))
    r = 1.0 / (1.0 + e)
    pos = z >= 0
    sig_pos = jnp.where(pos, r, e * r)
    sig_neg = jnp.where(pos, e * r, r)
    f = sig_pos + jnp.maximum(lb, LB_FLOOR) * sig_neg
    return jnp.log(f) * LOG2E, (1.0 - lb) * sig_neg


def _hgrn_lower_bound(lb_ref, layer, depth, direction):
    x = lb_ref[...]
    rows = [x[2 * i + direction:2 * i + direction + 1] for i in range(depth)]
    m = functools.reduce(jnp.maximum, rows)
    es = [jnp.exp(r - m) for r in rows]
    tot = functools.reduce(lambda a, b: a + b, es)
    cum = jnp.zeros_like(tot)
    for i in range(1, layer + 1):
        cum = cum + es[i]
    return jnp.maximum(cum / tot, 0.0)


def _hgrn_body(*refs, layer, depth, has_state, emit_state):
    it = iter(refs)
    lb_ref, ng_ref, q_ref, ff_ref, fb_ref, i_ref, g_ref = (next(it) for _ in range(7))
    s0_ref = next(it) if has_state else None
    o_ref = next(it)
    so_ref = next(it) if emit_state else None
    of_scr, ob_scr, st_scr, tri_scr, lvl_scr = (next(it) for _ in range(5))
    nc = q_ref.shape[0] // CHUNK
    hp = q_ref.shape[1] // HD
    ti = lax.broadcasted_iota(jnp.int32, (CHUNK, CHUNK), 0)
    si = lax.broadcasted_iota(jnp.int32, (CHUNK, CHUNK), 1)
    chains = [(h, d) for h in range(hp) for d in range(2)]
    lbs = {}
    for d in range(2):
        tri_scr[d] = jnp.where((si <= ti) if d == 0 else (si >= ti), 1.0, 0.0).astype(BF16)
        lvl_scr[d] = _hgrn_pair_level(ti, si, d == 0)
        lb_all = _hgrn_lower_bound(lb_ref, layer, depth, d)
        for h in range(hp):
            lbs[h, d] = lb_all[:, h * HD:(h + 1) * HD]
            st_scr[2 * h + d] = s0_ref[d, h].T if has_state else jnp.zeros((HD, HD), F32)

    def step(i, carry):
        for h, d in chains:
            fwd = d == 0
            ci = i if fwd else nc - 1 - i
            sl = pl.ds(pl.multiple_of(ci * CHUNK, CHUNK), CHUNK)
            cs = slice(h * HD, (h + 1) * HD)
            xq = q_ref[sl, cs].astype(F32)
            log_f, kk = _hgrn_gate((ff_ref if fwd else fb_ref)[sl, cs].astype(F32), lbs[h, d])
            o, st = _hgrn_chunk(xq * _sigmoid(xq), kk, log_f, i_ref[sl, cs].astype(F32), st_scr[2 * h + d],
                                tri_scr[d], lvl_scr[d], fwd)
            st_scr[2 * h + d] = st
            (of_scr if fwd else ob_scr)[sl, cs] = o
        return carry

    lax.fori_loop(0, nc, step, 0, unroll=2)
    for h, d in chains:
        if emit_state:
            so_ref[d, h] = st_scr[2 * h + d].T
    for h in range(hp):
        cs = slice(h * HD, (h + 1) * HD)
        xg = g_ref[:, cs].astype(F32)
        o_ref[:, cs] = (_rms_lanes(of_scr[:, cs] + ob_scr[:, cs], ng_ref[...]) * (xg * _sigmoid(xg))).astype(BF16)


def hgrn(u_hg, hg_lb2, hg_norm, state, layer, depth, batch, seq, row0, heads, emit_state, col0=0, hp=4):
    rb = row0 // seq
    ng = heads // hp
    w = hp * HD
    cb = col0 // w
    part = lambda p: pl.BlockSpec((seq, w), lambda b, h: (rb + b, cb + p * ng + h))
    in_specs = [pl.BlockSpec((2 * depth, w), lambda b, h: (0, h)),
                pl.BlockSpec((1, HD), lambda b, h: (0, 0)),
                part(0), part(1), part(2), part(3), part(4)]
    args = [hg_lb2, hg_norm.reshape(1, HD), u_hg, u_hg, u_hg, u_hg, u_hg]
    if state is not None:
        in_specs.append(pl.BlockSpec((None, None, 2, hp, HD, HD), lambda b, h: (b, layer, 0, h, 0, 0)))
        args.append(state)
    out_specs = [pl.BlockSpec((seq, w), lambda b, h: (b, h))]
    out_shape = [jax.ShapeDtypeStruct((batch * seq, heads * HD), BF16)]
    if emit_state:
        out_specs.append(pl.BlockSpec((None, 2, hp, HD, HD), lambda b, h: (b, 0, h, 0, 0)))
        out_shape.append(jax.ShapeDtypeStruct((batch, 2, heads, HD, HD), F32))
    return pl.pallas_call(
        functools.partial(_hgrn_body, layer=layer, depth=depth, has_state=state is not None,
                          emit_state=emit_state),
        grid=(batch, ng),
        in_specs=in_specs,
        out_specs=out_specs,
        out_shape=out_shape,
        scratch_shapes=[pltpu.VMEM((seq, w), F32), pltpu.VMEM((seq, w), F32),
                        pltpu.VMEM((2 * hp, HD, HD), F32),
                        pltpu.VMEM((2, CHUNK, CHUNK), BF16), pltpu.VMEM((2, CHUNK, CHUNK), jnp.int32)],
        compiler_params=_cp(("arbitrary", "arbitrary"), 48),
        name="hgrn",
    )(*args)


def _merge_body(*refs, n_ctx_tiles):
    o_refs, (gt_ref, x_ref, mod_ref, wb_ref, wo_ref, out_ref) = refs[:2 * N_BRANCH], refs[2 * N_BRANCH:]
    d = x_ref.shape[1]
    is_ctx = pl.program_id(0) < n_ctx_tiles
    acc = None
    for n in range(N_BRANCH):
        o = jnp.where(is_ctx, o_refs[2 * n][...], o_refs[2 * n + 1][...])
        y = gt_ref[:, n * d:(n + 1) * d].astype(F32) * _dot(o, wb_ref[n])
        acc = y if acc is None else acc + y
    out = _dot(acc.astype(BF16), wo_ref[...])
    out_ref[...] = x_ref[...] + mod_ref[...][5:6] * out


def merge(branch_outs, gates, x, mod_l, w_branch, w_out, t_ctx, l_lat, tm=256):
    t, d = x.shape
    mw = w_branch.shape[1]
    nct = t_ctx // tm
    row = functools.partial(_mod_row, tm=tm, t_ctx=t_ctx, l_lat=l_lat)
    cspec = pl.BlockSpec((tm, mw), lambda i: (jnp.minimum(i, nct - 1), 0))
    lspec = pl.BlockSpec((tm, mw), lambda i: (jnp.maximum(i - nct, 0), 0))
    return pl.pallas_call(
        functools.partial(_merge_body, n_ctx_tiles=nct),
        grid=(t // tm,),
        in_specs=[cspec, lspec] * N_BRANCH + [
                  pl.BlockSpec((tm, N_BRANCH * d), lambda i: (i, 0)),
                  pl.BlockSpec((tm, d), lambda i: (i, 0)),
                  pl.BlockSpec((None, N_MOD, d), lambda i: (row(i), 0, 0)),
                  pl.BlockSpec((N_BRANCH, mw, d), lambda i: (0, 0, 0), pipeline_mode=pl.Buffered(1)),
                  pl.BlockSpec((d, d), lambda i: (0, 0), pipeline_mode=pl.Buffered(1))],
        out_specs=pl.BlockSpec((tm, d), lambda i: (i, 0)),
        out_shape=jax.ShapeDtypeStruct((t, d), F32),
        compiler_params=_cp(("arbitrary",), 56),
        name="merge",
    )(*[o for pair in branch_outs for o in pair], gates, x, mod_l, w_branch, w_out)


def _final_norm_body(x_ref, g_ref, oc_ref, ol_ref, *, n_ctx_tiles):
    i = pl.program_id(0)

    @pl.when(i < n_ctx_tiles)
    def _():
        oc_ref[...] = _rms_lanes(x_ref[...], g_ref[...])

    @pl.when(i >= n_ctx_tiles)
    def _():
        ol_ref[...] = _rms_lanes(x_ref[...], g_ref[...])


def final_norm(x, g, t_ctx, tm=512):
    t, d = x.shape
    nct = t_ctx // tm
    return pl.pallas_call(
        functools.partial(_final_norm_body, n_ctx_tiles=nct),
        grid=(t // tm,),
        in_specs=[pl.BlockSpec((tm, d), lambda i: (i, 0)), pl.BlockSpec((1, d), lambda i: (0, 0))],
        out_specs=[pl.BlockSpec((tm, d), lambda i: (jnp.minimum(i, nct - 1), 0)),
                   pl.BlockSpec((tm, d), lambda i: (jnp.maximum(i - nct, 0), 0))],
        out_shape=[jax.ShapeDtypeStruct((t_ctx, d), F32), jax.ShapeDtypeStruct((t - t_ctx, d), F32)],
        compiler_params=_cp(("arbitrary",), 32),
        name="final_norm",
    )(x, g.reshape(1, d))


def kernel(x_prompt, x_sample, c, cache_a_k, cache_a_v, cache_b_k, cache_b_v, state_hgrn, c_ctx, w_mod, b_mod, norm_g, w_ffn1_gu, w_ffn1_down, w_ffn2_gu, w_ffn2_down, w_in, w_branch, w_out, a_sink, b_lambda, b_subln, hy_conv_w, hy_conv_b, hy_w1, hy_b1, hy_w2, hy_b2, hy_w3, hy_freq, hy_bias, hg_lb, hg_norm, final_g):
    batch, seq, d = x_prompt.shape
    dec_batch, dec_seq, _ = x_sample.shape
    depth = w_mod.shape[0]
    mix_w = w_branch.shape[2]
    a_heads = a_sink.shape[1]
    a_kvh = cache_a_k.shape[3]
    a_group = a_heads // a_kvh
    b_heads = cache_b_k.shape[3]
    hg_heads = state_hgrn.shape[3]
    t_ctx, t_lat = batch * seq, dec_batch * dec_seq

    n_attn = (a_heads + 2 * a_kvh + 3 * b_heads) * HD
    b_col0 = (a_heads + 2 * a_kvh) * HD

    x = jnp.concatenate([x_prompt.reshape(t_ctx, d), x_sample.reshape(t_lat, d)], axis=0)
    cond = jnp.concatenate([c_ctx[None, :], c], axis=0)
    cond = jnp.pad(cond, ((0, MOD_ROWS - cond.shape[0]), (0, 0)))
    mod = modulation(cond, w_mod, b_mod).reshape(depth, MOD_ROWS, N_MOD, d)

    rope_a = rope_tables(dec_seq, HD, 1)
    rope_b = rope_tables(dec_seq, HD // 2, 2)
    tabs_ctx = dft_half_tables(seq)
    tabs_lat = dft_half_tables(dec_seq)
    w1p = jnp.pad(hy_w1, ((0, 0), (0, HD - hy_w1.shape[1]), (0, 0)))
    kf_ctx = hyena_filters(seq, w1p, hy_b1, hy_w2, hy_b2, hy_w3, hy_freq, tabs_ctx, mix_w)
    kf_lat = hyena_filters(dec_seq, w1p, hy_b1, hy_w2, hy_b2, hy_w3, hy_freq, tabs_lat, mix_w)

    cak = cache_a_k.reshape(dec_batch, depth, -1, a_kvh * HD)
    cav = cache_a_v.reshape(dec_batch, depth, -1, a_kvh * HD)
    cbk = cache_b_k.reshape(dec_batch, depth, -1, b_heads * HD)
    cbv = cache_b_v.reshape(dec_batch, depth, -1, b_heads * HD)
    hg_lb2 = hg_lb.reshape(depth * 2, mix_w)

    ak_l, av_l, bk_l, bv_l, st_l = [], [], [], [], []
    for l in range(depth):
        lam_init = 0.8 - 0.6 * math.exp(-0.3 * l)
        mod_l = mod[l]
        x, h_mix = ffn(x, mod_l, norm_g[l, 0:1], w_ffn1_gu[l].astype(BF16), w_ffn1_down[l].astype(BF16), 0,
                       t_ctx, dec_seq, next_g_row=norm_g[l, 1:2])
        n_rec = n_attn + 8 * mix_w
        u_attn = proj(h_mix, w_in[l, :, :n_attn].astype(BF16), F32, False)
        u_rec = proj(h_mix, w_in[l, :, n_attn:n_rec].astype(BF16), BF16, False)
        gates = proj(h_mix, w_in[l, :, n_rec:].astype(BF16), BF16, True)

        uc = u_attn[:t_ctx]
        k0 = a_heads * HD
        ak_l.append(uc[:, k0:k0 + a_kvh * HD].reshape(batch, seq, a_kvh, HD))
        av_l.append(uc[:, k0 + a_kvh * HD:b_col0].reshape(batch, seq, a_kvh, HD))
        k1 = b_col0 + b_heads * HD
        bk_l.append(uc[:, k1:k1 + b_heads * HD].reshape(batch, seq, b_heads, 2, HD // 2))
        bv_l.append(uc[:, k1 + b_heads * HD:].reshape(batch, seq, b_heads, HD))

        oa_c = attn_a_ctx(u_attn, a_sink[l], batch, seq, a_kvh, a_group)
        oa_s = attn_a_lat(u_attn, cak, cav, a_sink[l], rope_a[0], rope_a[1], l, dec_batch, dec_seq, t_ctx,
                          a_kvh, a_group)
        ob_c = attn_b_ctx(u_attn, b_lambda[l], b_subln[l], lam_init, batch, seq, b_heads, b_col0)
        ob_s = attn_b_lat(u_attn, cbk, cbv, b_lambda[l], b_subln[l], rope_b[0], rope_b[1], lam_init, l,
                          dec_batch, dec_seq, t_ctx, b_heads, b_col0)
        cb = hy_conv_b[l].reshape(1, -1)
        oc_c = hyena(u_rec, hy_conv_w[l], cb, hy_bias[l], kf_ctx, tabs_ctx, l, batch, seq, 0, mix_w)
        oc_s = hyena(u_rec, hy_conv_w[l], cb, hy_bias[l], kf_lat, tabs_lat, l, dec_batch, dec_seq, t_ctx, mix_w)
        od_c, st = hgrn(u_rec, hg_lb2, hg_norm[l], None, l, depth, batch, seq, 0, hg_heads, True, col0=3 * mix_w)
        od_s, = hgrn(u_rec, hg_lb2, hg_norm[l], state_hgrn, l, depth, dec_batch, dec_seq, t_ctx, hg_heads, False,
                     col0=3 * mix_w)
        st_l.append(st)

        x = merge(((oa_c, oa_s), (ob_c, ob_s), (oc_c, oc_s), (od_c, od_s)), gates, x, mod_l,
                  w_branch[l].astype(BF16), w_out[l].astype(BF16), t_ctx, dec_seq)
        x = ffn(x, mod_l, norm_g[l, 2:3], w_ffn2_gu[l].astype(BF16), w_ffn2_down[l].astype(BF16), 2,
                t_ctx, dec_seq)

    y_ctx, y_lat = final_norm(x, final_g, t_ctx)
    return (y_ctx.reshape(batch, seq, d), y_lat.reshape(dec_batch, dec_seq, d),
            jnp.stack(ak_l, axis=1), jnp.stack(av_l, axis=1), jnp.stack(bk_l, axis=1), jnp.stack(bv_l, axis=1),
            jnp.stack(st_l, axis=1))
```

```python
import functools
import math

import jax
import jax.numpy as jnp
from jax import lax
from jax.experimental import pallas as pl
from jax.experimental.pallas import tpu as pltpu

F32 = jnp.float32
BF16 = jnp.bfloat16

EPS = 1e-6
NEG = -1e30
LB_FLOOR = 1e-30
ROPE_BASE = 10000.0
GRID_W = 64
N_MOD = 9
N_BRANCH = 4
HD = 128
A_WINDOW = 128
HY_EMB = 33
HY_TARGET, HY_FAST, HY_SLOW = 1e-2, 0.3, 1.5
CHUNK = 128
SUB = 8
LOG2E = 1.4426950408889634
MOD_ROWS = 16
FFN_TILE = 512
PROJ_TILE = 512
VMEM_MB = 2 ** 20


def _cp(sem, vmem_mb):
    return pltpu.CompilerParams(dimension_semantics=sem, vmem_limit_bytes=vmem_mb * VMEM_MB)


def _dot(a, b):
    return jnp.dot(a, b, preferred_element_type=F32)


def _dot_nt(a, b):
    return lax.dot_general(a, b, (((1,), (1,)), ((), ())), preferred_element_type=F32)


def _dot_tn(a, b):
    return lax.dot_general(a, b, (((0,), (0,)), ((), ())), preferred_element_type=F32)


def _split(a):
    hi = a.astype(BF16)
    lo = (a - hi.astype(F32)).astype(BF16)
    return hi, lo


def _dot3(a, b):
    ah, al = _split(a)
    bh, bl = _split(b)
    return _dot(ah, bh) + _dot(ah, bl) + _dot(al, bh)


def _sigmoid(x):
    return 0.5 * jnp.tanh(0.5 * x) + 0.5


def _norm_mod(x, g, sc, sh):
    y = x * lax.rsqrt(jnp.mean(x * x, axis=-1, keepdims=True) + EPS)
    return (y * g) * (1.0 + sc) + sh


def _norm_mod_cols(load, keep, reload, store, shape, g, sc, sh):
    n, d = shape
    tiles = [slice(t, t + HD) for t in range(0, d, HD)]
    ssq = jnp.zeros((n, HD), F32)
    for cols in tiles:
        x = load(cols)
        if keep is not None:
            keep(cols, x)
        ssq = ssq + x * x
    r = lax.rsqrt(jnp.sum(ssq, axis=-1, keepdims=True) * (1.0 / d) + EPS)
    gs = g * (1.0 + sc)
    for cols in tiles:
        store(cols, (reload(cols) * r) * gs[:, cols] + sh[:, cols])


def _rms_lanes(x, g):
    return x * lax.rsqrt(jnp.mean(x * x, axis=-1, keepdims=True) + EPS) * g


def _mod_body(c_ref, w_ref, b_ref, o_ref):
    c = c_ref[...]
    a = (c * _sigmoid(c)).astype(BF16)
    o_ref[...] = _dot(a, w_ref[...].astype(BF16)) + b_ref[...]


def modulation(cond, w_mod, b_mod, tn=1024):
    depth, d, n = w_mod.shape
    return pl.pallas_call(
        _mod_body,
        grid=(depth, n // tn),
        in_specs=[pl.BlockSpec((MOD_ROWS, d), lambda l, j: (0, 0)),
                  pl.BlockSpec((None, d, tn), lambda l, j: (l, 0, j)),
                  pl.BlockSpec((None, 1, tn), lambda l, j: (l, 0, j))],
        out_specs=pl.BlockSpec((None, MOD_ROWS, tn), lambda l, j: (l, 0, j)),
        out_shape=jax.ShapeDtypeStruct((depth, MOD_ROWS, n), F32),
        compiler_params=_cp(("arbitrary", "arbitrary"), 40),
        name="modulation",
    )(cond, w_mod, b_mod.reshape(depth, 1, n))


def _mod_row(i, tm, t_ctx, l_lat):
    start = i * tm
    return jnp.where(start < t_ctx, 0, 1 + (start - t_ctx) // l_lat)


def _ffn_body(*refs, sub, nj, next_sub):
    if next_sub is None:
        x_ref, mod_ref, g_ref, wg_ref, wu_ref, wd_ref, o_ref, h_scr = refs
    else:
        x_ref, mod_ref, g_ref, gn_ref, wg_ref, wu_ref, wd_ref, o_ref, hn_ref, h_scr = refs
    j = pl.program_id(1)

    @pl.when(j == 0)
    def _():
        m = mod_ref[...]
        load = lambda cols: x_ref[:, cols]

        def store(cols, h):
            h_scr[:, cols] = h.astype(BF16)

        _norm_mod_cols(load, None, load, store, x_ref.shape, g_ref[...],
                       m[3 * sub + 1:3 * sub + 2], m[3 * sub:3 * sub + 1])
        o_ref[...] = jnp.zeros_like(o_ref)

    h = h_scr[...]
    tf = wg_ref.shape[1]
    hc = tf // 2
    halves = [slice(c0, c0 + hc) for c0 in range(0, tf, hc)]
    ab = [(_dot(h, wg_ref[:, cs]), _dot(h, wu_ref[:, cs])) for cs in halves]
    acts = [(a * _sigmoid(a) * b).astype(BF16) for a, b in ab]
    o_ref[...] += functools.reduce(lambda p, q: p + q, [_dot(act, wd_ref[cs, :]) for act, cs in zip(acts, halves)])

    @pl.when(j == nj - 1)
    def _():
        m = mod_ref[...]
        ga = 0.5 * m[3 * sub + 2:3 * sub + 3]
        if next_sub is None:
            o_ref[...] = x_ref[...] + ga * o_ref[...]
        else:
            def keep(cols, y):
                o_ref[:, cols] = y

            def store(cols, h):
                hn_ref[:, cols] = h.astype(BF16)

            _norm_mod_cols(lambda cols: x_ref[:, cols] + ga[:, cols] * o_ref[:, cols], keep,
                           lambda cols: o_ref[:, cols], store, x_ref.shape, gn_ref[...],
                           m[3 * next_sub + 1:3 * next_sub + 2], m[3 * next_sub:3 * next_sub + 1])


def ffn(x, mod_l, g_row, w_gu, w_dn, sub, t_ctx, l_lat, next_g_row=None, tm=512, tf=FFN_TILE):
    t, d = x.shape
    dff = w_dn.shape[0]
    nj = dff // tf
    emit = next_g_row is not None
    row = functools.partial(_mod_row, tm=tm, t_ctx=t_ctx, l_lat=l_lat)
    gspec = pl.BlockSpec((1, d), lambda i, j: (0, 0))
    xspec = pl.BlockSpec((tm, d), lambda i, j: (i, 0))
    outs = pl.pallas_call(
        functools.partial(_ffn_body, sub=sub, nj=nj, next_sub=sub + 1 if emit else None),
        grid=(t // tm, nj),
        in_specs=[xspec,
                  pl.BlockSpec((None, N_MOD, d), lambda i, j: (row(i), 0, 0)),
                  gspec] + ([gspec] if emit else []) + [
                  pl.BlockSpec((d, tf), lambda i, j: (0, j)),
                  pl.BlockSpec((d, tf), lambda i, j: (0, j + nj)),
                  pl.BlockSpec((tf, d), lambda i, j: (j, 0))],
        out_specs=[xspec] + ([xspec] if emit else []),
        out_shape=[jax.ShapeDtypeStruct((t, d), F32)] + ([jax.ShapeDtypeStruct((t, d), BF16)] if emit else []),
        scratch_shapes=[pltpu.VMEM((tm, d), BF16)],
        compiler_params=_cp(("arbitrary", "arbitrary"), 56),
        name="ffn",
    )(x, mod_l, g_row, *([next_g_row] if emit else []), w_gu, w_gu, w_dn)
    return outs if emit else outs[0]


def _proj_body(h_ref, w_ref, o_ref, *, gate):
    r = _dot(h_ref[...], w_ref[...])
    o_ref[...] = (_sigmoid(r) if gate else r).astype(o_ref.dtype)


def proj(h, w, out_dtype, gate, tm=2048, tn=PROJ_TILE):
    t, d = h.shape
    n = w.shape[1]
    while t % tm:
        tm //= 2
    return pl.pallas_call(
        functools.partial(_proj_body, gate=gate),
        grid=(t // tm, n // tn),
        in_specs=[pl.BlockSpec((tm, d), lambda i, j: (i, 0)),
                  pl.BlockSpec((d, tn), lambda i, j: (0, j))],
        out_specs=pl.BlockSpec((tm, tn), lambda i, j: (i, j)),
        out_shape=jax.ShapeDtypeStruct((t, n), out_dtype),
        compiler_params=_cp(("arbitrary", "arbitrary"), 56),
        name="proj",
    )(h, w)


def rope_tables(l, rot_dim, reps):
    n_rows = l // GRID_W
    rows = jnp.broadcast_to(jnp.arange(n_rows, dtype=F32)[:, None], (n_rows, GRID_W)).reshape(-1)
    cols = jnp.broadcast_to(jnp.arange(GRID_W, dtype=F32)[None, :], (n_rows, GRID_W)).reshape(-1)
    axis_dim = rot_dim // 2
    inv = ROPE_BASE ** (-jnp.arange(0, axis_dim, 2, dtype=F32) / axis_dim)
    ang = jnp.concatenate([rows[:, None] * inv, cols[:, None] * inv], axis=-1)
    cos = jnp.repeat(jnp.cos(ang), 2, axis=-1)
    sin = jnp.repeat(jnp.sin(ang), 2, axis=-1)
    sign = jnp.tile(jnp.array([-1.0, 1.0], F32), rot_dim // 2)
    return jnp.tile(cos, (1, reps)), jnp.tile(sin * sign, (1, reps))


def _rope(x, c, s):
    lane = lax.broadcasted_iota(jnp.int32, x.shape, 1)
    nxt = pltpu.roll(x, x.shape[1] - 1, 1)
    prv = pltpu.roll(x, 1, 1)
    return x * c + jnp.where((lane & 1) == 0, nxt, prv) * s


def _attn_a_ctx_body(sink_ref, q_ref, k_ref, v_ref, o_ref, *, group, kvh):
    scale = HD ** -0.5
    for kh in range(kvh):
        k = k_ref[:, kh * HD:(kh + 1) * HD].astype(BF16)
        v = v_ref[:, kh * HD:(kh + 1) * HD].astype(BF16)
        for g in range(group):
            hs = slice((kh * group + g) * HD, (kh * group + g + 1) * HD)
            q = q_ref[:, hs].astype(BF16)
            s = _dot_nt(q, k) * scale
            sink = sink_ref[kh * group + g]
            m = jnp.maximum(jnp.max(s, axis=-1, keepdims=True), sink)
            p = jnp.exp(s - m)
            den = jnp.sum(p, axis=-1, keepdims=True) + jnp.exp(sink - m)
            o_ref[:, hs] = (_dot(p.astype(BF16), v) / den).astype(BF16)


def attn_a_ctx(u_attn, sink, batch, seq, kvh, group):
    qw = kvh * group * HD
    kw = kvh * HD
    kcol = qw // kw
    return pl.pallas_call(
        functools.partial(_attn_a_ctx_body, group=group, kvh=kvh),
        grid=(batch,),
        in_specs=[pl.BlockSpec(memory_space=pltpu.SMEM),
                  pl.BlockSpec((seq, qw), lambda b: (b, 0)),
                  pl.BlockSpec((seq, kw), lambda b: (b, kcol)),
                  pl.BlockSpec((seq, kw), lambda b: (b, kcol + 1))],
        out_specs=pl.BlockSpec((seq, qw), lambda b: (b, 0)),
        out_shape=jax.ShapeDtypeStruct((batch * seq, qw), BF16),
        compiler_params=_cp(("arbitrary",), 32),
        name="attn_a_ctx",
    )(sink, u_attn, u_attn, u_attn)


def _attn_a_lat_body(sink_ref, q_ref, k0_ref, k1_ref, k2_ref, v0_ref, v1_ref, v2_ref, kc_ref, vc_ref,
                     cq_ref, sq_ref, c0_ref, s0_ref, c2_ref, s2_ref, o_ref, *, group, kvh, seq):
    n = pl.program_id(1)
    blk = q_ref.shape[0]
    scale = HD ** -0.5
    cq, sq = cq_ref[...], sq_ref[...]
    qi = lax.broadcasted_iota(jnp.int32, (blk, 3 * blk), 0)
    kj = lax.broadcasted_iota(jnp.int32, (blk, 3 * blk), 1)
    kpos = (n - 1) * blk + kj
    qpos = n * blk + qi
    valid = (kpos >= 0) & (kpos < seq) & (jnp.abs(qpos - kpos) <= A_WINDOW)
    for kh in range(kvh):
        ks = slice(kh * HD, (kh + 1) * HD)
        kw = jnp.concatenate([_rope(k0_ref[:, ks], c0_ref[...], s0_ref[...]),
                              _rope(k1_ref[:, ks], cq, sq),
                              _rope(k2_ref[:, ks], c2_ref[...], s2_ref[...])], axis=0).astype(BF16)
        vw = jnp.concatenate([v0_ref[:, ks], v1_ref[:, ks], v2_ref[:, ks]], axis=0).astype(BF16)
        kc = kc_ref[:, ks].astype(BF16)
        vc = vc_ref[:, ks].astype(BF16)
        for g in range(group):
            hs = slice((kh * group + g) * HD, (kh * group + g + 1) * HD)
            q = (_rope(q_ref[:, hs], cq, sq) * (scale * LOG2E)).astype(BF16)
            s_loc = jnp.where(valid, _dot_nt(q, kw), NEG)
            s_ctx = _dot_nt(q, kc)
            sink = sink_ref[kh * group + g] * LOG2E
            m = jnp.maximum(jnp.maximum(jnp.max(s_loc, axis=-1, keepdims=True),
                                        jnp.max(s_ctx, axis=-1, keepdims=True)), sink)
            p_loc = jnp.exp2(s_loc - m)
            p_ctx = jnp.exp2(s_ctx - m)
            den = (jnp.sum(p_loc, axis=-1, keepdims=True) + jnp.sum(p_ctx, axis=-1, keepdims=True)
                   + jnp.exp2(sink - m))
            o = _dot(p_ctx.astype(BF16), vc) + _dot(p_loc.astype(BF16), vw)
            o_ref[:, hs] = (o / den).astype(BF16)


def attn_a_lat(u_attn, cache_k, cache_v, sink, rope_c, rope_s, layer, batch, seq, row0, kvh, group, blk=256):
    nb = seq // blk
    rb0 = row0 // blk
    qw = kvh * group * HD
    kw = kvh * HD
    kcol = qw // kw
    vcol = kcol + 1
    past = cache_k.shape[2]

    def rows(b, n):
        return rb0 + b * nb + n

    prev = lambda n: jnp.maximum(n - 1, 0)
    nxt = lambda n: jnp.minimum(n + 1, nb - 1)
    tab = lambda f: pl.BlockSpec((blk, HD), lambda b, n: (f(n), 0))
    kv = lambda f, col: pl.BlockSpec((blk, kw), lambda b, n: (rows(b, f(n)), col))
    same = lambda n: n
    return pl.pallas_call(
        functools.partial(_attn_a_lat_body, group=group, kvh=kvh, seq=seq),
        grid=(batch, nb),
        in_specs=[pl.BlockSpec(memory_space=pltpu.SMEM),
                  pl.BlockSpec((blk, qw), lambda b, n: (rows(b, n), 0)),
                  kv(prev, kcol), kv(same, kcol), kv(nxt, kcol),
                  kv(prev, vcol), kv(same, vcol), kv(nxt, vcol),
                  pl.BlockSpec((None, None, past, kw), lambda b, n: (b, layer, 0, 0)),
                  pl.BlockSpec((None, None, past, kw), lambda b, n: (b, layer, 0, 0)),
                  tab(same), tab(same), tab(prev), tab(prev), tab(nxt), tab(nxt)],
        out_specs=pl.BlockSpec((blk, qw), lambda b, n: (b * nb + n, 0)),
        out_shape=jax.ShapeDtypeStruct((batch * seq, qw), BF16),
        compiler_params=_cp(("arbitrary", "arbitrary"), 40),
        name="attn_a_lat",
    )(sink, u_attn, u_attn, u_attn, u_attn, u_attn, u_attn, u_attn, cache_k, cache_v,
      rope_c, rope_s, rope_c, rope_s, rope_c, rope_s)


def _lambda(lam_ref, lam_init):
    lw = lam_ref[...]
    return (jnp.exp(jnp.sum(lw[0:1] * lw[1:2], axis=-1, keepdims=True))
            - jnp.exp(jnp.sum(lw[2:3] * lw[3:4], axis=-1, keepdims=True)) + lam_init)


def _softmax_parts(parts):
    m = functools.reduce(jnp.maximum, [jnp.max(s, axis=-1, keepdims=True) for s in parts])
    ps = [jnp.exp(s - m) for s in parts]
    inv = 1.0 / functools.reduce(lambda a, b: a + b, [jnp.sum(p, axis=-1, keepdims=True) for p in ps])
    return [p * inv for p in ps]


def _attn_b_ctx_body(lam_ref, g_ref, q_ref, k_ref, v_ref, o_ref, *, lam_init):
    hd = HD // 2
    scale = hd ** -0.5
    lam = _lambda(lam_ref, lam_init)
    for h in range(q_ref.shape[1] // HD):
        hs = slice(h * HD, (h + 1) * HD)
        q = q_ref[:, hs].astype(BF16)
        k = k_ref[:, hs].astype(BF16)
        p0, = _softmax_parts([_dot_nt(q[:, :hd], k[:, :hd]) * scale])
        p1, = _softmax_parts([_dot_nt(q[:, hd:], k[:, hd:]) * scale])
        o = _dot((p0 - lam * p1).astype(BF16), v_ref[:, hs].astype(BF16))
        o_ref[:, hs] = (_rms_lanes(o, g_ref[...]) * (1.0 - lam_init)).astype(BF16)


def attn_b_ctx(u_attn, b_lambda, b_subln, lam_init, batch, seq, heads, col0):
    w = heads * HD
    qcol = col0 // w
    return pl.pallas_call(
        functools.partial(_attn_b_ctx_body, lam_init=lam_init),
        grid=(batch,),
        in_specs=[pl.BlockSpec(b_lambda.shape, lambda b: (0, 0)),
                  pl.BlockSpec((1, HD), lambda b: (0, 0)),
                  pl.BlockSpec((seq, w), lambda b: (b, qcol)),
                  pl.BlockSpec((seq, w), lambda b: (b, qcol + 1)),
                  pl.BlockSpec((seq, w), lambda b: (b, qcol + 2))],
        out_specs=pl.BlockSpec((seq, w), lambda b: (b, 0)),
        out_shape=jax.ShapeDtypeStruct((batch * seq, w), BF16),
        compiler_params=_cp(("arbitrary",), 32),
        name="attn_b_ctx",
    )(b_lambda, b_subln.reshape(1, HD), u_attn, u_attn, u_attn)


def _attn_b_lat_body(lam_ref, g_ref, q_ref, k_ref, v_ref, kc_ref, vc_ref, cq_ref, sq_ref, ck_ref, sk_ref,
                     o_ref, k_scr, kc_scr, v_scr, *, lam_init):
    hd = HD // 2
    scale = hd ** -0.5
    nq = pl.program_id(2)

    @pl.when(nq == 0)
    def _():
        k_scr[...] = _rope(k_ref[...], ck_ref[...], sk_ref[...]).T.astype(BF16)
        kc_scr[...] = kc_ref[...].T.astype(BF16)
        v_scr[...] = v_ref[...].astype(BF16)

    lam = _lambda(lam_ref, lam_init)
    k = k_scr[...]
    v = v_scr[...]
    kc = kc_scr[...]
    vc = vc_ref[...].astype(BF16)
    q = _rope(q_ref[...], cq_ref[...], sq_ref[...]) * (scale * LOG2E)
    lane = lax.broadcasted_iota(jnp.int32, q.shape, 1)
    outs = []
    for c in range(2):
        qc = jnp.where((lane >= hd) == (c == 1), q, 0.0).astype(BF16)
        s_ctx = _dot(qc, kc)
        s_lat = _dot(qc, k)
        m = jnp.maximum(jnp.max(s_ctx, axis=-1, keepdims=True), jnp.max(s_lat, axis=-1, keepdims=True))
        p_ctx = jnp.exp2(s_ctx - m)
        p_lat = jnp.exp2(s_lat - m)
        den = jnp.sum(p_ctx, axis=-1, keepdims=True) + jnp.sum(p_lat, axis=-1, keepdims=True)
        outs.append((_dot(p_ctx.astype(BF16), vc) + _dot(p_lat.astype(BF16), v)) / den)
    o = outs[0] - lam * outs[1]
    o_ref[...] = (_rms_lanes(o, g_ref[...]) * (1.0 - lam_init)).astype(BF16)


def attn_b_lat(u_attn, cache_k, cache_v, b_lambda, b_subln, rope_c, rope_s, lam_init, layer,
               batch, seq, row0, heads, col0, tq=256):
    qcol = col0 // HD
    kcol = qcol + heads
    vcol = kcol + heads
    nq = seq // tq
    past = cache_k.shape[2]
    rq0 = row0 // tq
    rs0 = row0 // seq
    return pl.pallas_call(
        functools.partial(_attn_b_lat_body, lam_init=lam_init),
        grid=(batch, heads, nq),
        in_specs=[pl.BlockSpec(b_lambda.shape, lambda b, h, n: (0, 0)),
                  pl.BlockSpec((1, HD), lambda b, h, n: (0, 0)),
                  pl.BlockSpec((tq, HD), lambda b, h, n: (rq0 + b * nq + n, qcol + h)),
                  pl.BlockSpec((seq, HD), lambda b, h, n: (rs0 + b, kcol + h)),
                  pl.BlockSpec((seq, HD), lambda b, h, n: (rs0 + b, vcol + h)),
                  pl.BlockSpec((None, None, past, HD), lambda b, h, n: (b, layer, 0, h)),
                  pl.BlockSpec((None, None, past, HD), lambda b, h, n: (b, layer, 0, h)),
                  pl.BlockSpec((tq, HD), lambda b, h, n: (n, 0)),
                  pl.BlockSpec((tq, HD), lambda b, h, n: (n, 0)),
                  pl.BlockSpec((seq, HD), lambda b, h, n: (0, 0)),
                  pl.BlockSpec((seq, HD), lambda b, h, n: (0, 0))],
        out_specs=pl.BlockSpec((tq, HD), lambda b, h, n: (b * nq + n, h)),
        out_shape=jax.ShapeDtypeStruct((batch * seq, heads * HD), BF16),
        scratch_shapes=[pltpu.VMEM((HD, seq), BF16), pltpu.VMEM((HD, past), BF16), pltpu.VMEM((seq, HD), BF16)],
        compiler_params=_cp(("arbitrary", "arbitrary", "arbitrary"), 40),
        name="attn_b_lat",
    )(b_lambda, b_subln.reshape(1, HD), u_attn, u_attn, u_attn, cache_k, cache_v,
      rope_c, rope_s, rope_c, rope_s)


def dft_half_tables(l):
    h = l // 2
    r = jnp.arange(l, dtype=jnp.int32)[:, None]
    f = r % h
    quarter = jnp.where(r >= h, l, 0)
    j = jnp.arange(h, dtype=jnp.int32)[None, :]

    def table(pos):
        m = ((2 * f + 1) * pos + quarter) % (4 * l)
        t = jnp.cos(m.astype(F32) * (math.pi / (2 * l)))
        hi = t.astype(BF16)
        return hi, (t - hi.astype(F32)).astype(BF16)

    (te, te_lo), (to, to_lo) = table(2 * j), table(2 * j + 1)
    return te, te_lo, to, to_lo, te.T, to.T


def parity_order(l):
    return jnp.concatenate([jnp.arange(0, l, 2), jnp.arange(1, l, 2)])


def hyena_feats(l):
    t = jnp.linspace(0.0, 1.0, l, dtype=F32)[:, None]
    bands = (HY_EMB - 1) // 2
    w = 2.0 * math.pi * jnp.arange(l, dtype=F32)[:, None] / l
    fr = jnp.linspace(1e-4, bands - 1, bands, dtype=F32)[None, :]
    feats = jnp.concatenate([t, jnp.cos(fr * w), -jnp.sin(fr * w)], axis=-1)
    return jnp.pad(feats, ((0, 0), (0, HD - HY_EMB)))


def hyena_decay(l, width):
    t = jnp.linspace(0.0, 1.0, l, dtype=F32)[:, None]
    deltas = jnp.abs(jnp.linspace(math.log(HY_TARGET) / HY_SLOW, math.log(HY_TARGET) / HY_FAST, width, dtype=F32))
    return jnp.exp(-t * deltas)


def _hy_time_body(feat_ref, dec_ref, w1_ref, b1_ref, w2_ref, b2_ref, w3_ref, fr_ref, kh_ref, kl_ref, nrm_scr,
                  *, width, l):
    ps = pl.program_id(2)
    c = pl.program_id(3)
    fr = fr_ref[...]
    h = jnp.sin(fr[0:1] * (_dot3(feat_ref[...], w1_ref[...]) + b1_ref[...]))
    h = jnp.sin(fr[1:2] * (_dot3(h, w2_ref[...]) + b2_ref[...]))
    dec = dec_ref[...]
    row = lax.broadcasted_iota(jnp.int32, dec.shape, 0)
    fwd = _dot3(h, w3_ref[:, :width]) * dec
    bwd = jnp.where((row == 0) & (c == 0), 0.0, _dot3(h, w3_ref[:, width:]) * dec)

    @pl.when((ps == 0) & (c == 0))
    def _():
        nrm_scr[...] = jnp.zeros_like(nrm_scr)

    @pl.when(ps == 0)
    def _():
        nrm_scr[...] += (jnp.sum(jnp.abs(fwd), axis=0, keepdims=True)
                         + jnp.sum(jnp.abs(bwd), axis=0, keepdims=True))

    @pl.when(ps == 1)
    def _():
        inv = 1.0 / ((nrm_scr[...] + EPS) * l)
        for part, val in ((0, fwd * inv), (1, bwd * inv)):
            hi, lo = _split(val)
            kh_ref[:, part * width:(part + 1) * width] = hi
            kl_ref[:, part * width:(part + 1) * width] = lo


def _hy_spec_body(ter_h, tei_h, tor_h, toi_h, ter_l, tei_l, tor_l, toi_l, keh_ref, kel_ref, koh_ref, kol_ref,
                   o_ref, *, width):
    keh, kel, koh, kol = keh_ref[...], kel_ref[...], koh_ref[...], kol_ref[...]

    def dot3(th, tl, kh, kl):
        t = th[...]
        return _dot(t, kh) + _dot(t, kl) + _dot(tl[...], kh)

    p_re, p_im = dot3(ter_h, ter_l, keh, kel), dot3(tei_h, tei_l, keh, kel)
    q_re, q_im = dot3(tor_h, tor_l, koh, kol), dot3(toi_h, toi_l, koh, kol)
    o_ref[0] = (p_re + q_re)[:, :width] + (p_re + q_re)[:, width:]
    o_ref[1] = (p_im + q_im)[:, :width] - (p_im + q_im)[:, width:]
    o_ref[2] = (p_re - q_re)[:, :width] + (p_re - q_re)[:, width:]
    o_ref[3] = (q_im - p_im)[:, :width] - (q_im - p_im)[:, width:]


def hyena_filters(l, hy_w1p, hy_b1, hy_w2, hy_b2, hy_w3, hy_freq, tabs, width, rc=512, tr=256):
    te, te_lo, to, to_lo = tabs[:4]
    depth = hy_w3.shape[0]
    ffn_w = hy_w2.shape[1]
    half = l // 2
    rc = min(rc, l)
    tr = min(tr, half)
    nr = half // tr
    order = parity_order(l)
    feats = hyena_feats(l)[order]
    dec = hyena_decay(l, width)[order]
    wspec = lambda shape: pl.BlockSpec((None,) + shape, lambda d, o, p, c: (d, 0, 0))
    kspec = pl.BlockSpec((None, rc, 2 * width), lambda d, o, p, c: (d, c * p, o))
    k_hi, k_lo = pl.pallas_call(
        functools.partial(_hy_time_body, width=width, l=l),
        grid=(depth, 2, 2, l // rc),
        in_specs=[pl.BlockSpec((rc, HD), lambda d, o, p, c: (c, 0)),
                  pl.BlockSpec((rc, width), lambda d, o, p, c: (c, 0)),
                  wspec((HD, ffn_w)), wspec((1, ffn_w)), wspec((ffn_w, ffn_w)), wspec((1, ffn_w)),
                  pl.BlockSpec((None, ffn_w, 2 * width), lambda d, o, p, c: (d, 0, o)),
                  wspec((2, ffn_w))],
        out_specs=[kspec, kspec],
        out_shape=[jax.ShapeDtypeStruct((depth, l, 4 * width), BF16)] * 2,
        scratch_shapes=[pltpu.VMEM((1, width), F32)],
        compiler_params=_cp(("arbitrary",) * 4, 32),
        name="hyena_time_filters",
    )(feats, dec, hy_w1p, hy_b1.reshape(depth, 1, ffn_w), hy_w2, hy_b2.reshape(depth, 1, ffn_w), hy_w3, hy_freq)
    re_spec = pl.BlockSpec((tr, half), lambda d, o, r: (r, 0))
    im_spec = pl.BlockSpec((tr, half), lambda d, o, r: (nr + r, 0))
    even = pl.BlockSpec((None, half, 2 * width), lambda d, o, r: (d, 0, o))
    odd = pl.BlockSpec((None, half, 2 * width), lambda d, o, r: (d, 1, o))
    return pl.pallas_call(
        functools.partial(_hy_spec_body, width=width),
        grid=(depth, 2, nr),
        in_specs=[re_spec, im_spec, re_spec, im_spec, re_spec, im_spec, re_spec, im_spec, even, even, odd, odd],
        out_specs=pl.BlockSpec((None, None, 4, tr, width), lambda d, o, r: (d, o, 0, r, 0)),
        out_shape=jax.ShapeDtypeStruct((depth, 2, 4, half, width), F32),
        compiler_params=_cp(("arbitrary",) * 3, 48),
        name="hyena_spectra",
    )(te, te, to, to, te_lo, te_lo, to_lo, to_lo, k_hi, k_lo, k_hi, k_lo)


def _short_conv(u, w, b):
    l = u.shape[0]
    row = lax.broadcasted_iota(jnp.int32, u.shape, 0)
    up = jnp.where(row == 0, 0.0, pltpu.roll(u, 1, 0))
    dn = jnp.where(row == l - 1, 0.0, pltpu.roll(u, l - 1, 0))
    return up * w[0:1] + u * w[1:2] + dn * w[2:3] + b


def _hyena_body(hv_ref, h1_ref, h2_ref, wv_ref, w1_ref, w2_ref, bv_ref, b1_ref, b2_ref, bias_ref,
                ter_ref, tei_ref, tor_ref, toi_ref, ger_ref, gei_ref, gor_ref, goi_ref, k_ref, o_ref,
                z_scr, ze_scr, zo_scr, acc_scr, *, nf):
    ph = pl.program_id(1)
    j = pl.program_id(2)
    half = ze_scr.shape[0]
    even = pl.ds(0, half, stride=2)
    odd = pl.ds(1, half, stride=2)
    tiles = [(t, slice(t * HD, (t + 1) * HD)) for t in range(z_scr.shape[0])]
    gather = lambda scr: jnp.concatenate([scr[t] for t, _ in tiles], axis=1)

    def set_input(z):
        for t, cols in tiles:
            z_scr[t] = z[:, cols]
            ze_scr[:, cols] = z_scr[t, even, :].astype(BF16)
            zo_scr[:, cols] = z_scr[t, odd, :].astype(BF16)

    @pl.when((ph == 0) & (j == 0))
    def _():
        set_input(_short_conv(hv_ref[...].astype(F32), wv_ref[...], bv_ref[...]))

    @pl.when(j == 0)
    def _():
        acc_scr[...] = jnp.zeros_like(acc_scr)

    ze, zo = ze_scr[...], zo_scr[...]
    p_re, p_im = _dot(ter_ref[...], ze), _dot(tei_ref[...], ze)
    q_re, q_im = _dot(tor_ref[...], zo), _dot(toi_ref[...], zo)
    z1_re, z1_im, z2_re, z2_im = p_re + q_re, p_im + q_im, p_re - q_re, q_im - p_im
    k1_re, k1_im, k2_re, k2_im = k_ref[0], k_ref[1], k_ref[2], k_ref[3]
    y1_re = z1_re * k1_re - z1_im * k1_im
    y1_im = z1_re * k1_im + z1_im * k1_re
    y2_re = z2_re * k2_re - z2_im * k2_im
    y2_im = z2_re * k2_im + z2_im * k2_re
    y_even = (_dot(ger_ref[...], (y1_re + y2_re).astype(BF16))
              + _dot(gei_ref[...], (y1_im - y2_im).astype(BF16)))
    y_odd = (_dot(gor_ref[...], (y1_re - y2_re).astype(BF16))
             + _dot(goi_ref[...], (y1_im + y2_im).astype(BF16)))
    for t, cols in tiles:
        acc_scr[t, even, :] += y_even[:, cols]
        acc_scr[t, odd, :] += y_odd[:, cols]

    @pl.when((j == nf - 1) & (ph == 0))
    def _():
        x1 = _short_conv(h1_ref[...].astype(F32), w1_ref[...], b1_ref[...])
        set_input(x1 * (gather(acc_scr) + gather(z_scr) * bias_ref[0:1]))

    @pl.when((j == nf - 1) & (ph == 1))
    def _():
        x2 = _short_conv(h2_ref[...].astype(F32), w2_ref[...], b2_ref[...])
        o_ref[...] = (x2 * (gather(acc_scr) + gather(z_scr) * bias_ref[1:2])).astype(BF16)


def hyena(u_hy, conv_w, conv_b, bias, kf, tabs, layer, batch, seq, row0, width, tfq=256):
    te, _, to, _, ge, go = tabs
    half = seq // 2
    tfq = min(tfq, half)
    nf = half // tfq
    rb = row0 // seq
    hspec = lambda part: pl.BlockSpec((seq, width), lambda b, p, j: (rb + b, part))
    wspec = lambda part: pl.BlockSpec((3, width), lambda b, p, j: (0, part))
    bspec = lambda part: pl.BlockSpec((1, width), lambda b, p, j: (0, part))
    t_re = pl.BlockSpec((tfq, half), lambda b, p, j: (j, 0))
    t_im = pl.BlockSpec((tfq, half), lambda b, p, j: (nf + j, 0))
    g_re = pl.BlockSpec((half, tfq), lambda b, p, j: (0, j))
    g_im = pl.BlockSpec((half, tfq), lambda b, p, j: (0, nf + j))
    return pl.pallas_call(
        functools.partial(_hyena_body, nf=nf),
        grid=(batch, 2, nf),
        in_specs=[hspec(0), hspec(1), hspec(2), wspec(0), wspec(1), wspec(2), bspec(0), bspec(1), bspec(2),
                  pl.BlockSpec((2, width), lambda b, p, j: (0, 0)),
                  t_re, t_im, t_re, t_im, g_re, g_im, g_re, g_im,
                  pl.BlockSpec((None, None, 4, tfq, width), lambda b, p, j: (layer, p, 0, j, 0))],
        out_specs=pl.BlockSpec((seq, width), lambda b, p, j: (b, 0)),
        out_shape=jax.ShapeDtypeStruct((batch * seq, width), BF16),
        scratch_shapes=[pltpu.VMEM((width // HD, seq, HD), F32), pltpu.VMEM((half, width), BF16),
                        pltpu.VMEM((half, width), BF16), pltpu.VMEM((width // HD, seq, HD), F32)],
        compiler_params=_cp(("arbitrary",) * 3, 56),
        name="hyena",
    )(u_hy, u_hy, u_hy, conv_w, conv_w, conv_w, conv_b, conv_b, conv_b, bias,
      te, te, to, to, ge, ge, go, go, kf)


def _hgrn_pair_level(t, s, fwd):
    x = t ^ s
    lvl = jnp.where(x == 0, 0, 32 - lax.clz(x))
    used = (s <= t) if fwd else (s >= t)
    return jnp.where(used, lvl, -1)


def _hgrn_midpoint(b3, blk, fwd):
    rows = b3.shape[1]
    r0 = blk // 2 - 1 if fwd else blk // 2
    ref = b3[:, r0:r0 + 1, :]
    if rows > blk:
        pos = lax.broadcasted_iota(jnp.int32, b3.shape, 1)
        for j in range(1, rows // blk):
            ref = jnp.where(pos >= j * blk, b3[:, j * blk + r0:j * blk + r0 + 1, :], ref)
    return ref


def _hgrn_chunk(q, k, g, v, st, tri, lvl, fwd):
    c = CHUNK
    gh, gl = _split(g)
    b = _dot(tri, gh) + _dot(tri, gl)
    tot = b[c - 1:c] if fwd else b[0:1]
    vb = v.astype(BF16)
    o = _dot_nt((q * jnp.exp2(b)).astype(BF16), st.astype(BF16))
    kd = (k * jnp.exp2(tot - b)).astype(BF16)
    st_new = st * jnp.exp2(tot) + _dot_tn(vb, kd)

    a = jnp.where(lvl == 0, jnp.sum(q * k, axis=-1, keepdims=True), 0.0)
    a = jnp.where(lvl == 1, _dot_nt((q * jnp.exp2(g)).astype(BF16), k.astype(BF16)), a)
    blk, m = 4, 2
    while blk <= c:
        rows = max(blk, SUB)
        b3, q3, k3 = (x.reshape(c // rows, rows, HD) for x in (b, q, k))
        e = jnp.exp2(-jnp.abs(b3 - _hgrn_midpoint(b3, blk, fwd)))
        qt = (q3 * e).reshape(c, HD).astype(BF16)
        kt = (k3 * e).reshape(c, HD).astype(BF16)
        a = jnp.where(lvl == m, _dot_nt(qt, kt), a)
        blk, m = blk * 2, m + 1
    return o + _dot(a.astype(BF16), vb), st_new


def _hgrn_gate(z, lb):
    e = jnp.exp(-jnp.abs(z))
    r = 1.0 / (1.0 + e)
    pos = z >= 0
    sig_pos = jnp.where(pos, r, e * r)
    sig_neg = jnp.where(pos, e * r, r)
    f = sig_pos + jnp.maximum(lb, LB_FLOOR) * sig_neg
    return jnp.log(f) * LOG2E, (1.0 - lb) * sig_neg


def _hgrn_lower_bound(lb_ref, layer, depth, direction):
    x = lb_ref[...]
    rows = [x[2 * i + direction:2 * i + direction + 1] for i in range(depth)]
    m = functools.reduce(jnp.maximum, rows)
    es = [jnp.exp(r - m) for r in rows]
    tot = functools.reduce(lambda a, b: a + b, es)
    cum = jnp.zeros_like(tot)
    for i in range(1, layer + 1):
        cum = cum + es[i]
    return jnp.maximum(cum / tot, 0.0)


def _hgrn_body(*refs, layer, depth, has_state, emit_state):
    it = iter(refs)
    lb_ref, ng_ref, q_ref, ff_ref, fb_ref, i_ref, g_ref = (next(it) for _ in range(7))
    s0_ref = next(it) if has_state else None
    o_ref = next(it)
    so_ref = next(it) if emit_state else None
    of_scr, ob_scr, st_scr, tri_scr, lvl_scr = (next(it) for _ in range(5))
    nc = q_ref.shape[0] // CHUNK
    hp = q_ref.shape[1] // HD
    ti = lax.broadcasted_iota(jnp.int32, (CHUNK, CHUNK), 0)
    si = lax.broadcasted_iota(jnp.int32, (CHUNK, CHUNK), 1)
    chains = [(h, d) for h in range(hp) for d in range(2)]
    lbs = {}
    for d in range(2):
        tri_scr[d] = jnp.where((si <= ti) if d == 0 else (si >= ti), 1.0, 0.0).astype(BF16)
        lvl_scr[d] = _hgrn_pair_level(ti, si, d == 0)
        lb_all = _hgrn_lower_bound(lb_ref, layer, depth, d)
        for h in range(hp):
            lbs[h, d] = lb_all[:, h * HD:(h + 1) * HD]
            st_scr[2 * h + d] = s0_ref[d, h].T if has_state else jnp.zeros((HD, HD), F32)

    def step(i, carry):
        for h, d in chains:
            fwd = d == 0
            ci = i if fwd else nc - 1 - i
            sl = pl.ds(pl.multiple_of(ci * CHUNK, CHUNK), CHUNK)
            cs = slice(h * HD, (h + 1) * HD)
            xq = q_ref[sl, cs].astype(F32)
            log_f, kk = _hgrn_gate((ff_ref if fwd else fb_ref)[sl, cs].astype(F32), lbs[h, d])
            o, st = _hgrn_chunk(xq * _sigmoid(xq), kk, log_f, i_ref[sl, cs].astype(F32), st_scr[2 * h + d],
                                tri_scr[d], lvl_scr[d], fwd)
            st_scr[2 * h + d] = st
            (of_scr if fwd else ob_scr)[sl, cs] = o
        return carry

    lax.fori_loop(0, nc, step, 0, unroll=4)
    for h, d in chains:
        if emit_state:
            so_ref[d, h] = st_scr[2 * h + d].T
    for h in range(hp):
        cs = slice(h * HD, (h + 1) * HD)
        xg = g_ref[:, cs].astype(F32)
        o_ref[:, cs] = (_rms_lanes(of_scr[:, cs] + ob_scr[:, cs], ng_ref[...]) * (xg * _sigmoid(xg))).astype(BF16)


def hgrn(u_hg, hg_lb2, hg_norm, state, layer, depth, batch, seq, row0, heads, emit_state, col0=0, hp=4):
    rb = row0 // seq
    ng = heads // hp
    w = hp * HD
    cb = col0 // w
    part = lambda p: pl.BlockSpec((seq, w), lambda b, h: (rb + b, cb + p * ng + h))
    in_specs = [pl.BlockSpec((2 * depth, w), lambda b, h: (0, h)),
                pl.BlockSpec((1, HD), lambda b, h: (0, 0)),
                part(0), part(1), part(2), part(3), part(4)]
    args = [hg_lb2, hg_norm.reshape(1, HD), u_hg, u_hg, u_hg, u_hg, u_hg]
    if state is not None:
        in_specs.append(pl.BlockSpec((None, None, 2, hp, HD, HD), lambda b, h: (b, layer, 0, h, 0, 0)))
        args.append(state)
    out_specs = [pl.BlockSpec((seq, w), lambda b, h: (b, h))]
    out_shape = [jax.ShapeDtypeStruct((batch * seq, heads * HD), BF16)]
    if emit_state:
        out_specs.append(pl.BlockSpec((None, 2, hp, HD, HD), lambda b, h: (b, 0, h, 0, 0)))
        out_shape.append(jax.ShapeDtypeStruct((batch, 2, heads, HD, HD), F32))
    return pl.pallas_call(
        functools.partial(_hgrn_body, layer=layer, depth=depth, has_state=state is not None,
                          emit_state=emit_state),
        grid=(batch, ng),
        in_specs=in_specs,
        out_specs=out_specs,
        out_shape=out_shape,
        scratch_shapes=[pltpu.VMEM((seq, w), F32), pltpu.VMEM((seq, w), F32),
                        pltpu.VMEM((2 * hp, HD, HD), F32),
                        pltpu.VMEM((2, CHUNK, CHUNK), BF16), pltpu.VMEM((2, CHUNK, CHUNK), jnp.int32)],
        compiler_params=_cp(("arbitrary", "arbitrary"), 48),
        name="hgrn",
    )(*args)


def _merge_body(*refs, n_ctx_tiles):
    o_refs, (gt_ref, x_ref, mod_ref, wb_ref, wo_ref, out_ref) = refs[:2 * N_BRANCH], refs[2 * N_BRANCH:]
    d = x_ref.shape[1]
    is_ctx = pl.program_id(0) < n_ctx_tiles
    acc = None
    for n in range(N_BRANCH):
        o = jnp.where(is_ctx, o_refs[2 * n][...], o_refs[2 * n + 1][...])
        y = gt_ref[:, n * d:(n + 1) * d].astype(F32) * _dot(o, wb_ref[n])
        acc = y if acc is None else acc + y
    out = _dot(acc.astype(BF16), wo_ref[...])
    out_ref[...] = x_ref[...] + mod_ref[...][5:6] * out


def merge(branch_outs, gates, x, mod_l, w_branch, w_out, t_ctx, l_lat, tm=256):
    t, d = x.shape
    mw = w_branch.shape[1]
    nct = t_ctx // tm
    row = functools.partial(_mod_row, tm=tm, t_ctx=t_ctx, l_lat=l_lat)
    cspec = pl.BlockSpec((tm, mw), lambda i: (jnp.minimum(i, nct - 1), 0))
    lspec = pl.BlockSpec((tm, mw), lambda i: (jnp.maximum(i - nct, 0), 0))
    return pl.pallas_call(
        functools.partial(_merge_body, n_ctx_tiles=nct),
        grid=(t // tm,),
        in_specs=[cspec, lspec] * N_BRANCH + [
                  pl.BlockSpec((tm, N_BRANCH * d), lambda i: (i, 0)),
                  pl.BlockSpec((tm, d), lambda i: (i, 0)),
                  pl.BlockSpec((None, N_MOD, d), lambda i: (row(i), 0, 0)),
                  pl.BlockSpec((N_BRANCH, mw, d), lambda i: (0, 0, 0), pipeline_mode=pl.Buffered(1)),
                  pl.BlockSpec((d, d), lambda i: (0, 0), pipeline_mode=pl.Buffered(1))],
        out_specs=pl.BlockSpec((tm, d), lambda i: (i, 0)),
        out_shape=jax.ShapeDtypeStruct((t, d), F32),
        compiler_params=_cp(("arbitrary",), 56),
        name="merge",
    )(*[o for pair in branch_outs for o in pair], gates, x, mod_l, w_branch, w_out)


def _final_norm_body(x_ref, g_ref, oc_ref, ol_ref, *, n_ctx_tiles):
    i = pl.program_id(0)

    @pl.when(i < n_ctx_tiles)
    def _():
        oc_ref[...] = _rms_lanes(x_ref[...], g_ref[...])

    @pl.when(i >= n_ctx_tiles)
    def _():
        ol_ref[...] = _rms_lanes(x_ref[...], g_ref[...])


def final_norm(x, g, t_ctx, tm=512):
    t, d = x.shape
    nct = t_ctx // tm
    return pl.pallas_call(
        functools.partial(_final_norm_body, n_ctx_tiles=nct),
        grid=(t // tm,),
        in_specs=[pl.BlockSpec((tm, d), lambda i: (i, 0)), pl.BlockSpec((1, d), lambda i: (0, 0))],
        out_specs=[pl.BlockSpec((tm, d), lambda i: (jnp.minimum(i, nct - 1), 0)),
                   pl.BlockSpec((tm, d), lambda i: (jnp.maximum(i - nct, 0), 0))],
        out_shape=[jax.ShapeDtypeStruct((t_ctx, d), F32), jax.ShapeDtypeStruct((t - t_ctx, d), F32)],
        compiler_params=_cp(("arbitrary",), 32),
        name="final_norm",
    )(x, g.reshape(1, d))


def kernel(x_prompt, x_sample, c, cache_a_k, cache_a_v, cache_b_k, cache_b_v, state_hgrn, c_ctx, w_mod, b_mod, norm_g, w_ffn1_gu, w_ffn1_down, w_ffn2_gu, w_ffn2_down, w_in, w_branch, w_out, a_sink, b_lambda, b_subln, hy_conv_w, hy_conv_b, hy_w1, hy_b1, hy_w2, hy_b2, hy_w3, hy_freq, hy_bias, hg_lb, hg_norm, final_g):
    batch, seq, d = x_prompt.shape
    dec_batch, dec_seq, _ = x_sample.shape
    depth = w_mod.shape[0]
    mix_w = w_branch.shape[2]
    a_heads = a_sink.shape[1]
    a_kvh = cache_a_k.shape[3]
    a_group = a_heads // a_kvh
    b_heads = cache_b_k.shape[3]
    hg_heads = state_hgrn.shape[3]
    t_ctx, t_lat = batch * seq, dec_batch * dec_seq

    n_attn = (a_heads + 2 * a_kvh + 3 * b_heads) * HD
    b_col0 = (a_heads + 2 * a_kvh) * HD

    x = jnp.concatenate([x_prompt.reshape(t_ctx, d), x_sample.reshape(t_lat, d)], axis=0)
    cond = jnp.concatenate([c_ctx[None, :], c], axis=0)
    cond = jnp.pad(cond, ((0, MOD_ROWS - cond.shape[0]), (0, 0)))
    mod = modulation(cond, w_mod, b_mod).reshape(depth, MOD_ROWS, N_MOD, d)

    rope_a = rope_tables(dec_seq, HD, 1)
    rope_b = rope_tables(dec_seq, HD // 2, 2)
    tabs_ctx = dft_half_tables(seq)
    tabs_lat = dft_half_tables(dec_seq)
    w1p = jnp.pad(hy_w1, ((0, 0), (0, HD - hy_w1.shape[1]), (0, 0)))
    kf_ctx = hyena_filters(seq, w1p, hy_b1, hy_w2, hy_b2, hy_w3, hy_freq, tabs_ctx, mix_w)
    kf_lat = hyena_filters(dec_seq, w1p, hy_b1, hy_w2, hy_b2, hy_w3, hy_freq, tabs_lat, mix_w)

    cak = cache_a_k.reshape(dec_batch, depth, -1, a_kvh * HD)
    cav = cache_a_v.reshape(dec_batch, depth, -1, a_kvh * HD)
    cbk = cache_b_k.reshape(dec_batch, depth, -1, b_heads * HD)
    cbv = cache_b_v.reshape(dec_batch, depth, -1, b_heads * HD)
    hg_lb2 = hg_lb.reshape(depth * 2, mix_w)

    ak_l, av_l, bk_l, bv_l, st_l = [], [], [], [], []
    for l in range(depth):
        lam_init = 0.8 - 0.6 * math.exp(-0.3 * l)
        mod_l = mod[l]
        x, h_mix = ffn(x, mod_l, norm_g[l, 0:1], w_ffn1_gu[l].astype(BF16), w_ffn1_down[l].astype(BF16), 0,
                       t_ctx, dec_seq, next_g_row=norm_g[l, 1:2])
        n_rec = n_attn + 8 * mix_w
        u_attn = proj(h_mix, w_in[l, :, :n_attn].astype(BF16), F32, False)
        u_rec = proj(h_mix, w_in[l, :, n_attn:n_rec].astype(BF16), BF16, False)
        gates = proj(h_mix, w_in[l, :, n_rec:].astype(BF16), BF16, True)

        uc = u_attn[:t_ctx]
        k0 = a_heads * HD
        ak_l.append(uc[:, k0:k0 + a_kvh * HD].reshape(batch, seq, a_kvh, HD))
        av_l.append(uc[:, k0 + a_kvh * HD:b_col0].reshape(batch, seq, a_kvh, HD))
        k1 = b_col0 + b_heads * HD
        bk_l.append(uc[:, k1:k1 + b_heads * HD].reshape(batch, seq, b_heads, 2, HD // 2))
        bv_l.append(uc[:, k1 + b_heads * HD:].reshape(batch, seq, b_heads, HD))

        oa_c = attn_a_ctx(u_attn, a_sink[l], batch, seq, a_kvh, a_group)
        oa_s = attn_a_lat(u_attn, cak, cav, a_sink[l], rope_a[0], rope_a[1], l, dec_batch, dec_seq, t_ctx,
                          a_kvh, a_group)
        ob_c = attn_b_ctx(u_attn, b_lambda[l], b_subln[l], lam_init, batch, seq, b_heads, b_col0)
        ob_s = attn_b_lat(u_attn, cbk, cbv, b_lambda[l], b_subln[l], rope_b[0], rope_b[1], lam_init, l,
                          dec_batch, dec_seq, t_ctx, b_heads, b_col0)
        cb = hy_conv_b[l].reshape(1, -1)
        oc_c = hyena(u_rec, hy_conv_w[l], cb, hy_bias[l], kf_ctx, tabs_ctx, l, batch, seq, 0, mix_w)
        oc_s = hyena(u_rec, hy_conv_w[l], cb, hy_bias[l], kf_lat, tabs_lat, l, dec_batch, dec_seq, t_ctx, mix_w)
        od_c, st = hgrn(u_rec, hg_lb2, hg_norm[l], None, l, depth, batch, seq, 0, hg_heads, True, col0=3 * mix_w)
        od_s, = hgrn(u_rec, hg_lb2, hg_norm[l], state_hgrn, l, depth, dec_batch, dec_seq, t_ctx, hg_heads, False,
                     col0=3 * mix_w)
        st_l.append(st)

        x = merge(((oa_c, oa_s), (ob_c, ob_s), (oc_c, oc_s), (od_c, od_s)), gates, x, mod_l,
                  w_branch[l].astype(BF16), w_out[l].astype(BF16), t_ctx, dec_seq)
        x = ffn(x, mod_l, norm_g[l, 2:3], w_ffn2_gu[l].astype(BF16), w_ffn2_down[l].astype(BF16), 2,
                t_ctx, dec_seq)

    y_ctx, y_lat = final_norm(x, final_g, t_ctx)
    return (y_ctx.reshape(batch, seq, d), y_lat.reshape(dec_batch, dec_seq, d),
            jnp.stack(ak_l, axis=1), jnp.stack(av_l, axis=1), jnp.stack(bk_l, axis=1), jnp.stack(bv_l, axis=1),
            jnp.stack(st_l, axis=1))
```

```python
import functools
import math

import jax
import jax.numpy as jnp
from jax import lax
from jax.experimental import pallas as pl
from jax.experimental.pallas import tpu as pltpu

F32 = jnp.float32
BF16 = jnp.bfloat16

EPS = 1e-6
NEG = -1e30
LB_FLOOR = 1e-30
ROPE_BASE = 10000.0
GRID_W = 64
N_MOD = 9
N_BRANCH = 4
HD = 128
A_WINDOW = 128
HY_EMB = 33
HY_TARGET, HY_FAST, HY_SLOW = 1e-2, 0.3, 1.5
CHUNK = 128
SUB = 8
LOG2E = 1.4426950408889634
MOD_ROWS = 16
FFN_TILE = 512
PROJ_TILE = 512
VMEM_MB = 2 ** 20


def _cp(sem, vmem_mb):
    return pltpu.CompilerParams(dimension_semantics=sem, vmem_limit_bytes=vmem_mb * VMEM_MB)


def _dot(a, b):
    return jnp.dot(a, b, preferred_element_type=F32)


def _dot_nt(a, b):
    return lax.dot_general(a, b, (((1,), (1,)), ((), ())), preferred_element_type=F32)


def _dot_tn(a, b):
    return lax.dot_general(a, b, (((0,), (0,)), ((), ())), preferred_element_type=F32)


def _split(a):
    hi = a.astype(BF16)
    lo = (a - hi.astype(F32)).astype(BF16)
    return hi, lo


def _dot3(a, b):
    ah, al = _split(a)
    bh, bl = _split(b)
    return _dot(ah, bh) + _dot(ah, bl) + _dot(al, bh)


def _sigmoid(x):
    return 0.5 * jnp.tanh(0.5 * x) + 0.5


def _norm_mod(x, g, sc, sh):
    y = x * lax.rsqrt(jnp.mean(x * x, axis=-1, keepdims=True) + EPS)
    return (y * g) * (1.0 + sc) + sh


def _norm_mod_cols(load, keep, reload, store, shape, g, sc, sh):
    n, d = shape
    tiles = [slice(t, t + HD) for t in range(0, d, HD)]
    ssq = jnp.zeros((n, HD), F32)
    for cols in tiles:
        x = load(cols)
        if keep is not None:
            keep(cols, x)
        ssq = ssq + x * x
    r = lax.rsqrt(jnp.sum(ssq, axis=-1, keepdims=True) * (1.0 / d) + EPS)
    gs = g * (1.0 + sc)
    for cols in tiles:
        store(cols, (reload(cols) * r) * gs[:, cols] + sh[:, cols])


def _rms_lanes(x, g):
    return x * lax.rsqrt(jnp.mean(x * x, axis=-1, keepdims=True) + EPS) * g


def _mod_body(c_ref, w_ref, b_ref, o_ref):
    c = c_ref[...]
    a = (c * _sigmoid(c)).astype(BF16)
    o_ref[...] = _dot(a, w_ref[...].astype(BF16)) + b_ref[...]


def modulation(cond, w_mod, b_mod, tn=1024):
    depth, d, n = w_mod.shape
    return pl.pallas_call(
        _mod_body,
        grid=(depth, n // tn),
        in_specs=[pl.BlockSpec((MOD_ROWS, d), lambda l, j: (0, 0)),
                  pl.BlockSpec((None, d, tn), lambda l, j: (l, 0, j)),
                  pl.BlockSpec((None, 1, tn), lambda l, j: (l, 0, j))],
        out_specs=pl.BlockSpec((None, MOD_ROWS, tn), lambda l, j: (l, 0, j)),
        out_shape=jax.ShapeDtypeStruct((depth, MOD_ROWS, n), F32),
        compiler_params=_cp(("arbitrary", "arbitrary"), 40),
        name="modulation",
    )(cond, w_mod, b_mod.reshape(depth, 1, n))


def _mod_row(i, tm, t_ctx, l_lat):
    start = i * tm
    return jnp.where(start < t_ctx, 0, 1 + (start - t_ctx) // l_lat)


def _ffn_body(*refs, sub, nj, next_sub):
    if next_sub is None:
        x_ref, mod_ref, g_ref, wg_ref, wu_ref, wd_ref, o_ref, h_scr = refs
    else:
        x_ref, mod_ref, g_ref, gn_ref, wg_ref, wu_ref, wd_ref, o_ref, hn_ref, h_scr = refs
    j = pl.program_id(1)

    @pl.when(j == 0)
    def _():
        m = mod_ref[...]
        load = lambda cols: x_ref[:, cols]

        def store(cols, h):
            h_scr[:, cols] = h.astype(BF16)

        _norm_mod_cols(load, None, load, store, x_ref.shape, g_ref[...],
                       m[3 * sub + 1:3 * sub + 2], m[3 * sub:3 * sub + 1])
        o_ref[...] = jnp.zeros_like(o_ref)

    h = h_scr[...]
    tf = wg_ref.shape[1]
    hc = tf // 2
    halves = [slice(c0, c0 + hc) for c0 in range(0, tf, hc)]
    ab = [(_dot(h, wg_ref[:, cs]), _dot(h, wu_ref[:, cs])) for cs in halves]
    acts = [(a * _sigmoid(a) * b).astype(BF16) for a, b in ab]
    o_ref[...] += functools.reduce(lambda p, q: p + q, [_dot(act, wd_ref[cs, :]) for act, cs in zip(acts, halves)])

    @pl.when(j == nj - 1)
    def _():
        m = mod_ref[...]
        ga = 0.5 * m[3 * sub + 2:3 * sub + 3]
        if next_sub is None:
            o_ref[...] = x_ref[...] + ga * o_ref[...]
        else:
            def keep(cols, y):
                o_ref[:, cols] = y

            def store(cols, h):
                hn_ref[:, cols] = h.astype(BF16)

            _norm_mod_cols(lambda cols: x_ref[:, cols] + ga[:, cols] * o_ref[:, cols], keep,
                           lambda cols: o_ref[:, cols], store, x_ref.shape, gn_ref[...],
                           m[3 * next_sub + 1:3 * next_sub + 2], m[3 * next_sub:3 * next_sub + 1])


def ffn(x, mod_l, g_row, w_gu, w_dn, sub, t_ctx, l_lat, next_g_row=None, tm=512, tf=FFN_TILE):
    t, d = x.shape
    dff = w_dn.shape[0]
    nj = dff // tf
    emit = next_g_row is not None
    row = functools.partial(_mod_row, tm=tm, t_ctx=t_ctx, l_lat=l_lat)
    gspec = pl.BlockSpec((1, d), lambda i, j: (0, 0))
    xspec = pl.BlockSpec((tm, d), lambda i, j: (i, 0))
    outs = pl.pallas_call(
        functools.partial(_ffn_body, sub=sub, nj=nj, next_sub=sub + 1 if emit else None),
        grid=(t // tm, nj),
        in_specs=[xspec,
                  pl.BlockSpec((None, N_MOD, d), lambda i, j: (row(i), 0, 0)),
                  gspec] + ([gspec] if emit else []) + [
                  pl.BlockSpec((d, tf), lambda i, j: (0, j)),
                  pl.BlockSpec((d, tf), lambda i, j: (0, j + nj)),
                  pl.BlockSpec((tf, d), lambda i, j: (j, 0))],
        out_specs=[xspec] + ([xspec] if emit else []),
        out_shape=[jax.ShapeDtypeStruct((t, d), F32)] + ([jax.ShapeDtypeStruct((t, d), BF16)] if emit else []),
        scratch_shapes=[pltpu.VMEM((tm, d), BF16)],
        compiler_params=_cp(("arbitrary", "arbitrary"), 56),
        name="ffn",
    )(x, mod_l, g_row, *([next_g_row] if emit else []), w_gu, w_gu, w_dn)
    return outs if emit else outs[0]


def _proj_body(h_ref, w_ref, o_ref, *, gate):
    r = _dot(h_ref[...], w_ref[...])
    o_ref[...] = _sigmoid(r.astype(o_ref.dtype)) if gate else r.astype(o_ref.dtype)


def proj(h, w, out_dtype, gate, tm=2048, tn=PROJ_TILE):
    t, d = h.shape
    n = w.shape[1]
    while t % tm:
        tm //= 2
    return pl.pallas_call(
        functools.partial(_proj_body, gate=gate),
        grid=(t // tm, n // tn),
        in_specs=[pl.BlockSpec((tm, d), lambda i, j: (i, 0)),
                  pl.BlockSpec((d, tn), lambda i, j: (0, j))],
        out_specs=pl.BlockSpec((tm, tn), lambda i, j: (i, j)),
        out_shape=jax.ShapeDtypeStruct((t, n), out_dtype),
        compiler_params=_cp(("arbitrary", "arbitrary"), 56),
        name="proj",
    )(h, w)


def rope_tables(l, rot_dim, reps):
    n_rows = l // GRID_W
    rows = jnp.broadcast_to(jnp.arange(n_rows, dtype=F32)[:, None], (n_rows, GRID_W)).reshape(-1)
    cols = jnp.broadcast_to(jnp.arange(GRID_W, dtype=F32)[None, :], (n_rows, GRID_W)).reshape(-1)
    axis_dim = rot_dim // 2
    inv = ROPE_BASE ** (-jnp.arange(0, axis_dim, 2, dtype=F32) / axis_dim)
    ang = jnp.concatenate([rows[:, None] * inv, cols[:, None] * inv], axis=-1)
    cos = jnp.repeat(jnp.cos(ang), 2, axis=-1)
    sin = jnp.repeat(jnp.sin(ang), 2, axis=-1)
    sign = jnp.tile(jnp.array([-1.0, 1.0], F32), rot_dim // 2)
    return jnp.tile(cos, (1, reps)), jnp.tile(sin * sign, (1, reps))


def _rope(x, c, s):
    lane = lax.broadcasted_iota(jnp.int32, x.shape, 1)
    nxt = pltpu.roll(x, x.shape[1] - 1, 1)
    prv = pltpu.roll(x, 1, 1)
    return x * c + jnp.where((lane & 1) == 0, nxt, prv) * s


def _attn_a_ctx_body(sink_ref, q_ref, k_ref, v_ref, o_ref, *, group, kvh):
    scale = HD ** -0.5
    for kh in range(kvh):
        k = k_ref[:, kh * HD:(kh + 1) * HD].astype(BF16)
        v = v_ref[:, kh * HD:(kh + 1) * HD].astype(BF16)
        for g in range(group):
            hs = slice((kh * group + g) * HD, (kh * group + g + 1) * HD)
            q = q_ref[:, hs].astype(BF16)
            s = _dot_nt(q, k) * scale
            sink = sink_ref[kh * group + g]
            m = jnp.maximum(jnp.max(s, axis=-1, keepdims=True), sink)
            p = jnp.exp(s - m)
            den = jnp.sum(p, axis=-1, keepdims=True) + jnp.exp(sink - m)
            o_ref[:, hs] = (_dot(p.astype(BF16), v) / den).astype(BF16)


def attn_a_ctx(u_attn, sink, batch, seq, kvh, group):
    qw = kvh * group * HD
    kw = kvh * HD
    kcol = qw // kw
    return pl.pallas_call(
        functools.partial(_attn_a_ctx_body, group=group, kvh=kvh),
        grid=(batch,),
        in_specs=[pl.BlockSpec(memory_space=pltpu.SMEM),
                  pl.BlockSpec((seq, qw), lambda b: (b, 0)),
                  pl.BlockSpec((seq, kw), lambda b: (b, kcol)),
                  pl.BlockSpec((seq, kw), lambda b: (b, kcol + 1))],
        out_specs=pl.BlockSpec((seq, qw), lambda b: (b, 0)),
        out_shape=jax.ShapeDtypeStruct((batch * seq, qw), BF16),
        compiler_params=_cp(("arbitrary",), 32),
        name="attn_a_ctx",
    )(sink, u_attn, u_attn, u_attn)


def _attn_a_lat_body(sink_ref, q_ref, k0_ref, k1_ref, k2_ref, v0_ref, v1_ref, v2_ref, kc_ref, vc_ref,
                     cq_ref, sq_ref, c0_ref, s0_ref, c2_ref, s2_ref, o_ref, *, group, kvh, seq):
    n = pl.program_id(1)
    blk = q_ref.shape[0]
    scale = HD ** -0.5
    cq, sq = cq_ref[...], sq_ref[...]
    qi = lax.broadcasted_iota(jnp.int32, (blk, 3 * blk), 0)
    kj = lax.broadcasted_iota(jnp.int32, (blk, 3 * blk), 1)
    kpos = (n - 1) * blk + kj
    qpos = n * blk + qi
    valid = (kpos >= 0) & (kpos < seq) & (jnp.abs(qpos - kpos) <= A_WINDOW)
    for kh in range(kvh):
        ks = slice(kh * HD, (kh + 1) * HD)
        kw = jnp.concatenate([_rope(k0_ref[:, ks], c0_ref[...], s0_ref[...]),
                              _rope(k1_ref[:, ks], cq, sq),
                              _rope(k2_ref[:, ks], c2_ref[...], s2_ref[...])], axis=0).astype(BF16)
        vw = jnp.concatenate([v0_ref[:, ks], v1_ref[:, ks], v2_ref[:, ks]], axis=0).astype(BF16)
        kc = kc_ref[:, ks].astype(BF16)
        vc = vc_ref[:, ks].astype(BF16)
        for g in range(group):
            hs = slice((kh * group + g) * HD, (kh * group + g + 1) * HD)
            q = (_rope(q_ref[:, hs], cq, sq) * (scale * LOG2E)).astype(BF16)
            s_loc = jnp.where(valid, _dot_nt(q, kw), NEG)
            s_ctx = _dot_nt(q, kc)
            sink = sink_ref[kh * group + g] * LOG2E
            m = jnp.maximum(jnp.maximum(jnp.max(s_loc, axis=-1, keepdims=True),
                                        jnp.max(s_ctx, axis=-1, keepdims=True)), sink)
            p_loc = jnp.exp2(s_loc - m)
            p_ctx = jnp.exp2(s_ctx - m)
            den = (jnp.sum(p_loc, axis=-1, keepdims=True) + jnp.sum(p_ctx, axis=-1, keepdims=True)
                   + jnp.exp2(sink - m))
            o = _dot(p_ctx.astype(BF16), vc) + _dot(p_loc.astype(BF16), vw)
            o_ref[:, hs] = (o / den).astype(BF16)


def attn_a_lat(u_attn, cache_k, cache_v, sink, rope_c, rope_s, layer, batch, seq, row0, kvh, group, blk=256):
    nb = seq // blk
    rb0 = row0 // blk
    qw = kvh * group * HD
    kw = kvh * HD
    kcol = qw // kw
    vcol = kcol + 1
    past = cache_k.shape[2]

    def rows(b, n):
        return rb0 + b * nb + n

    prev = lambda n: jnp.maximum(n - 1, 0)
    nxt = lambda n: jnp.minimum(n + 1, nb - 1)
    tab = lambda f: pl.BlockSpec((blk, HD), lambda b, n: (f(n), 0))
    kv = lambda f, col: pl.BlockSpec((blk, kw), lambda b, n: (rows(b, f(n)), col))
    same = lambda n: n
    return pl.pallas_call(
        functools.partial(_attn_a_lat_body, group=group, kvh=kvh, seq=seq),
        grid=(batch, nb),
        in_specs=[pl.BlockSpec(memory_space=pltpu.SMEM),
                  pl.BlockSpec((blk, qw), lambda b, n: (rows(b, n), 0)),
                  kv(prev, kcol), kv(same, kcol), kv(nxt, kcol),
                  kv(prev, vcol), kv(same, vcol), kv(nxt, vcol),
                  pl.BlockSpec((None, None, past, kw), lambda b, n: (b, layer, 0, 0)),
                  pl.BlockSpec((None, None, past, kw), lambda b, n: (b, layer, 0, 0)),
                  tab(same), tab(same), tab(prev), tab(prev), tab(nxt), tab(nxt)],
        out_specs=pl.BlockSpec((blk, qw), lambda b, n: (b * nb + n, 0)),
        out_shape=jax.ShapeDtypeStruct((batch * seq, qw), BF16),
        compiler_params=_cp(("arbitrary", "arbitrary"), 40),
        name="attn_a_lat",
    )(sink, u_attn, u_attn, u_attn, u_attn, u_attn, u_attn, u_attn, cache_k, cache_v,
      rope_c, rope_s, rope_c, rope_s, rope_c, rope_s)


def _lambda(lam_ref, lam_init):
    lw = lam_ref[...]
    return (jnp.exp(jnp.sum(lw[0:1] * lw[1:2], axis=-1, keepdims=True))
            - jnp.exp(jnp.sum(lw[2:3] * lw[3:4], axis=-1, keepdims=True)) + lam_init)


def _softmax_parts(parts):
    m = functools.reduce(jnp.maximum, [jnp.max(s, axis=-1, keepdims=True) for s in parts])
    ps = [jnp.exp(s - m) for s in parts]
    inv = 1.0 / functools.reduce(lambda a, b: a + b, [jnp.sum(p, axis=-1, keepdims=True) for p in ps])
    return [p * inv for p in ps]


def _attn_b_ctx_body(lam_ref, g_ref, q_ref, k_ref, v_ref, o_ref, *, lam_init):
    hd = HD // 2
    scale = hd ** -0.5
    lam = _lambda(lam_ref, lam_init)
    for h in range(q_ref.shape[1] // HD):
        hs = slice(h * HD, (h + 1) * HD)
        q = q_ref[:, hs].astype(BF16)
        k = k_ref[:, hs].astype(BF16)
        p0, = _softmax_parts([_dot_nt(q[:, :hd], k[:, :hd]) * scale])
        p1, = _softmax_parts([_dot_nt(q[:, hd:], k[:, hd:]) * scale])
        o = _dot((p0 - lam * p1).astype(BF16), v_ref[:, hs].astype(BF16))
        o_ref[:, hs] = (_rms_lanes(o, g_ref[...]) * (1.0 - lam_init)).astype(BF16)


def attn_b_ctx(u_attn, b_lambda, b_subln, lam_init, batch, seq, heads, col0):
    w = heads * HD
    qcol = col0 // w
    return pl.pallas_call(
        functools.partial(_attn_b_ctx_body, lam_init=lam_init),
        grid=(batch,),
        in_specs=[pl.BlockSpec(b_lambda.shape, lambda b: (0, 0)),
                  pl.BlockSpec((1, HD), lambda b: (0, 0)),
                  pl.BlockSpec((seq, w), lambda b: (b, qcol)),
                  pl.BlockSpec((seq, w), lambda b: (b, qcol + 1)),
                  pl.BlockSpec((seq, w), lambda b: (b, qcol + 2))],
        out_specs=pl.BlockSpec((seq, w), lambda b: (b, 0)),
        out_shape=jax.ShapeDtypeStruct((batch * seq, w), BF16),
        compiler_params=_cp(("arbitrary",), 32),
        name="attn_b_ctx",
    )(b_lambda, b_subln.reshape(1, HD), u_attn, u_attn, u_attn)


def _attn_b_lat_body(lam_ref, g_ref, q_ref, k_ref, v_ref, kc_ref, vc_ref, cq_ref, sq_ref, ck_ref, sk_ref,
                     o_ref, k_scr, kc_scr, v_scr, *, lam_init):
    hd = HD // 2
    scale = hd ** -0.5
    nq = pl.program_id(2)

    @pl.when(nq == 0)
    def _():
        k_scr[...] = _rope(k_ref[...], ck_ref[...], sk_ref[...]).T.astype(BF16)
        kc_scr[...] = kc_ref[...].T.astype(BF16)
        v_scr[...] = v_ref[...].astype(BF16)

    lam = _lambda(lam_ref, lam_init)
    k = k_scr[...]
    v = v_scr[...]
    kc = kc_scr[...]
    vc = vc_ref[...].astype(BF16)
    q = _rope(q_ref[...], cq_ref[...], sq_ref[...]) * (scale * LOG2E)
    lane = lax.broadcasted_iota(jnp.int32, q.shape, 1)
    outs = []
    for c in range(2):
        qc = jnp.where((lane >= hd) == (c == 1), q, 0.0).astype(BF16)
        s_ctx = _dot(qc, kc)
        s_lat = _dot(qc, k)
        m = jnp.maximum(jnp.max(s_ctx, axis=-1, keepdims=True), jnp.max(s_lat, axis=-1, keepdims=True))
        p_ctx = jnp.exp2(s_ctx - m)
        p_lat = jnp.exp2(s_lat - m)
        den = jnp.sum(p_ctx, axis=-1, keepdims=True) + jnp.sum(p_lat, axis=-1, keepdims=True)
        outs.append((_dot(p_ctx.astype(BF16), vc) + _dot(p_lat.astype(BF16), v)) / den)
    o = outs[0] - lam * outs[1]
    o_ref[...] = (_rms_lanes(o, g_ref[...]) * (1.0 - lam_init)).astype(BF16)


def attn_b_lat(u_attn, cache_k, cache_v, b_lambda, b_subln, rope_c, rope_s, lam_init, layer,
               batch, seq, row0, heads, col0, tq=256):
    qcol = col0 // HD
    kcol = qcol + heads
    vcol = kcol + heads
    nq = seq // tq
    past = cache_k.shape[2]
    rq0 = row0 // tq
    rs0 = row0 // seq
    return pl.pallas_call(
        functools.partial(_attn_b_lat_body, lam_init=lam_init),
        grid=(batch, heads, nq),
        in_specs=[pl.BlockSpec(b_lambda.shape, lambda b, h, n: (0, 0)),
                  pl.BlockSpec((1, HD), lambda b, h, n: (0, 0)),
                  pl.BlockSpec((tq, HD), lambda b, h, n: (rq0 + b * nq + n, qcol + h)),
                  pl.BlockSpec((seq, HD), lambda b, h, n: (rs0 + b, kcol + h)),
                  pl.BlockSpec((seq, HD), lambda b, h, n: (rs0 + b, vcol + h)),
                  pl.BlockSpec((None, None, past, HD), lambda b, h, n: (b, layer, 0, h)),
                  pl.BlockSpec((None, None, past, HD), lambda b, h, n: (b, layer, 0, h)),
                  pl.BlockSpec((tq, HD), lambda b, h, n: (n, 0)),
                  pl.BlockSpec((tq, HD), lambda b, h, n: (n, 0)),
                  pl.BlockSpec((seq, HD), lambda b, h, n: (0, 0)),
                  pl.BlockSpec((seq, HD), lambda b, h, n: (0, 0))],
        out_specs=pl.BlockSpec((tq, HD), lambda b, h, n: (b * nq + n, h)),
        out_shape=jax.ShapeDtypeStruct((batch * seq, heads * HD), BF16),
        scratch_shapes=[pltpu.VMEM((HD, seq), BF16), pltpu.VMEM((HD, past), BF16), pltpu.VMEM((seq, HD), BF16)],
        compiler_params=_cp(("arbitrary", "arbitrary", "arbitrary"), 40),
        name="attn_b_lat",
    )(b_lambda, b_subln.reshape(1, HD), u_attn, u_attn, u_attn, cache_k, cache_v,
      rope_c, rope_s, rope_c, rope_s)


def dft_half_tables(l):
    h = l // 2
    r = jnp.arange(l, dtype=jnp.int32)[:, None]
    f = r % h
    quarter = jnp.where(r >= h, l, 0)
    j = jnp.arange(h, dtype=jnp.int32)[None, :]

    def table(pos):
        m = ((2 * f + 1) * pos + quarter) % (4 * l)
        t = jnp.cos(m.astype(F32) * (math.pi / (2 * l)))
        hi = t.astype(BF16)
        return hi, (t - hi.astype(F32)).astype(BF16)

    (te, te_lo), (to, to_lo) = table(2 * j), table(2 * j + 1)
    return te, te_lo, to, to_lo, te.T, to.T


def parity_order(l):
    return jnp.concatenate([jnp.arange(0, l, 2), jnp.arange(1, l, 2)])


def hyena_feats(l):
    t = jnp.linspace(0.0, 1.0, l, dtype=F32)[:, None]
    bands = (HY_EMB - 1) // 2
    w = 2.0 * math.pi * jnp.arange(l, dtype=F32)[:, None] / l
    fr = jnp.linspace(1e-4, bands - 1, bands, dtype=F32)[None, :]
    feats = jnp.concatenate([t, jnp.cos(fr * w), -jnp.sin(fr * w)], axis=-1)
    return jnp.pad(feats, ((0, 0), (0, HD - HY_EMB)))


def hyena_decay(l, width):
    t = jnp.linspace(0.0, 1.0, l, dtype=F32)[:, None]
    deltas = jnp.abs(jnp.linspace(math.log(HY_TARGET) / HY_SLOW, math.log(HY_TARGET) / HY_FAST, width, dtype=F32))
    return jnp.exp(-t * deltas)


def _hy_time_body(feat_ref, dec_ref, w1_ref, b1_ref, w2_ref, b2_ref, w3_ref, fr_ref, kh_ref, kl_ref, nrm_scr,
                  *, width, l):
    ps = pl.program_id(2)
    c = pl.program_id(3)
    fr = fr_ref[...]
    h = jnp.sin(fr[0:1] * (_dot3(feat_ref[...], w1_ref[...]) + b1_ref[...]))
    h = jnp.sin(fr[1:2] * (_dot3(h, w2_ref[...]) + b2_ref[...]))
    dec = dec_ref[...]
    row = lax.broadcasted_iota(jnp.int32, dec.shape, 0)
    fwd = _dot3(h, w3_ref[:, :width]) * dec
    bwd = jnp.where((row == 0) & (c == 0), 0.0, _dot3(h, w3_ref[:, width:]) * dec)

    @pl.when((ps == 0) & (c == 0))
    def _():
        nrm_scr[...] = jnp.zeros_like(nrm_scr)

    @pl.when(ps == 0)
    def _():
        nrm_scr[...] += (jnp.sum(jnp.abs(fwd), axis=0, keepdims=True)
                         + jnp.sum(jnp.abs(bwd), axis=0, keepdims=True))

    @pl.when(ps == 1)
    def _():
        inv = 1.0 / ((nrm_scr[...] + EPS) * l)
        for part, val in ((0, fwd * inv), (1, bwd * inv)):
            hi, lo = _split(val)
            kh_ref[:, part * width:(part + 1) * width] = hi
            kl_ref[:, part * width:(part + 1) * width] = lo


def _hy_spec_body(ter_h, tei_h, tor_h, toi_h, ter_l, tei_l, tor_l, toi_l, keh_ref, kel_ref, koh_ref, kol_ref,
                   o_ref, *, width):
    keh, kel, koh, kol = keh_ref[...], kel_ref[...], koh_ref[...], kol_ref[...]

    def dot3(th, tl, kh, kl):
        t = th[...]
        return _dot(t, kh) + _dot(t, kl) + _dot(tl[...], kh)

    p_re, p_im = dot3(ter_h, ter_l, keh, kel), dot3(tei_h, tei_l, keh, kel)
    q_re, q_im = dot3(tor_h, tor_l, koh, kol), dot3(toi_h, toi_l, koh, kol)
    o_ref[0] = (p_re + q_re)[:, :width] + (p_re + q_re)[:, width:]
    o_ref[1] = (p_im + q_im)[:, :width] - (p_im + q_im)[:, width:]
    o_ref[2] = (p_re - q_re)[:, :width] + (p_re - q_re)[:, width:]
    o_ref[3] = (q_im - p_im)[:, :width] - (q_im - p_im)[:, width:]


def hyena_filters(l, hy_w1p, hy_b1, hy_w2, hy_b2, hy_w3, hy_freq, tabs, width, rc=512, tr=256):
    te, te_lo, to, to_lo = tabs[:4]
    depth = hy_w3.shape[0]
    ffn_w = hy_w2.shape[1]
    half = l // 2
    rc = min(rc, l)
    tr = min(tr, half)
    nr = half // tr
    order = parity_order(l)
    feats = hyena_feats(l)[order]
    dec = hyena_decay(l, width)[order]
    wspec = lambda shape: pl.BlockSpec((None,) + shape, lambda d, o, p, c: (d, 0, 0))
    kspec = pl.BlockSpec((None, rc, 2 * width), lambda d, o, p, c: (d, c * p, o))
    k_hi, k_lo = pl.pallas_call(
        functools.partial(_hy_time_body, width=width, l=l),
        grid=(depth, 2, 2, l // rc),
        in_specs=[pl.BlockSpec((rc, HD), lambda d, o, p, c: (c, 0)),
                  pl.BlockSpec((rc, width), lambda d, o, p, c: (c, 0)),
                  wspec((HD, ffn_w)), wspec((1, ffn_w)), wspec((ffn_w, ffn_w)), wspec((1, ffn_w)),
                  pl.BlockSpec((None, ffn_w, 2 * width), lambda d, o, p, c: (d, 0, o)),
                  wspec((2, ffn_w))],
        out_specs=[kspec, kspec],
        out_shape=[jax.ShapeDtypeStruct((depth, l, 4 * width), BF16)] * 2,
        scratch_shapes=[pltpu.VMEM((1, width), F32)],
        compiler_params=_cp(("arbitrary",) * 4, 32),
        name="hyena_time_filters",
    )(feats, dec, hy_w1p, hy_b1.reshape(depth, 1, ffn_w), hy_w2, hy_b2.reshape(depth, 1, ffn_w), hy_w3, hy_freq)
    re_spec = pl.BlockSpec((tr, half), lambda d, o, r: (r, 0))
    im_spec = pl.BlockSpec((tr, half), lambda d, o, r: (nr + r, 0))
    even = pl.BlockSpec((None, half, 2 * width), lambda d, o, r: (d, 0, o))
    odd = pl.BlockSpec((None, half, 2 * width), lambda d, o, r: (d, 1, o))
    return pl.pallas_call(
        functools.partial(_hy_spec_body, width=width),
        grid=(depth, 2, nr),
        in_specs=[re_spec, im_spec, re_spec, im_spec, re_spec, im_spec, re_spec, im_spec, even, even, odd, odd],
        out_specs=pl.BlockSpec((None, None, 4, tr, width), lambda d, o, r: (d, o, 0, r, 0)),
        out_shape=jax.ShapeDtypeStruct((depth, 2, 4, half, width), F32),
        compiler_params=_cp(("arbitrary",) * 3, 48),
        name="hyena_spectra",
    )(te, te, to, to, te_lo, te_lo, to_lo, to_lo, k_hi, k_lo, k_hi, k_lo)


def _short_conv(u, w, b):
    l = u.shape[0]
    row = lax.broadcasted_iota(jnp.int32, u.shape, 0)
    up = jnp.where(row == 0, 0.0, pltpu.roll(u, 1, 0))
    dn = jnp.where(row == l - 1, 0.0, pltpu.roll(u, l - 1, 0))
    return up * w[0:1] + u * w[1:2] + dn * w[2:3] + b


def _hyena_body(hv_ref, h1_ref, h2_ref, wv_ref, w1_ref, w2_ref, bv_ref, b1_ref, b2_ref, bias_ref,
                ter_ref, tei_ref, tor_ref, toi_ref, ger_ref, gei_ref, gor_ref, goi_ref, k_ref, o_ref,
                z_scr, ze_scr, zo_scr, acc_scr, *, nf):
    ph = pl.program_id(1)
    j = pl.program_id(2)
    half = ze_scr.shape[0]
    even = pl.ds(0, half, stride=2)
    odd = pl.ds(1, half, stride=2)
    tiles = [(t, slice(t * HD, (t + 1) * HD)) for t in range(z_scr.shape[0])]
    gather = lambda scr: jnp.concatenate([scr[t] for t, _ in tiles], axis=1)

    def set_input(z):
        for t, cols in tiles:
            z_scr[t] = z[:, cols]
            ze_scr[:, cols] = z_scr[t, even, :].astype(BF16)
            zo_scr[:, cols] = z_scr[t, odd, :].astype(BF16)

    @pl.when((ph == 0) & (j == 0))
    def _():
        set_input(_short_conv(hv_ref[...].astype(F32), wv_ref[...], bv_ref[...]))

    @pl.when(j == 0)
    def _():
        acc_scr[...] = jnp.zeros_like(acc_scr)

    ze, zo = ze_scr[...], zo_scr[...]
    p_re, p_im = _dot(ter_ref[...], ze), _dot(tei_ref[...], ze)
    q_re, q_im = _dot(tor_ref[...], zo), _dot(toi_ref[...], zo)
    z1_re, z1_im, z2_re, z2_im = p_re + q_re, p_im + q_im, p_re - q_re, q_im - p_im
    k1_re, k1_im, k2_re, k2_im = k_ref[0], k_ref[1], k_ref[2], k_ref[3]
    y1_re = z1_re * k1_re - z1_im * k1_im
    y1_im = z1_re * k1_im + z1_im * k1_re
    y2_re = z2_re * k2_re - z2_im * k2_im
    y2_im = z2_re * k2_im + z2_im * k2_re
    y_even = (_dot(ger_ref[...], (y1_re + y2_re).astype(BF16))
              + _dot(gei_ref[...], (y1_im - y2_im).astype(BF16)))
    y_odd = (_dot(gor_ref[...], (y1_re - y2_re).astype(BF16))
             + _dot(goi_ref[...], (y1_im + y2_im).astype(BF16)))
    for t, cols in tiles:
        acc_scr[t, even, :] += y_even[:, cols]
        acc_scr[t, odd, :] += y_odd[:, cols]

    @pl.when((j == nf - 1) & (ph == 0))
    def _():
        x1 = _short_conv(h1_ref[...].astype(F32), w1_ref[...], b1_ref[...])
        set_input(x1 * (gather(acc_scr) + gather(z_scr) * bias_ref[0:1]))

    @pl.when((j == nf - 1) & (ph == 1))
    def _():
        x2 = _short_conv(h2_ref[...].astype(F32), w2_ref[...], b2_ref[...])
        o_ref[...] = (x2 * (gather(acc_scr) + gather(z_scr) * bias_ref[1:2])).astype(BF16)


def hyena(u_hy, conv_w, conv_b, bias, kf, tabs, layer, batch, seq, row0, width, tfq=256):
    te, _, to, _, ge, go = tabs
    half = seq // 2
    tfq = min(tfq, half)
    nf = half // tfq
    rb = row0 // seq
    hspec = lambda part: pl.BlockSpec((seq, width), lambda b, p, j: (rb + b, part))
    wspec = lambda part: pl.BlockSpec((3, width), lambda b, p, j: (0, part))
    bspec = lambda part: pl.BlockSpec((1, width), lambda b, p, j: (0, part))
    t_re = pl.BlockSpec((tfq, half), lambda b, p, j: (j, 0))
    t_im = pl.BlockSpec((tfq, half), lambda b, p, j: (nf + j, 0))
    g_re = pl.BlockSpec((half, tfq), lambda b, p, j: (0, j))
    g_im = pl.BlockSpec((half, tfq), lambda b, p, j: (0, nf + j))
    return pl.pallas_call(
        functools.partial(_hyena_body, nf=nf),
        grid=(batch, 2, nf),
        in_specs=[hspec(0), hspec(1), hspec(2), wspec(0), wspec(1), wspec(2), bspec(0), bspec(1), bspec(2),
                  pl.BlockSpec((2, width), lambda b, p, j: (0, 0)),
                  t_re, t_im, t_re, t_im, g_re, g_im, g_re, g_im,
                  pl.BlockSpec((None, None, 4, tfq, width), lambda b, p, j: (layer, p, 0, j, 0))],
        out_specs=pl.BlockSpec((seq, width), lambda b, p, j: (b, 0)),
        out_shape=jax.ShapeDtypeStruct((batch * seq, width), BF16),
        scratch_shapes=[pltpu.VMEM((width // HD, seq, HD), F32), pltpu.VMEM((half, width), BF16),
                        pltpu.VMEM((half, width), BF16), pltpu.VMEM((width // HD, seq, HD), F32)],
        compiler_params=_cp(("arbitrary",) * 3, 56),
        name="hyena",
    )(u_hy, u_hy, u_hy, conv_w, conv_w, conv_w, conv_b, conv_b, conv_b, bias,
      te, te, to, to, ge, ge, go, go, kf)


def _hgrn_pair_level(t, s, fwd):
    x = t ^ s
    lvl = jnp.where(x == 0, 0, 32 - lax.clz(x))
    used = (s <= t) if fwd else (s >= t)
    return jnp.where(used, lvl, -1)


def _hgrn_midpoint(b3, blk, fwd):
    rows = b3.shape[1]
    r0 = blk // 2 - 1 if fwd else blk // 2
    ref = b3[:, r0:r0 + 1, :]
    if rows > blk:
        pos = lax.broadcasted_iota(jnp.int32, b3.shape, 1)
        for j in range(1, rows // blk):
            ref = jnp.where(pos >= j * blk, b3[:, j * blk + r0:j * blk + r0 + 1, :], ref)
    return ref


def _hgrn_chunk(q, k, g, v, st, tri, lvl, fwd):
    c = CHUNK
    gh, gl = _split(g)
    b = _dot(tri, gh) + _dot(tri, gl)
    tot = b[c - 1:c] if fwd else b[0:1]
    vb = v.astype(BF16)
    o = _dot_nt((q * jnp.exp2(b)).astype(BF16), st.astype(BF16))
    kd = (k * jnp.exp2(tot - b)).astype(BF16)
    st_new = st * jnp.exp2(tot) + _dot_tn(vb, kd)

    a = jnp.where(lvl == 0, jnp.sum(q * k, axis=-1, keepdims=True), 0.0)
    a = jnp.where(lvl == 1, _dot_nt((q * jnp.exp2(g)).astype(BF16), k.astype(BF16)), a)
    blk, m = 4, 2
    while blk <= c:
        rows = max(blk, SUB)
        b3, q3, k3 = (x.reshape(c // rows, rows, HD) for x in (b, q, k))
        e = jnp.exp2(-jnp.abs(b3 - _hgrn_midpoint(b3, blk, fwd)))
        qt = (q3 * e).reshape(c, HD).astype(BF16)
        kt = (k3 * e).reshape(c, HD).astype(BF16)
        a = jnp.where(lvl == m, _dot_nt(qt, kt), a)
        blk, m = blk * 2, m + 1
    return o + _dot(a.astype(BF16), vb), st_new


def _hgrn_gate(z, lb):
    e = jnp.exp(-jnp.abs(z))
    r = 1.0 / (1.0 + e)
    pos = z >= 0
    sig_pos = jnp.where(pos, r, e * r)
    sig_neg = jnp.where(pos, e * r, r)
    f = sig_pos + jnp.maximum(lb, LB_FLOOR) * sig_neg
    return jnp.log(f) * LOG2E, (1.0 - lb) * sig_neg


def _hgrn_lower_bound(lb_ref, layer, depth, direction):
    x = lb_ref[...]
    rows = [x[2 * i + direction:2 * i + direction + 1] for i in range(depth)]
    m = functools.reduce(jnp.maximum, rows)
    es = [jnp.exp(r - m) for r in rows]
    tot = functools.reduce(lambda a, b: a + b, es)
    cum = jnp.zeros_like(tot)
    for i in range(1, layer + 1):
        cum = cum + es[i]
    return jnp.maximum(cum / tot, 0.0)


def _hgrn_body(*refs, layer, depth, has_state, emit_state):
    it = iter(refs)
    lb_ref, ng_ref, q_ref, ff_ref, fb_ref, i_ref, g_ref = (next(it) for _ in range(7))
    s0_ref = next(it) if has_state else None
    o_ref = next(it)
    so_ref = next(it) if emit_state else None
    of_scr, ob_scr, st_scr, tri_scr, lvl_scr = (next(it) for _ in range(5))
    nc = q_ref.shape[0] // CHUNK
    hp = q_ref.shape[1] // HD
    ti = lax.broadcasted_iota(jnp.int32, (CHUNK, CHUNK), 0)
    si = lax.broadcasted_iota(jnp.int32, (CHUNK, CHUNK), 1)
    chains = [(h, d) for h in range(hp) for d in range(2)]
    lbs = {}
    for d in range(2):
        tri_scr[d] = jnp.where((si <= ti) if d == 0 else (si >= ti), 1.0, 0.0).astype(BF16)
        lvl_scr[d] = _hgrn_pair_level(ti, si, d == 0)
        lb_all = _hgrn_lower_bound(lb_ref, layer, depth, d)
        for h in range(hp):
            lbs[h, d] = lb_all[:, h * HD:(h + 1) * HD]
            st_scr[2 * h + d] = s0_ref[d, h].T if has_state else jnp.zeros((HD, HD), F32)

    def step(i, carry):
        for h, d in chains:
            fwd = d == 0
            ci = i if fwd else nc - 1 - i
            sl = pl.ds(pl.multiple_of(ci * CHUNK, CHUNK), CHUNK)
            cs = slice(h * HD, (h + 1) * HD)
            xq = q_ref[sl, cs].astype(F32)
            log_f, kk = _hgrn_gate((ff_ref if fwd else fb_ref)[sl, cs].astype(F32), lbs[h, d])
            o, st = _hgrn_chunk(xq * _sigmoid(xq), kk, log_f, i_ref[sl, cs].astype(F32), st_scr[2 * h + d],
                                tri_scr[d], lvl_scr[d], fwd)
            st_scr[2 * h + d] = st
            (of_scr if fwd else ob_scr)[sl, cs] = o
        return carry

    lax.fori_loop(0, nc, step, 0, unroll=4)
    for h, d in chains:
        if emit_state:
            so_ref[d, h] = st_scr[2 * h + d].T
    for h in range(hp):
        cs = slice(h * HD, (h + 1) * HD)
        xg = g_ref[:, cs].astype(F32)
        o_ref[:, cs] = (_rms_lanes(of_scr[:, cs] + ob_scr[:, cs], ng_ref[...]) * (xg * _sigmoid(xg))).astype(BF16)


def hgrn(u_hg, hg_lb2, hg_norm, state, layer, depth, batch, seq, row0, heads, emit_state, col0=0, hp=4):
    rb = row0 // seq
    ng = heads // hp
    w = hp * HD
    cb = col0 // w
    part = lambda p: pl.BlockSpec((seq, w), lambda b, h: (rb + b, cb + p * ng + h))
    in_specs = [pl.BlockSpec((2 * depth, w), lambda b, h: (0, h)),
                pl.BlockSpec((1, HD), lambda b, h: (0, 0)),
                part(0), part(1), part(2), part(3), part(4)]
    args = [hg_lb2, hg_norm.reshape(1, HD), u_hg, u_hg, u_hg, u_hg, u_hg]
    if state is not None:
        in_specs.append(pl.BlockSpec((None, None, 2, hp, HD, HD), lambda b, h: (b, layer, 0, h, 0, 0)))
        args.append(state)
    out_specs = [pl.BlockSpec((seq, w), lambda b, h: (b, h))]
    out_shape = [jax.ShapeDtypeStruct((batch * seq, heads * HD), BF16)]
    if emit_state:
        out_specs.append(pl.BlockSpec((None, 2, hp, HD, HD), lambda b, h: (b, 0, h, 0, 0)))
        out_shape.append(jax.ShapeDtypeStruct((batch, 2, heads, HD, HD), F32))
    return pl.pallas_call(
        functools.partial(_hgrn_body, layer=layer, depth=depth, has_state=state is not None,
                          emit_state=emit_state),
        grid=(batch, ng),
        in_specs=in_specs,
        out_specs=out_specs,
        out_shape=out_shape,
        scratch_shapes=[pltpu.VMEM((seq, w), F32), pltpu.VMEM((seq, w), F32),
                        pltpu.VMEM((2 * hp, HD, HD), F32),
                        pltpu.VMEM((2, CHUNK, CHUNK), BF16), pltpu.VMEM((2, CHUNK, CHUNK), jnp.int32)],
        compiler_params=_cp(("arbitrary", "arbitrary"), 48),
        name="hgrn",
    )(*args)


def _merge_body(*refs, n_ctx_tiles):
    o_refs, (gt_ref, x_ref, mod_ref, wb_ref, wo_ref, out_ref) = refs[:2 * N_BRANCH], refs[2 * N_BRANCH:]
    d = x_ref.shape[1]
    is_ctx = pl.program_id(0) < n_ctx_tiles
    acc = None
    for n in range(N_BRANCH):
        o = jnp.where(is_ctx, o_refs[2 * n][...], o_refs[2 * n + 1][...])
        y = gt_ref[:, n * d:(n + 1) * d].astype(F32) * _dot(o, wb_ref[n])
        acc = y if acc is None else acc + y
    out = _dot(acc.astype(BF16), wo_ref[...])
    out_ref[...] = x_ref[...] + mod_ref[...][5:6] * out


def merge(branch_outs, gates, x, mod_l, w_branch, w_out, t_ctx, l_lat, tm=256):
    t, d = x.shape
    mw = w_branch.shape[1]
    nct = t_ctx // tm
    row = functools.partial(_mod_row, tm=tm, t_ctx=t_ctx, l_lat=l_lat)
    cspec = pl.BlockSpec((tm, mw), lambda i: (jnp.minimum(i, nct - 1), 0))
    lspec = pl.BlockSpec((tm, mw), lambda i: (jnp.maximum(i - nct, 0), 0))
    return pl.pallas_call(
        functools.partial(_merge_body, n_ctx_tiles=nct),
        grid=(t // tm,),
        in_specs=[cspec, lspec] * N_BRANCH + [
                  pl.BlockSpec((tm, N_BRANCH * d), lambda i: (i, 0)),
                  pl.BlockSpec((tm, d), lambda i: (i, 0)),
                  pl.BlockSpec((None, N_MOD, d), lambda i: (row(i), 0, 0)),
                  pl.BlockSpec((N_BRANCH, mw, d), lambda i: (0, 0, 0), pipeline_mode=pl.Buffered(1)),
                  pl.BlockSpec((d, d), lambda i: (0, 0), pipeline_mode=pl.Buffered(1))],
        out_specs=pl.BlockSpec((tm, d), lambda i: (i, 0)),
        out_shape=jax.ShapeDtypeStruct((t, d), F32),
        compiler_params=_cp(("arbitrary",), 56),
        name="merge",
    )(*[o for pair in branch_outs for o in pair], gates, x, mod_l, w_branch, w_out)


def _final_norm_body(x_ref, g_ref, oc_ref, ol_ref, *, n_ctx_tiles):
    i = pl.program_id(0)

    @pl.when(i < n_ctx_tiles)
    def _():
        oc_ref[...] = _rms_lanes(x_ref[...], g_ref[...])

    @pl.when(i >= n_ctx_tiles)
    def _():
        ol_ref[...] = _rms_lanes(x_ref[...], g_ref[...])


def final_norm(x, g, t_ctx, tm=512):
    t, d = x.shape
    nct = t_ctx // tm
    return pl.pallas_call(
        functools.partial(_final_norm_body, n_ctx_tiles=nct),
        grid=(t // tm,),
        in_specs=[pl.BlockSpec((tm, d), lambda i: (i, 0)), pl.BlockSpec((1, d), lambda i: (0, 0))],
        out_specs=[pl.BlockSpec((tm, d), lambda i: (jnp.minimum(i, nct - 1), 0)),
                   pl.BlockSpec((tm, d), lambda i: (jnp.maximum(i - nct, 0), 0))],
        out_shape=[jax.ShapeDtypeStruct((t_ctx, d), F32), jax.ShapeDtypeStruct((t - t_ctx, d), F32)],
        compiler_params=_cp(("arbitrary",), 32),
        name="final_norm",
    )(x, g.reshape(1, d))


def kernel(x_prompt, x_sample, c, cache_a_k, cache_a_v, cache_b_k, cache_b_v, state_hgrn, c_ctx, w_mod, b_mod, norm_g, w_ffn1_gu, w_ffn1_down, w_ffn2_gu, w_ffn2_down, w_in, w_branch, w_out, a_sink, b_lambda, b_subln, hy_conv_w, hy_conv_b, hy_w1, hy_b1, hy_w2, hy_b2, hy_w3, hy_freq, hy_bias, hg_lb, hg_norm, final_g):
    batch, seq, d = x_prompt.shape
    dec_batch, dec_seq, _ = x_sample.shape
    depth = w_mod.shape[0]
    mix_w = w_branch.shape[2]
    a_heads = a_sink.shape[1]
    a_kvh = cache_a_k.shape[3]
    a_group = a_heads // a_kvh
    b_heads = cache_b_k.shape[3]
    hg_heads = state_hgrn.shape[3]
    t_ctx, t_lat = batch * seq, dec_batch * dec_seq

    n_attn = (a_heads + 2 * a_kvh + 3 * b_heads) * HD
    b_col0 = (a_heads + 2 * a_kvh) * HD

    x = jnp.concatenate([x_prompt.reshape(t_ctx, d), x_sample.reshape(t_lat, d)], axis=0)
    cond = jnp.concatenate([c_ctx[None, :], c], axis=0)
    cond = jnp.pad(cond, ((0, MOD_ROWS - cond.shape[0]), (0, 0)))
    mod = modulation(cond, w_mod, b_mod).reshape(depth, MOD_ROWS, N_MOD, d)

    rope_a = rope_tables(dec_seq, HD, 1)
    rope_b = rope_tables(dec_seq, HD // 2, 2)
    tabs_ctx = dft_half_tables(seq)
    tabs_lat = dft_half_tables(dec_seq)
    w1p = jnp.pad(hy_w1, ((0, 0), (0, HD - hy_w1.shape[1]), (0, 0)))
    kf_ctx = hyena_filters(seq, w1p, hy_b1, hy_w2, hy_b2, hy_w3, hy_freq, tabs_ctx, mix_w)
    kf_lat = hyena_filters(dec_seq, w1p, hy_b1, hy_w2, hy_b2, hy_w3, hy_freq, tabs_lat, mix_w)

    cak = cache_a_k.reshape(dec_batch, depth, -1, a_kvh * HD)
    cav = cache_a_v.reshape(dec_batch, depth, -1, a_kvh * HD)
    cbk = cache_b_k.reshape(dec_batch, depth, -1, b_heads * HD)
    cbv = cache_b_v.reshape(dec_batch, depth, -1, b_heads * HD)
    hg_lb2 = hg_lb.reshape(depth * 2, mix_w)

    ak_l, av_l, bk_l, bv_l, st_l = [], [], [], [], []
    for l in range(depth):
        lam_init = 0.8 - 0.6 * math.exp(-0.3 * l)
        mod_l = mod[l]
        x, h_mix = ffn(x, mod_l, norm_g[l, 0:1], w_ffn1_gu[l].astype(BF16), w_ffn1_down[l].astype(BF16), 0,
                       t_ctx, dec_seq, next_g_row=norm_g[l, 1:2])
        n_rec = n_attn + 8 * mix_w
        u_attn = proj(h_mix, w_in[l, :, :n_attn].astype(BF16), F32, False)
        u_rec = proj(h_mix, w_in[l, :, n_attn:n_rec].astype(BF16), BF16, False)
        gates = proj(h_mix, w_in[l, :, n_rec:].astype(BF16), BF16, True)

        uc = u_attn[:t_ctx]
        k0 = a_heads * HD
        ak_l.append(uc[:, k0:k0 + a_kvh * HD].reshape(batch, seq, a_kvh, HD))
        av_l.append(uc[:, k0 + a_kvh * HD:b_col0].reshape(batch, seq, a_kvh, HD))
        k1 = b_col0 + b_heads * HD
        bk_l.append(uc[:, k1:k1 + b_heads * HD].reshape(batch, seq, b_heads, 2, HD // 2))
        bv_l.append(uc[:, k1 + b_heads * HD:].reshape(batch, seq, b_heads, HD))

        oa_c = attn_a_ctx(u_attn, a_sink[l], batch, seq, a_kvh, a_group)
        oa_s = attn_a_lat(u_attn, cak, cav, a_sink[l], rope_a[0], rope_a[1], l, dec_batch, dec_seq, t_ctx,
                          a_kvh, a_group)
        ob_c = attn_b_ctx(u_attn, b_lambda[l], b_subln[l], lam_init, batch, seq, b_heads, b_col0)
        ob_s = attn_b_lat(u_attn, cbk, cbv, b_lambda[l], b_subln[l], rope_b[0], rope_b[1], lam_init, l,
                          dec_batch, dec_seq, t_ctx, b_heads, b_col0)
        cb = hy_conv_b[l].reshape(1, -1)
        oc_c = hyena(u_rec, hy_conv_w[l], cb, hy_bias[l], kf_ctx, tabs_ctx, l, batch, seq, 0, mix_w)
        oc_s = hyena(u_rec, hy_conv_w[l], cb, hy_bias[l], kf_lat, tabs_lat, l, dec_batch, dec_seq, t_ctx, mix_w)
        od_c, st = hgrn(u_rec, hg_lb2, hg_norm[l], None, l, depth, batch, seq, 0, hg_heads, True, col0=3 * mix_w)
        od_s, = hgrn(u_rec, hg_lb2, hg_norm[l], state_hgrn, l, depth, dec_batch, dec_seq, t_ctx, hg_heads, False,
                     col0=3 * mix_w)
        st_l.append(st)

        x = merge(((oa_c, oa_s), (ob_c, ob_s), (oc_c, oc_s), (od_c, od_s)), gates, x, mod_l,
                  w_branch[l].astype(BF16), w_out[l].astype(BF16), t_ctx, dec_seq)
        x = ffn(x, mod_l, norm_g[l, 2:3], w_ffn2_gu[l].astype(BF16), w_ffn2_down[l].astype(BF16), 2,
                t_ctx, dec_seq)

    y_ctx, y_lat = final_norm(x, final_g, t_ctx)
    return (y_ctx.reshape(batch, seq, d), y_lat.reshape(dec_batch, dec_seq, d),
            jnp.stack(ak_l, axis=1), jnp.stack(av_l, axis=1), jnp.stack(bk_l, axis=1), jnp.stack(bv_l, axis=1),
            jnp.stack(st_l, axis=1))
```

```python
import functools
import math

import jax
import jax.numpy as jnp
from jax import lax
from jax.experimental import pallas as pl
from jax.experimental.pallas import tpu as pltpu

F32 = jnp.float32
BF16 = jnp.bfloat16

EPS = 1e-6
NEG = -1e30
LB_FLOOR = 1e-30
ROPE_BASE = 10000.0
GRID_W = 64
N_MOD = 9
N_BRANCH = 4
HD = 128
A_WINDOW = 128
HY_EMB = 33
HY_TARGET, HY_FAST, HY_SLOW = 1e-2, 0.3, 1.5
CHUNK = 128
SUB = 8
LOG2E = 1.4426950408889634
MOD_ROWS = 16
FFN_TILE = 512
PROJ_TILE = 512
VMEM_MB = 2 ** 20


def _cp(sem, vmem_mb):
    return pltpu.CompilerParams(dimension_semantics=sem, vmem_limit_bytes=vmem_mb * VMEM_MB)


def _dot(a, b):
    return jnp.dot(a, b, preferred_element_type=F32)


def _dot_nt(a, b):
    return lax.dot_general(a, b, (((1,), (1,)), ((), ())), preferred_element_type=F32)


def _dot_tn(a, b):
    return lax.dot_general(a, b, (((0,), (0,)), ((), ())), preferred_element_type=F32)


def _split(a):
    hi = a.astype(BF16)
    lo = (a - hi.astype(F32)).astype(BF16)
    return hi, lo


def _dot3(a, b):
    ah, al = _split(a)
    bh, bl = _split(b)
    return _dot(ah, bh) + _dot(ah, bl) + _dot(al, bh)


def _sigmoid(x):
    return 0.5 * jnp.tanh(0.5 * x) + 0.5


def _norm_mod(x, g, sc, sh):
    y = x * lax.rsqrt(jnp.mean(x * x, axis=-1, keepdims=True) + EPS)
    return (y * g) * (1.0 + sc) + sh


def _norm_mod_cols(load, keep, reload, store, shape, g, sc, sh):
    n, d = shape
    tiles = [slice(t, t + HD) for t in range(0, d, HD)]
    ssq = jnp.zeros((n, HD), F32)
    for cols in tiles:
        x = load(cols)
        if keep is not None:
            keep(cols, x)
        ssq = ssq + x * x
    r = lax.rsqrt(jnp.sum(ssq, axis=-1, keepdims=True) * (1.0 / d) + EPS)
    gs = g * (1.0 + sc)
    for cols in tiles:
        store(cols, (reload(cols) * r) * gs[:, cols] + sh[:, cols])


def _rms_lanes(x, g):
    return x * lax.rsqrt(jnp.mean(x * x, axis=-1, keepdims=True) + EPS) * g


def _mod_body(c_ref, w_ref, b_ref, o_ref):
    c = c_ref[...]
    a = (c * _sigmoid(c)).astype(BF16)
    o_ref[...] = _dot(a, w_ref[...].astype(BF16)) + b_ref[...]


def modulation(cond, w_mod, b_mod, tn=1024):
    depth, d, n = w_mod.shape
    return pl.pallas_call(
        _mod_body,
        grid=(depth, n // tn),
        in_specs=[pl.BlockSpec((MOD_ROWS, d), lambda l, j: (0, 0)),
                  pl.BlockSpec((None, d, tn), lambda l, j: (l, 0, j)),
                  pl.BlockSpec((None, 1, tn), lambda l, j: (l, 0, j))],
        out_specs=pl.BlockSpec((None, MOD_ROWS, tn), lambda l, j: (l, 0, j)),
        out_shape=jax.ShapeDtypeStruct((depth, MOD_ROWS, n), F32),
        compiler_params=_cp(("arbitrary", "arbitrary"), 40),
        name="modulation",
    )(cond, w_mod, b_mod.reshape(depth, 1, n))


def _mod_row(i, tm, t_ctx, l_lat):
    start = i * tm
    return jnp.where(start < t_ctx, 0, 1 + (start - t_ctx) // l_lat)


def _ffn_body(*refs, sub, nj, next_sub):
    if next_sub is None:
        x_ref, mod_ref, g_ref, wg_ref, wu_ref, wd_ref, o_ref, h_scr = refs
    else:
        x_ref, mod_ref, g_ref, gn_ref, wg_ref, wu_ref, wd_ref, o_ref, hn_ref, h_scr = refs
    j = pl.program_id(1)

    @pl.when(j == 0)
    def _():
        m = mod_ref[...]
        load = lambda cols: x_ref[:, cols]

        def store(cols, h):
            h_scr[:, cols] = h.astype(BF16)

        _norm_mod_cols(load, None, load, store, x_ref.shape, g_ref[...],
                       m[3 * sub + 1:3 * sub + 2], m[3 * sub:3 * sub + 1])
        o_ref[...] = jnp.zeros_like(o_ref)

    h = h_scr[...]
    tf = wg_ref.shape[1]
    hc = tf // 2
    halves = [slice(c0, c0 + hc) for c0 in range(0, tf, hc)]
    ab = [(_dot(h, wg_ref[:, cs]), _dot(h, wu_ref[:, cs])) for cs in halves]
    acts = [(a * _sigmoid(a) * b).astype(BF16) for a, b in ab]
    o_ref[...] += functools.reduce(lambda p, q: p + q, [_dot(act, wd_ref[cs, :]) for act, cs in zip(acts, halves)])

    @pl.when(j == nj - 1)
    def _():
        m = mod_ref[...]
        ga = 0.5 * m[3 * sub + 2:3 * sub + 3]
        if next_sub is None:
            o_ref[...] = x_ref[...] + ga * o_ref[...]
        else:
            def keep(cols, y):
                o_ref[:, cols] = y

            def store(cols, h):
                hn_ref[:, cols] = h.astype(BF16)

            _norm_mod_cols(lambda cols: x_ref[:, cols] + ga[:, cols] * o_ref[:, cols], keep,
                           lambda cols: o_ref[:, cols], store, x_ref.shape, gn_ref[...],
                           m[3 * next_sub + 1:3 * next_sub + 2], m[3 * next_sub:3 * next_sub + 1])


def ffn(x, mod_l, g_row, w_gu, w_dn, sub, t_ctx, l_lat, next_g_row=None, tm=512, tf=FFN_TILE):
    t, d = x.shape
    dff = w_dn.shape[0]
    nj = dff // tf
    emit = next_g_row is not None
    row = functools.partial(_mod_row, tm=tm, t_ctx=t_ctx, l_lat=l_lat)
    gspec = pl.BlockSpec((1, d), lambda i, j: (0, 0))
    xspec = pl.BlockSpec((tm, d), lambda i, j: (i, 0))
    outs = pl.pallas_call(
        functools.partial(_ffn_body, sub=sub, nj=nj, next_sub=sub + 1 if emit else None),
        grid=(t // tm, nj),
        in_specs=[xspec,
                  pl.BlockSpec((None, N_MOD, d), lambda i, j: (row(i), 0, 0)),
                  gspec] + ([gspec] if emit else []) + [
                  pl.BlockSpec((d, tf), lambda i, j: (0, j)),
                  pl.BlockSpec((d, tf), lambda i, j: (0, j + nj)),
                  pl.BlockSpec((tf, d), lambda i, j: (j, 0))],
        out_specs=[xspec] + ([xspec] if emit else []),
        out_shape=[jax.ShapeDtypeStruct((t, d), F32)] + ([jax.ShapeDtypeStruct((t, d), BF16)] if emit else []),
        scratch_shapes=[pltpu.VMEM((tm, d), BF16)],
        compiler_params=_cp(("arbitrary", "arbitrary"), 56),
        name="ffn",
    )(x, mod_l, g_row, *([next_g_row] if emit else []), w_gu, w_gu, w_dn)
    return outs if emit else outs[0]


def _proj_body(h_ref, w_ref, o_ref, *, gate):
    r = _dot(h_ref[...], w_ref[...])
    o_ref[...] = _sigmoid(r.astype(o_ref.dtype)) if gate else r.astype(o_ref.dtype)


def _proj_kv_body(h_ref, w_ref, o_ref, kv_ref, *, kv_tiles, n_ctx_tiles):
    i, j = pl.program_id(0), pl.program_id(1)
    r = _dot(h_ref[...], w_ref[...])
    o_ref[...] = r
    is_kv = functools.reduce(lambda a, b: a | b, [j == c for c in kv_tiles])

    @pl.when((i < n_ctx_tiles) & is_kv)
    def _():
        kv_ref[...] = r


def proj_with_context_kv(h, w, kv_tiles, t_ctx, tm=2048, tn=PROJ_TILE):
    t, d = h.shape
    n = w.shape[1]
    while t_ctx % tm:
        tm //= 2
    nct = t_ctx // tm
    last = len(kv_tiles) - 1

    def kv_col(j):
        return functools.reduce(lambda a, b: a + b, [(j >= c).astype(jnp.int32) for c in kv_tiles[1:]], 0)

    return pl.pallas_call(
        functools.partial(_proj_kv_body, kv_tiles=kv_tiles, n_ctx_tiles=nct),
        grid=(t // tm, n // tn),
        in_specs=[pl.BlockSpec((tm, d), lambda i, j: (i, 0)),
                  pl.BlockSpec((d, tn), lambda i, j: (0, j))],
        out_specs=[pl.BlockSpec((tm, tn), lambda i, j: (i, j)),
                   pl.BlockSpec((tm, tn), lambda i, j: (jnp.minimum(i, nct - 1),
                                                        jnp.where(i < nct, kv_col(j), last)))],
        out_shape=[jax.ShapeDtypeStruct((t, n), F32),
                   jax.ShapeDtypeStruct((t_ctx, len(kv_tiles) * tn), F32)],
        compiler_params=_cp(("arbitrary", "arbitrary"), 56),
        name="proj_kv",
    )(h, w)


def proj(h, w, out_dtype, gate, tm=2048, tn=PROJ_TILE):
    t, d = h.shape
    n = w.shape[1]
    while t % tm:
        tm //= 2
    return pl.pallas_call(
        functools.partial(_proj_body, gate=gate),
        grid=(t // tm, n // tn),
        in_specs=[pl.BlockSpec((tm, d), lambda i, j: (i, 0)),
                  pl.BlockSpec((d, tn), lambda i, j: (0, j))],
        out_specs=pl.BlockSpec((tm, tn), lambda i, j: (i, j)),
        out_shape=jax.ShapeDtypeStruct((t, n), out_dtype),
        compiler_params=_cp(("arbitrary", "arbitrary"), 56),
        name="proj",
    )(h, w)


def rope_tables(l, rot_dim, reps):
    n_rows = l // GRID_W
    rows = jnp.broadcast_to(jnp.arange(n_rows, dtype=F32)[:, None], (n_rows, GRID_W)).reshape(-1)
    cols = jnp.broadcast_to(jnp.arange(GRID_W, dtype=F32)[None, :], (n_rows, GRID_W)).reshape(-1)
    axis_dim = rot_dim // 2
    inv = ROPE_BASE ** (-jnp.arange(0, axis_dim, 2, dtype=F32) / axis_dim)
    ang = jnp.concatenate([rows[:, None] * inv, cols[:, None] * inv], axis=-1)
    cos = jnp.repeat(jnp.cos(ang), 2, axis=-1)
    sin = jnp.repeat(jnp.sin(ang), 2, axis=-1)
    sign = jnp.tile(jnp.array([-1.0, 1.0], F32), rot_dim // 2)
    return jnp.tile(cos, (1, reps)), jnp.tile(sin * sign, (1, reps))


def _rope(x, c, s):
    lane = lax.broadcasted_iota(jnp.int32, x.shape, 1)
    nxt = pltpu.roll(x, x.shape[1] - 1, 1)
    prv = pltpu.roll(x, 1, 1)
    return x * c + jnp.where((lane & 1) == 0, nxt, prv) * s


def _attn_a_ctx_body(sink_ref, q_ref, k_ref, v_ref, o_ref, *, group, kvh):
    scale = HD ** -0.5
    for kh in range(kvh):
        k = k_ref[:, kh * HD:(kh + 1) * HD].astype(BF16)
        v = v_ref[:, kh * HD:(kh + 1) * HD].astype(BF16)
        for g in range(group):
            hs = slice((kh * group + g) * HD, (kh * group + g + 1) * HD)
            q = q_ref[:, hs].astype(BF16)
            s = _dot_nt(q, k) * scale
            sink = sink_ref[kh * group + g]
            m = jnp.maximum(jnp.max(s, axis=-1, keepdims=True), sink)
            p = jnp.exp(s - m)
            den = jnp.sum(p, axis=-1, keepdims=True) + jnp.exp(sink - m)
            o_ref[:, hs] = (_dot(p.astype(BF16), v) / den).astype(BF16)


def attn_a_ctx(u_attn, sink, batch, seq, kvh, group):
    qw = kvh * group * HD
    kw = kvh * HD
    kcol = qw // kw
    return pl.pallas_call(
        functools.partial(_attn_a_ctx_body, group=group, kvh=kvh),
        grid=(batch,),
        in_specs=[pl.BlockSpec(memory_space=pltpu.SMEM),
                  pl.BlockSpec((seq, qw), lambda b: (b, 0)),
                  pl.BlockSpec((seq, kw), lambda b: (b, kcol)),
                  pl.BlockSpec((seq, kw), lambda b: (b, kcol + 1))],
        out_specs=pl.BlockSpec((seq, qw), lambda b: (b, 0)),
        out_shape=jax.ShapeDtypeStruct((batch * seq, qw), BF16),
        compiler_params=_cp(("arbitrary",), 32),
        name="attn_a_ctx",
    )(sink, u_attn, u_attn, u_attn)


def _attn_a_lat_body(sink_ref, q_ref, k0_ref, k1_ref, k2_ref, v0_ref, v1_ref, v2_ref, kc_ref, vc_ref,
                     cq_ref, sq_ref, c0_ref, s0_ref, c2_ref, s2_ref, o_ref, *, group, kvh, seq):
    n = pl.program_id(1)
    blk = q_ref.shape[0]
    scale = HD ** -0.5
    cq, sq = cq_ref[...], sq_ref[...]
    qi = lax.broadcasted_iota(jnp.int32, (blk, 3 * blk), 0)
    kj = lax.broadcasted_iota(jnp.int32, (blk, 3 * blk), 1)
    kpos = (n - 1) * blk + kj
    qpos = n * blk + qi
    valid = (kpos >= 0) & (kpos < seq) & (jnp.abs(qpos - kpos) <= A_WINDOW)
    for kh in range(kvh):
        ks = slice(kh * HD, (kh + 1) * HD)
        kw = jnp.concatenate([_rope(k0_ref[:, ks], c0_ref[...], s0_ref[...]),
                              _rope(k1_ref[:, ks], cq, sq),
                              _rope(k2_ref[:, ks], c2_ref[...], s2_ref[...])], axis=0).astype(BF16)
        vw = jnp.concatenate([v0_ref[:, ks], v1_ref[:, ks], v2_ref[:, ks]], axis=0).astype(BF16)
        kc = kc_ref[:, ks].astype(BF16)
        vc = vc_ref[:, ks].astype(BF16)
        for g in range(group):
            hs = slice((kh * group + g) * HD, (kh * group + g + 1) * HD)
            q = (_rope(q_ref[:, hs], cq, sq) * (scale * LOG2E)).astype(BF16)
            s_loc = jnp.where(valid, _dot_nt(q, kw), NEG)
            s_ctx = _dot_nt(q, kc)
            sink = sink_ref[kh * group + g] * LOG2E
            m = jnp.maximum(jnp.maximum(jnp.max(s_loc, axis=-1, keepdims=True),
                                        jnp.max(s_ctx, axis=-1, keepdims=True)), sink)
            p_loc = jnp.exp2(s_loc - m)
            p_ctx = jnp.exp2(s_ctx - m)
            den = (jnp.sum(p_loc, axis=-1, keepdims=True) + jnp.sum(p_ctx, axis=-1, keepdims=True)
                   + jnp.exp2(sink - m))
            o = _dot(p_ctx.astype(BF16), vc) + _dot(p_loc.astype(BF16), vw)
            o_ref[:, hs] = (o / den).astype(BF16)


def attn_a_lat(u_attn, cache_k, cache_v, sink, rope_c, rope_s, layer, batch, seq, row0, kvh, group, blk=256):
    nb = seq // blk
    rb0 = row0 // blk
    qw = kvh * group * HD
    kw = kvh * HD
    kcol = qw // kw
    vcol = kcol + 1
    past = cache_k.shape[2]

    def rows(b, n):
        return rb0 + b * nb + n

    prev = lambda n: jnp.maximum(n - 1, 0)
    nxt = lambda n: jnp.minimum(n + 1, nb - 1)
    tab = lambda f: pl.BlockSpec((blk, HD), lambda b, n: (f(n), 0))
    kv = lambda f, col: pl.BlockSpec((blk, kw), lambda b, n: (rows(b, f(n)), col))
    same = lambda n: n
    return pl.pallas_call(
        functools.partial(_attn_a_lat_body, group=group, kvh=kvh, seq=seq),
        grid=(batch, nb),
        in_specs=[pl.BlockSpec(memory_space=pltpu.SMEM),
                  pl.BlockSpec((blk, qw), lambda b, n: (rows(b, n), 0)),
                  kv(prev, kcol), kv(same, kcol), kv(nxt, kcol),
                  kv(prev, vcol), kv(same, vcol), kv(nxt, vcol),
                  pl.BlockSpec((None, None, past, kw), lambda b, n: (b, layer, 0, 0)),
                  pl.BlockSpec((None, None, past, kw), lambda b, n: (b, layer, 0, 0)),
                  tab(same), tab(same), tab(prev), tab(prev), tab(nxt), tab(nxt)],
        out_specs=pl.BlockSpec((blk, qw), lambda b, n: (b * nb + n, 0)),
        out_shape=jax.ShapeDtypeStruct((batch * seq, qw), BF16),
        compiler_params=_cp(("arbitrary", "arbitrary"), 40),
        name="attn_a_lat",
    )(sink, u_attn, u_attn, u_attn, u_attn, u_attn, u_attn, u_attn, cache_k, cache_v,
      rope_c, rope_s, rope_c, rope_s, rope_c, rope_s)


def _lambda(lam_ref, lam_init):
    lw = lam_ref[...]
    return (jnp.exp(jnp.sum(lw[0:1] * lw[1:2], axis=-1, keepdims=True))
            - jnp.exp(jnp.sum(lw[2:3] * lw[3:4], axis=-1, keepdims=True)) + lam_init)


def _softmax_parts(parts):
    m = functools.reduce(jnp.maximum, [jnp.max(s, axis=-1, keepdims=True) for s in parts])
    ps = [jnp.exp(s - m) for s in parts]
    inv = 1.0 / functools.reduce(lambda a, b: a + b, [jnp.sum(p, axis=-1, keepdims=True) for p in ps])
    return [p * inv for p in ps]


def _attn_b_ctx_body(lam_ref, g_ref, q_ref, k_ref, v_ref, o_ref, *, lam_init):
    hd = HD // 2
    scale = hd ** -0.5
    lam = _lambda(lam_ref, lam_init)
    for h in range(q_ref.shape[1] // HD):
        hs = slice(h * HD, (h + 1) * HD)
        q = q_ref[:, hs].astype(BF16)
        k = k_ref[:, hs].astype(BF16)
        p0, = _softmax_parts([_dot_nt(q[:, :hd], k[:, :hd]) * scale])
        p1, = _softmax_parts([_dot_nt(q[:, hd:], k[:, hd:]) * scale])
        o = _dot((p0 - lam * p1).astype(BF16), v_ref[:, hs].astype(BF16))
        o_ref[:, hs] = (_rms_lanes(o, g_ref[...]) * (1.0 - lam_init)).astype(BF16)


def attn_b_ctx(u_attn, b_lambda, b_subln, lam_init, batch, seq, heads, col0):
    w = heads * HD
    qcol = col0 // w
    return pl.pallas_call(
        functools.partial(_attn_b_ctx_body, lam_init=lam_init),
        grid=(batch,),
        in_specs=[pl.BlockSpec(b_lambda.shape, lambda b: (0, 0)),
                  pl.BlockSpec((1, HD), lambda b: (0, 0)),
                  pl.BlockSpec((seq, w), lambda b: (b, qcol)),
                  pl.BlockSpec((seq, w), lambda b: (b, qcol + 1)),
                  pl.BlockSpec((seq, w), lambda b: (b, qcol + 2))],
        out_specs=pl.BlockSpec((seq, w), lambda b: (b, 0)),
        out_shape=jax.ShapeDtypeStruct((batch * seq, w), BF16),
        compiler_params=_cp(("arbitrary",), 32),
        name="attn_b_ctx",
    )(b_lambda, b_subln.reshape(1, HD), u_attn, u_attn, u_attn)


def _attn_b_lat_body(lam_ref, g_ref, q_ref, k_ref, v_ref, kc_ref, vc_ref, cq_ref, sq_ref, ck_ref, sk_ref,
                     o_ref, k_scr, kc_scr, v_scr, *, lam_init):
    hd = HD // 2
    scale = hd ** -0.5
    nq = pl.program_id(2)

    @pl.when(nq == 0)
    def _():
        k_scr[...] = _rope(k_ref[...], ck_ref[...], sk_ref[...]).T.astype(BF16)
        kc_scr[...] = kc_ref[...].T.astype(BF16)
        v_scr[...] = v_ref[...].astype(BF16)

    lam = _lambda(lam_ref, lam_init)
    k = k_scr[...]
    v = v_scr[...]
    kc = kc_scr[...]
    vc = vc_ref[...].astype(BF16)
    q = _rope(q_ref[...], cq_ref[...], sq_ref[...]) * (scale * LOG2E)
    lane = lax.broadcasted_iota(jnp.int32, q.shape, 1)
    outs = []
    for c in range(2):
        qc = jnp.where((lane >= hd) == (c == 1), q, 0.0).astype(BF16)
        s_ctx = _dot(qc, kc)
        s_lat = _dot(qc, k)
        m = jnp.maximum(jnp.max(s_ctx, axis=-1, keepdims=True), jnp.max(s_lat, axis=-1, keepdims=True))
        p_ctx = jnp.exp2(s_ctx - m)
        p_lat = jnp.exp2(s_lat - m)
        den = jnp.sum(p_ctx, axis=-1, keepdims=True) + jnp.sum(p_lat, axis=-1, keepdims=True)
        outs.append((_dot(p_ctx.astype(BF16), vc) + _dot(p_lat.astype(BF16), v)) / den)
    o = outs[0] - lam * outs[1]
    o_ref[...] = (_rms_lanes(o, g_ref[...]) * (1.0 - lam_init)).astype(BF16)


def attn_b_lat(u_attn, cache_k, cache_v, b_lambda, b_subln, rope_c, rope_s, lam_init, layer,
               batch, seq, row0, heads, col0, tq=256):
    qcol = col0 // HD
    kcol = qcol + heads
    vcol = kcol + heads
    nq = seq // tq
    past = cache_k.shape[2]
    rq0 = row0 // tq
    rs0 = row0 // seq
    return pl.pallas_call(
        functools.partial(_attn_b_lat_body, lam_init=lam_init),
        grid=(batch, heads, nq),
        in_specs=[pl.BlockSpec(b_lambda.shape, lambda b, h, n: (0, 0)),
                  pl.BlockSpec((1, HD), lambda b, h, n: (0, 0)),
                  pl.BlockSpec((tq, HD), lambda b, h, n: (rq0 + b * nq + n, qcol + h)),
                  pl.BlockSpec((seq, HD), lambda b, h, n: (rs0 + b, kcol + h)),
                  pl.BlockSpec((seq, HD), lambda b, h, n: (rs0 + b, vcol + h)),
                  pl.BlockSpec((None, None, past, HD), lambda b, h, n: (b, layer, 0, h)),
                  pl.BlockSpec((None, None, past, HD), lambda b, h, n: (b, layer, 0, h)),
                  pl.BlockSpec((tq, HD), lambda b, h, n: (n, 0)),
                  pl.BlockSpec((tq, HD), lambda b, h, n: (n, 0)),
                  pl.BlockSpec((seq, HD), lambda b, h, n: (0, 0)),
                  pl.BlockSpec((seq, HD), lambda b, h, n: (0, 0))],
        out_specs=pl.BlockSpec((tq, HD), lambda b, h, n: (b * nq + n, h)),
        out_shape=jax.ShapeDtypeStruct((batch * seq, heads * HD), BF16),
        scratch_shapes=[pltpu.VMEM((HD, seq), BF16), pltpu.VMEM((HD, past), BF16), pltpu.VMEM((seq, HD), BF16)],
        compiler_params=_cp(("arbitrary", "arbitrary", "arbitrary"), 40),
        name="attn_b_lat",
    )(b_lambda, b_subln.reshape(1, HD), u_attn, u_attn, u_attn, cache_k, cache_v,
      rope_c, rope_s, rope_c, rope_s)


def dft_half_tables(l):
    h = l // 2
    r = jnp.arange(l, dtype=jnp.int32)[:, None]
    f = r % h
    quarter = jnp.where(r >= h, l, 0)
    j = jnp.arange(h, dtype=jnp.int32)[None, :]

    def table(pos):
        m = ((2 * f + 1) * pos + quarter) % (4 * l)
        t = jnp.cos(m.astype(F32) * (math.pi / (2 * l)))
        hi = t.astype(BF16)
        return hi, (t - hi.astype(F32)).astype(BF16)

    (te, te_lo), (to, to_lo) = table(2 * j), table(2 * j + 1)
    return te, te_lo, to, to_lo, te.T, to.T


def parity_order(l):
    return jnp.concatenate([jnp.arange(0, l, 2), jnp.arange(1, l, 2)])


def hyena_feats(l):
    t = jnp.linspace(0.0, 1.0, l, dtype=F32)[:, None]
    bands = (HY_EMB - 1) // 2
    w = 2.0 * math.pi * jnp.arange(l, dtype=F32)[:, None] / l
    fr = jnp.linspace(1e-4, bands - 1, bands, dtype=F32)[None, :]
    feats = jnp.concatenate([t, jnp.cos(fr * w), -jnp.sin(fr * w)], axis=-1)
    return jnp.pad(feats, ((0, 0), (0, HD - HY_EMB)))


def hyena_decay(l, width):
    t = jnp.linspace(0.0, 1.0, l, dtype=F32)[:, None]
    deltas = jnp.abs(jnp.linspace(math.log(HY_TARGET) / HY_SLOW, math.log(HY_TARGET) / HY_FAST, width, dtype=F32))
    return jnp.exp(-t * deltas)


def _hy_time_body(feat_ref, dec_ref, w1_ref, b1_ref, w2_ref, b2_ref, w3_ref, fr_ref, kh_ref, kl_ref, nrm_scr,
                  *, width, l):
    ps = pl.program_id(2)
    c = pl.program_id(3)
    fr = fr_ref[...]
    h = jnp.sin(fr[0:1] * (_dot3(feat_ref[...], w1_ref[...]) + b1_ref[...]))
    h = jnp.sin(fr[1:2] * (_dot3(h, w2_ref[...]) + b2_ref[...]))
    dec = dec_ref[...]
    row = lax.broadcasted_iota(jnp.int32, dec.shape, 0)
    fwd = _dot3(h, w3_ref[:, :width]) * dec
    bwd = jnp.where((row == 0) & (c == 0), 0.0, _dot3(h, w3_ref[:, width:]) * dec)

    @pl.when((ps == 0) & (c == 0))
    def _():
        nrm_scr[...] = jnp.zeros_like(nrm_scr)

    @pl.when(ps == 0)
    def _():
        nrm_scr[...] += (jnp.sum(jnp.abs(fwd), axis=0, keepdims=True)
                         + jnp.sum(jnp.abs(bwd), axis=0, keepdims=True))

    @pl.when(ps == 1)
    def _():
        inv = 1.0 / ((nrm_scr[...] + EPS) * l)
        for part, val in ((0, fwd * inv), (1, bwd * inv)):
            hi, lo = _split(val)
            kh_ref[:, part * width:(part + 1) * width] = hi
            kl_ref[:, part * width:(part + 1) * width] = lo


def _hy_spec_body(ter_h, tei_h, tor_h, toi_h, ter_l, tei_l, tor_l, toi_l, keh_ref, kel_ref, koh_ref, kol_ref,
                   o_ref, *, width):
    keh, kel, koh, kol = keh_ref[...], kel_ref[...], koh_ref[...], kol_ref[...]

    def dot3(th, tl, kh, kl):
        t = th[...]
        return _dot(t, kh) + _dot(t, kl) + _dot(tl[...], kh)

    p_re, p_im = dot3(ter_h, ter_l, keh, kel), dot3(tei_h, tei_l, keh, kel)
    q_re, q_im = dot3(tor_h, tor_l, koh, kol), dot3(toi_h, toi_l, koh, kol)
    o_ref[0] = (p_re + q_re)[:, :width] + (p_re + q_re)[:, width:]
    o_ref[1] = (p_im + q_im)[:, :width] - (p_im + q_im)[:, width:]
    o_ref[2] = (p_re - q_re)[:, :width] + (p_re - q_re)[:, width:]
    o_ref[3] = (q_im - p_im)[:, :width] - (q_im - p_im)[:, width:]


def hyena_filters(l, hy_w1p, hy_b1, hy_w2, hy_b2, hy_w3, hy_freq, tabs, width, rc=512, tr=256):
    te, te_lo, to, to_lo = tabs[:4]
    depth = hy_w3.shape[0]
    ffn_w = hy_w2.shape[1]
    half = l // 2
    rc = min(rc, l)
    tr = min(tr, half)
    nr = half // tr
    order = parity_order(l)
    feats = hyena_feats(l)[order]
    dec = hyena_decay(l, width)[order]
    wspec = lambda shape: pl.BlockSpec((None,) + shape, lambda d, o, p, c: (d, 0, 0))
    kspec = pl.BlockSpec((None, rc, 2 * width), lambda d, o, p, c: (d, c * p, o))
    k_hi, k_lo = pl.pallas_call(
        functools.partial(_hy_time_body, width=width, l=l),
        grid=(depth, 2, 2, l // rc),
        in_specs=[pl.BlockSpec((rc, HD), lambda d, o, p, c: (c, 0)),
                  pl.BlockSpec((rc, width), lambda d, o, p, c: (c, 0)),
                  wspec((HD, ffn_w)), wspec((1, ffn_w)), wspec((ffn_w, ffn_w)), wspec((1, ffn_w)),
                  pl.BlockSpec((None, ffn_w, 2 * width), lambda d, o, p, c: (d, 0, o)),
                  wspec((2, ffn_w))],
        out_specs=[kspec, kspec],
        out_shape=[jax.ShapeDtypeStruct((depth, l, 4 * width), BF16)] * 2,
        scratch_shapes=[pltpu.VMEM((1, width), F32)],
        compiler_params=_cp(("arbitrary",) * 4, 32),
        name="hyena_time_filters",
    )(feats, dec, hy_w1p, hy_b1.reshape(depth, 1, ffn_w), hy_w2, hy_b2.reshape(depth, 1, ffn_w), hy_w3, hy_freq)
    re_spec = pl.BlockSpec((tr, half), lambda d, o, r: (r, 0))
    im_spec = pl.BlockSpec((tr, half), lambda d, o, r: (nr + r, 0))
    even = pl.BlockSpec((None, half, 2 * width), lambda d, o, r: (d, 0, o))
    odd = pl.BlockSpec((None, half, 2 * width), lambda d, o, r: (d, 1, o))
    return pl.pallas_call(
        functools.partial(_hy_spec_body, width=width),
        grid=(depth, 2, nr),
        in_specs=[re_spec, im_spec, re_spec, im_spec, re_spec, im_spec, re_spec, im_spec, even, even, odd, odd],
        out_specs=pl.BlockSpec((None, None, 4, tr, width), lambda d, o, r: (d, o, 0, r, 0)),
        out_shape=jax.ShapeDtypeStruct((depth, 2, 4, half, width), F32),
        compiler_params=_cp(("arbitrary",) * 3, 48),
        name="hyena_spectra",
    )(te, te, to, to, te_lo, te_lo, to_lo, to_lo, k_hi, k_lo, k_hi, k_lo)


def _short_conv(u, w, b):
    l = u.shape[0]
    row = lax.broadcasted_iota(jnp.int32, u.shape, 0)
    up = jnp.where(row == 0, 0.0, pltpu.roll(u, 1, 0))
    dn = jnp.where(row == l - 1, 0.0, pltpu.roll(u, l - 1, 0))
    return up * w[0:1] + u * w[1:2] + dn * w[2:3] + b


def _hyena_body(hv_ref, h1_ref, h2_ref, wv_ref, w1_ref, w2_ref, bv_ref, b1_ref, b2_ref, bias_ref,
                ter_ref, tei_ref, tor_ref, toi_ref, ger_ref, gei_ref, gor_ref, goi_ref, k_ref, o_ref,
                z_scr, ze_scr, zo_scr, acc_scr, *, nf):
    ph = pl.program_id(1)
    j = pl.program_id(2)
    half = ze_scr.shape[0]
    even = pl.ds(0, half, stride=2)
    odd = pl.ds(1, half, stride=2)
    tiles = [(t, slice(t * HD, (t + 1) * HD)) for t in range(z_scr.shape[0])]
    gather = lambda scr: jnp.concatenate([scr[t] for t, _ in tiles], axis=1)

    def set_input(z):
        for t, cols in tiles:
            z_scr[t] = z[:, cols]
            ze_scr[:, cols] = z_scr[t, even, :].astype(BF16)
            zo_scr[:, cols] = z_scr[t, odd, :].astype(BF16)

    @pl.when((ph == 0) & (j == 0))
    def _():
        set_input(_short_conv(hv_ref[...].astype(F32), wv_ref[...], bv_ref[...]))

    @pl.when(j == 0)
    def _():
        acc_scr[...] = jnp.zeros_like(acc_scr)

    ze, zo = ze_scr[...], zo_scr[...]
    p_re, p_im = _dot(ter_ref[...], ze), _dot(tei_ref[...], ze)
    q_re, q_im = _dot(tor_ref[...], zo), _dot(toi_ref[...], zo)
    z1_re, z1_im, z2_re, z2_im = p_re + q_re, p_im + q_im, p_re - q_re, q_im - p_im
    k1_re, k1_im, k2_re, k2_im = k_ref[0], k_ref[1], k_ref[2], k_ref[3]
    y1_re = z1_re * k1_re - z1_im * k1_im
    y1_im = z1_re * k1_im + z1_im * k1_re
    y2_re = z2_re * k2_re - z2_im * k2_im
    y2_im = z2_re * k2_im + z2_im * k2_re
    y_even = (_dot(ger_ref[...], (y1_re + y2_re).astype(BF16))
              + _dot(gei_ref[...], (y1_im - y2_im).astype(BF16)))
    y_odd = (_dot(gor_ref[...], (y1_re - y2_re).astype(BF16))
             + _dot(goi_ref[...], (y1_im + y2_im).astype(BF16)))
    for t, cols in tiles:
        acc_scr[t, even, :] += y_even[:, cols]
        acc_scr[t, odd, :] += y_odd[:, cols]

    @pl.when((j == nf - 1) & (ph == 0))
    def _():
        x1 = _short_conv(h1_ref[...].astype(F32), w1_ref[...], b1_ref[...])
        set_input(x1 * (gather(acc_scr) + gather(z_scr) * bias_ref[0:1]))

    @pl.when((j == nf - 1) & (ph == 1))
    def _():
        x2 = _short_conv(h2_ref[...].astype(F32), w2_ref[...], b2_ref[...])
        o_ref[...] = (x2 * (gather(acc_scr) + gather(z_scr) * bias_ref[1:2])).astype(BF16)


def hyena(u_hy, conv_w, conv_b, bias, kf, tabs, layer, batch, seq, row0, width, tfq=256):
    te, _, to, _, ge, go = tabs
    half = seq // 2
    tfq = min(tfq, half)
    nf = half // tfq
    rb = row0 // seq
    hspec = lambda part: pl.BlockSpec((seq, width), lambda b, p, j: (rb + b, part))
    wspec = lambda part: pl.BlockSpec((3, width), lambda b, p, j: (0, part))
    bspec = lambda part: pl.BlockSpec((1, width), lambda b, p, j: (0, part))
    t_re = pl.BlockSpec((tfq, half), lambda b, p, j: (j, 0))
    t_im = pl.BlockSpec((tfq, half), lambda b, p, j: (nf + j, 0))
    g_re = pl.BlockSpec((half, tfq), lambda b, p, j: (0, j))
    g_im = pl.BlockSpec((half, tfq), lambda b, p, j: (0, nf + j))
    return pl.pallas_call(
        functools.partial(_hyena_body, nf=nf),
        grid=(batch, 2, nf),
        in_specs=[hspec(0), hspec(1), hspec(2), wspec(0), wspec(1), wspec(2), bspec(0), bspec(1), bspec(2),
                  pl.BlockSpec((2, width), lambda b, p, j: (0, 0)),
                  t_re, t_im, t_re, t_im, g_re, g_im, g_re, g_im,
                  pl.BlockSpec((None, None, 4, tfq, width), lambda b, p, j: (layer, p, 0, j, 0))],
        out_specs=pl.BlockSpec((seq, width), lambda b, p, j: (b, 0)),
        out_shape=jax.ShapeDtypeStruct((batch * seq, width), BF16),
        scratch_shapes=[pltpu.VMEM((width // HD, seq, HD), F32), pltpu.VMEM((half, width), BF16),
                        pltpu.VMEM((half, width), BF16), pltpu.VMEM((width // HD, seq, HD), F32)],
        compiler_params=_cp(("arbitrary",) * 3, 56),
        name="hyena",
    )(u_hy, u_hy, u_hy, conv_w, conv_w, conv_w, conv_b, conv_b, conv_b, bias,
      te, te, to, to, ge, ge, go, go, kf)


def _hgrn_pair_level(t, s, fwd):
    x = t ^ s
    lvl = jnp.where(x == 0, 0, 32 - lax.clz(x))
    used = (s <= t) if fwd else (s >= t)
    return jnp.where(used, lvl, -1)


def _hgrn_midpoint(b3, blk, fwd):
    rows = b3.shape[1]
    r0 = blk // 2 - 1 if fwd else blk // 2
    ref = b3[:, r0:r0 + 1, :]
    if rows > blk:
        pos = lax.broadcasted_iota(jnp.int32, b3.shape, 1)
        for j in range(1, rows // blk):
            ref = jnp.where(pos >= j * blk, b3[:, j * blk + r0:j * blk + r0 + 1, :], ref)
    return ref


def _hgrn_chunk(q, k, g, v, st, tri, lvl, fwd):
    c = CHUNK
    gh, gl = _split(g)
    b = _dot(tri, gh) + _dot(tri, gl)
    tot = b[c - 1:c] if fwd else b[0:1]
    vb = v.astype(BF16)
    o = _dot_nt((q * jnp.exp2(b)).astype(BF16), st.astype(BF16))
    kd = (k * jnp.exp2(tot - b)).astype(BF16)
    st_new = st * jnp.exp2(tot) + _dot_tn(vb, kd)

    a = jnp.where(lvl == 0, jnp.sum(q * k, axis=-1, keepdims=True), 0.0)
    a = jnp.where(lvl == 1, _dot_nt((q * jnp.exp2(g)).astype(BF16), k.astype(BF16)), a)
    blk, m = 4, 2
    while blk <= c:
        rows = max(blk, SUB)
        b3, q3, k3 = (x.reshape(c // rows, rows, HD) for x in (b, q, k))
        e = jnp.exp2(-jnp.abs(b3 - _hgrn_midpoint(b3, blk, fwd)))
        qt = (q3 * e).reshape(c, HD).astype(BF16)
        kt = (k3 * e).reshape(c, HD).astype(BF16)
        a = jnp.where(lvl == m, _dot_nt(qt, kt), a)
        blk, m = blk * 2, m + 1
    return o + _dot(a.astype(BF16), vb), st_new


def _hgrn_gate(z, lb):
    e = jnp.exp(-jnp.abs(z))
    r = 1.0 / (1.0 + e)
    pos = z >= 0
    sig_pos = jnp.where(pos, r, e * r)
    sig_neg = jnp.where(pos, e * r, r)
    f = sig_pos + jnp.maximum(lb, LB_FLOOR) * sig_neg
    return jnp.log(f) * LOG2E, (1.0 - lb) * sig_neg


def _hgrn_lower_bound(lb_ref, layer, depth, direction):
    x = lb_ref[...]
    rows = [x[2 * i + direction:2 * i + direction + 1] for i in range(depth)]
    m = functools.reduce(jnp.maximum, rows)
    es = [jnp.exp(r - m) for r in rows]
    tot = functools.reduce(lambda a, b: a + b, es)
    cum = jnp.zeros_like(tot)
    for i in range(1, layer + 1):
        cum = cum + es[i]
    return jnp.maximum(cum / tot, 0.0)


def _hgrn_body(*refs, layer, depth, has_state, emit_state):
    it = iter(refs)
    lb_ref, ng_ref, q_ref, ff_ref, fb_ref, i_ref, g_ref = (next(it) for _ in range(7))
    s0_ref = next(it) if has_state else None
    o_ref = next(it)
    so_ref = next(it) if emit_state else None
    of_scr, ob_scr, st_scr, tri_scr, lvl_scr = (next(it) for _ in range(5))
    nc = q_ref.shape[0] // CHUNK
    hp = q_ref.shape[1] // HD
    ti = lax.broadcasted_iota(jnp.int32, (CHUNK, CHUNK), 0)
    si = lax.broadcasted_iota(jnp.int32, (CHUNK, CHUNK), 1)
    chains = [(h, d) for h in range(hp) for d in range(2)]
    lbs = {}
    for d in range(2):
        tri_scr[d] = jnp.where((si <= ti) if d == 0 else (si >= ti), 1.0, 0.0).astype(BF16)
        lvl_scr[d] = _hgrn_pair_level(ti, si, d == 0)
        lb_all = _hgrn_lower_bound(lb_ref, layer, depth, d)
        for h in range(hp):
            lbs[h, d] = lb_all[:, h * HD:(h + 1) * HD]
            st_scr[2 * h + d] = s0_ref[d, h].T if has_state else jnp.zeros((HD, HD), F32)

    def step(i, carry):
        for h, d in chains:
            fwd = d == 0
            ci = i if fwd else nc - 1 - i
            sl = pl.ds(pl.multiple_of(ci * CHUNK, CHUNK), CHUNK)
            cs = slice(h * HD, (h + 1) * HD)
            xq = q_ref[sl, cs].astype(F32)
            log_f, kk = _hgrn_gate((ff_ref if fwd else fb_ref)[sl, cs].astype(F32), lbs[h, d])
            o, st = _hgrn_chunk(xq * _sigmoid(xq), kk, log_f, i_ref[sl, cs].astype(F32), st_scr[2 * h + d],
                                tri_scr[d], lvl_scr[d], fwd)
            st_scr[2 * h + d] = st
            (of_scr if fwd else ob_scr)[sl, cs] = o
        return carry

    lax.fori_loop(0, nc, step, 0, unroll=4)
    for h, d in chains:
        if emit_state:
            so_ref[d, h] = st_scr[2 * h + d].T
    for h in range(hp):
        cs = slice(h * HD, (h + 1) * HD)
        xg = g_ref[:, cs].astype(F32)
        o_ref[:, cs] = (_rms_lanes(of_scr[:, cs] + ob_scr[:, cs], ng_ref[...]) * (xg * _sigmoid(xg))).astype(BF16)


def hgrn(u_hg, hg_lb2, hg_norm, state, layer, depth, batch, seq, row0, heads, emit_state, col0=0, hp=4):
    rb = row0 // seq
    ng = heads // hp
    w = hp * HD
    cb = col0 // w
    part = lambda p: pl.BlockSpec((seq, w), lambda b, h: (rb + b, cb + p * ng + h))
    in_specs = [pl.BlockSpec((2 * depth, w), lambda b, h: (0, h)),
                pl.BlockSpec((1, HD), lambda b, h: (0, 0)),
                part(0), part(1), part(2), part(3), part(4)]
    args = [hg_lb2, hg_norm.reshape(1, HD), u_hg, u_hg, u_hg, u_hg, u_hg]
    if state is not None:
        in_specs.append(pl.BlockSpec((None, None, 2, hp, HD, HD), lambda b, h: (b, layer, 0, h, 0, 0)))
        args.append(state)
    out_specs = [pl.BlockSpec((seq, w), lambda b, h: (b, h))]
    out_shape = [jax.ShapeDtypeStruct((batch * seq, heads * HD), BF16)]
    if emit_state:
        out_specs.append(pl.BlockSpec((None, 2, hp, HD, HD), lambda b, h: (b, 0, h, 0, 0)))
        out_shape.append(jax.ShapeDtypeStruct((batch, 2, heads, HD, HD), F32))
    return pl.pallas_call(
        functools.partial(_hgrn_body, layer=layer, depth=depth, has_state=state is not None,
                          emit_state=emit_state),
        grid=(batch, ng),
        in_specs=in_specs,
        out_specs=out_specs,
        out_shape=out_shape,
        scratch_shapes=[pltpu.VMEM((seq, w), F32), pltpu.VMEM((seq, w), F32),
                        pltpu.VMEM((2 * hp, HD, HD), F32),
                        pltpu.VMEM((2, CHUNK, CHUNK), BF16), pltpu.VMEM((2, CHUNK, CHUNK), jnp.int32)],
        compiler_params=_cp(("arbitrary", "arbitrary"), 48),
        name="hgrn",
    )(*args)


def _merge_body(*refs, n_ctx_tiles):
    o_refs, (gt_ref, x_ref, mod_ref, wb_ref, wo_ref, out_ref) = refs[:2 * N_BRANCH], refs[2 * N_BRANCH:]
    d = x_ref.shape[1]
    is_ctx = pl.program_id(0) < n_ctx_tiles
    acc = None
    for n in range(N_BRANCH):
        o = jnp.where(is_ctx, o_refs[2 * n][...], o_refs[2 * n + 1][...])
        y = gt_ref[:, n * d:(n + 1) * d].astype(F32) * _dot(o, wb_ref[n])
        acc = y if acc is None else acc + y
    out = _dot(acc.astype(BF16), wo_ref[...])
    out_ref[...] = x_ref[...] + mod_ref[...][5:6] * out


def merge(branch_outs, gates, x, mod_l, w_branch, w_out, t_ctx, l_lat, tm=256):
    t, d = x.shape
    mw = w_branch.shape[1]
    nct = t_ctx // tm
    row = functools.partial(_mod_row, tm=tm, t_ctx=t_ctx, l_lat=l_lat)
    cspec = pl.BlockSpec((tm, mw), lambda i: (jnp.minimum(i, nct - 1), 0))
    lspec = pl.BlockSpec((tm, mw), lambda i: (jnp.maximum(i - nct, 0), 0))
    return pl.pallas_call(
        functools.partial(_merge_body, n_ctx_tiles=nct),
        grid=(t // tm,),
        in_specs=[cspec, lspec] * N_BRANCH + [
                  pl.BlockSpec((tm, N_BRANCH * d), lambda i: (i, 0)),
                  pl.BlockSpec((tm, d), lambda i: (i, 0)),
                  pl.BlockSpec((None, N_MOD, d), lambda i: (row(i), 0, 0)),
                  pl.BlockSpec((N_BRANCH, mw, d), lambda i: (0, 0, 0), pipeline_mode=pl.Buffered(1)),
                  pl.BlockSpec((d, d), lambda i: (0, 0), pipeline_mode=pl.Buffered(1))],
        out_specs=pl.BlockSpec((tm, d), lambda i: (i, 0)),
        out_shape=jax.ShapeDtypeStruct((t, d), F32),
        compiler_params=_cp(("arbitrary",), 56),
        name="merge",
    )(*[o for pair in branch_outs for o in pair], gates, x, mod_l, w_branch, w_out)


def _final_norm_body(x_ref, g_ref, oc_ref, ol_ref, *, n_ctx_tiles):
    i = pl.program_id(0)

    @pl.when(i < n_ctx_tiles)
    def _():
        oc_ref[...] = _rms_lanes(x_ref[...], g_ref[...])

    @pl.when(i >= n_ctx_tiles)
    def _():
        ol_ref[...] = _rms_lanes(x_ref[...], g_ref[...])


def final_norm(x, g, t_ctx, tm=512):
    t, d = x.shape
    nct = t_ctx // tm
    return pl.pallas_call(
        functools.partial(_final_norm_body, n_ctx_tiles=nct),
        grid=(t // tm,),
        in_specs=[pl.BlockSpec((tm, d), lambda i: (i, 0)), pl.BlockSpec((1, d), lambda i: (0, 0))],
        out_specs=[pl.BlockSpec((tm, d), lambda i: (jnp.minimum(i, nct - 1), 0)),
                   pl.BlockSpec((tm, d), lambda i: (jnp.maximum(i - nct, 0), 0))],
        out_shape=[jax.ShapeDtypeStruct((t_ctx, d), F32), jax.ShapeDtypeStruct((t - t_ctx, d), F32)],
        compiler_params=_cp(("arbitrary",), 32),
        name="final_norm",
    )(x, g.reshape(1, d))


def kernel(x_prompt, x_sample, c, cache_a_k, cache_a_v, cache_b_k, cache_b_v, state_hgrn, c_ctx, w_mod, b_mod, norm_g, w_ffn1_gu, w_ffn1_down, w_ffn2_gu, w_ffn2_down, w_in, w_branch, w_out, a_sink, b_lambda, b_subln, hy_conv_w, hy_conv_b, hy_w1, hy_b1, hy_w2, hy_b2, hy_w3, hy_freq, hy_bias, hg_lb, hg_norm, final_g):
    batch, seq, d = x_prompt.shape
    dec_batch, dec_seq, _ = x_sample.shape
    depth = w_mod.shape[0]
    mix_w = w_branch.shape[2]
    a_heads = a_sink.shape[1]
    a_kvh = cache_a_k.shape[3]
    a_group = a_heads // a_kvh
    b_heads = cache_b_k.shape[3]
    hg_heads = state_hgrn.shape[3]
    t_ctx, t_lat = batch * seq, dec_batch * dec_seq

    n_attn = (a_heads + 2 * a_kvh + 3 * b_heads) * HD
    b_col0 = (a_heads + 2 * a_kvh) * HD

    x = jnp.concatenate([x_prompt.reshape(t_ctx, d), x_sample.reshape(t_lat, d)], axis=0)
    cond = jnp.concatenate([c_ctx[None, :], c], axis=0)
    cond = jnp.pad(cond, ((0, MOD_ROWS - cond.shape[0]), (0, 0)))
    mod = modulation(cond, w_mod, b_mod).reshape(depth, MOD_ROWS, N_MOD, d)

    rope_a = rope_tables(dec_seq, HD, 1)
    rope_b = rope_tables(dec_seq, HD // 2, 2)
    tabs_ctx = dft_half_tables(seq)
    tabs_lat = dft_half_tables(dec_seq)
    w1p = jnp.pad(hy_w1, ((0, 0), (0, HD - hy_w1.shape[1]), (0, 0)))
    kf_ctx = hyena_filters(seq, w1p, hy_b1, hy_w2, hy_b2, hy_w3, hy_freq, tabs_ctx, mix_w)
    kf_lat = hyena_filters(dec_seq, w1p, hy_b1, hy_w2, hy_b2, hy_w3, hy_freq, tabs_lat, mix_w)

    cak = cache_a_k.reshape(dec_batch, depth, -1, a_kvh * HD)
    cav = cache_a_v.reshape(dec_batch, depth, -1, a_kvh * HD)
    cbk = cache_b_k.reshape(dec_batch, depth, -1, b_heads * HD)
    cbv = cache_b_v.reshape(dec_batch, depth, -1, b_heads * HD)
    hg_lb2 = hg_lb.reshape(depth * 2, mix_w)

    ak_l, av_l, bk_l, bv_l, st_l = [], [], [], [], []
    for l in range(depth):
        lam_init = 0.8 - 0.6 * math.exp(-0.3 * l)
        mod_l = mod[l]
        x, h_mix = ffn(x, mod_l, norm_g[l, 0:1], w_ffn1_gu[l].astype(BF16), w_ffn1_down[l].astype(BF16), 0,
                       t_ctx, dec_seq, next_g_row=norm_g[l, 1:2])
        n_rec = n_attn + 8 * mix_w
        k_a, k_b = a_heads * HD // PROJ_TILE, (b_col0 + b_heads * HD) // PROJ_TILE
        u_attn, kv = proj_with_context_kv(h_mix, w_in[l, :, :n_attn].astype(BF16), (k_a, k_b, k_b + 1), t_ctx)
        u_rec = proj(h_mix, w_in[l, :, n_attn:n_rec].astype(BF16), BF16, False)
        gates = proj(h_mix, w_in[l, :, n_rec:].astype(BF16), BF16, True)

        c0, c1 = a_kvh * HD, 2 * a_kvh * HD
        c2 = c1 + b_heads * HD
        ak_l.append(kv[:, :c0].reshape(batch, seq, a_kvh, HD))
        av_l.append(kv[:, c0:c1].reshape(batch, seq, a_kvh, HD))
        bk_l.append(kv[:, c1:c2].reshape(batch, seq, b_heads, 2, HD // 2))
        bv_l.append(kv[:, c2:].reshape(batch, seq, b_heads, HD))

        oa_c = attn_a_ctx(u_attn, a_sink[l], batch, seq, a_kvh, a_group)
        oa_s = attn_a_lat(u_attn, cak, cav, a_sink[l], rope_a[0], rope_a[1], l, dec_batch, dec_seq, t_ctx,
                          a_kvh, a_group)
        ob_c = attn_b_ctx(u_attn, b_lambda[l], b_subln[l], lam_init, batch, seq, b_heads, b_col0)
        ob_s = attn_b_lat(u_attn, cbk, cbv, b_lambda[l], b_subln[l], rope_b[0], rope_b[1], lam_init, l,
                          dec_batch, dec_seq, t_ctx, b_heads, b_col0)
        cb = hy_conv_b[l].reshape(1, -1)
        oc_c = hyena(u_rec, hy_conv_w[l], cb, hy_bias[l], kf_ctx, tabs_ctx, l, batch, seq, 0, mix_w)
        oc_s = hyena(u_rec, hy_conv_w[l], cb, hy_bias[l], kf_lat, tabs_lat, l, dec_batch, dec_seq, t_ctx, mix_w)
        od_c, st = hgrn(u_rec, hg_lb2, hg_norm[l], None, l, depth, batch, seq, 0, hg_heads, True, col0=3 * mix_w)
        od_s, = hgrn(u_rec, hg_lb2, hg_norm[l], state_hgrn, l, depth, dec_batch, dec_seq, t_ctx, hg_heads, False,
                     col0=3 * mix_w)
        st_l.append(st)

        x = merge(((oa_c, oa_s), (ob_c, ob_s), (oc_c, oc_s), (od_c, od_s)), gates, x, mod_l,
                  w_branch[l].astype(BF16), w_out[l].astype(BF16), t_ctx, dec_seq)
        x = ffn(x, mod_l, norm_g[l, 2:3], w_ffn2_gu[l].astype(BF16), w_ffn2_down[l].astype(BF16), 2,
                t_ctx, dec_seq)

    y_ctx, y_lat = final_norm(x, final_g, t_ctx)
    return (y_ctx.reshape(batch, seq, d), y_lat.reshape(dec_batch, dec_seq, d),
            jnp.stack(ak_l, axis=1), jnp.stack(av_l, axis=1), jnp.stack(bk_l, axis=1), jnp.stack(bv_l, axis=1),
            jnp.stack(st_l, axis=1))
```
